```python
import jax, jax.numpy as jnp
from jax import lax
import numpy as np

D_MODEL = 1024
BATCH = 2
SEQ = 16384
DEPTH = 1
DEC_BATCH = 16
DEC_SEQ = 4096
PAST_LEN = 128

GRID_W = 64
NA_HEADS = 8
NA_HEAD_DIM = 64
NA_WIDTH = NA_HEADS * NA_HEAD_DIM
NA_KH_MAX = 8
NA_KW = 16
NA_SLAB_W = 2 * NA_KW
MLA_HEADS = 8
MLA_NOPE_DIM = 64
MLA_ROPE_DIM = 32
MLA_QK_DIM = MLA_NOPE_DIM + MLA_ROPE_DIM
MLA_V_DIM = 64
MLA_Q_RANK = 384
MLA_KV_RANK = 256
MLA_WIDTH = MLA_HEADS * MLA_V_DIM
ROPE_BASE = 10000.0
Q_BLOCK = 128
D_FF = 2816
LN_EPS = 1e-5
RMS_EPS = 1e-6
IN_COLS = 3 * NA_WIDTH + MLA_Q_RANK + MLA_KV_RANK + MLA_ROPE_DIM + 2 * D_MODEL

kernel_name = "hybrid_na_mla_macaron_deepnorm_encoder"


def layer_norm(x, g, b):
    xf = x.astype(jnp.float32)
    mu = jnp.mean(xf, axis=-1, keepdims=True)
    var = jnp.mean(jnp.square(xf - mu), axis=-1, keepdims=True)
    return ((xf - mu) * lax.rsqrt(var + LN_EPS) * g + b).astype(x.dtype)


def rms_norm(x, g):
    xf = x.astype(jnp.float32)
    return (xf * lax.rsqrt(jnp.mean(jnp.square(xf), axis=-1, keepdims=True) + RMS_EPS) * g).astype(x.dtype)


def swiglu_ffn(x, w_in, w_out):
    a, u = jnp.split(x @ w_in, 2, axis=-1)
    return (jax.nn.silu(a) * u) @ w_out


def rope_tables(n):
    inv = 1.0 / (ROPE_BASE ** (jnp.arange(0, MLA_ROPE_DIM, 2, dtype=jnp.float32) / MLA_ROPE_DIM))
    ang = jnp.arange(n, dtype=jnp.float32)[:, None] * inv[None, :]
    return jnp.cos(ang), jnp.sin(ang)


def apply_rope(x, cos, sin):
    x1, x2 = jnp.split(x.astype(jnp.float32), 2, axis=-1)
    return jnp.concatenate([x1 * cos - x2 * sin, x1 * sin + x2 * cos], axis=-1).astype(x.dtype)


def neighbourhood_attention(q, k, v, rpb):
    b, n = q.shape[0], q.shape[1]
    rows = n // GRID_W
    kh = min(NA_KH_MAX, rows)
    ncb = GRID_W // NA_KW
    qg = q.reshape(b, rows, ncb, NA_KW, NA_HEADS, NA_HEAD_DIM)
    kg = k.reshape(b, rows, GRID_W, NA_HEADS, NA_HEAD_DIM)
    vg = v.reshape(b, rows, GRID_W, NA_HEADS, NA_HEAD_DIM)
    r = jnp.arange(rows)
    row_start = jnp.clip(r - kh // 2, 0, rows - kh)
    key_rows = row_start[:, None] + jnp.arange(kh)[None, :]
    j = jnp.arange(ncb)
    slab_start = jnp.clip(j * NA_KW - NA_KW // 2, 0, GRID_W - NA_SLAB_W)
    key_cols = slab_start[:, None] + jnp.arange(NA_SLAB_W)[None, :]
    ridx = key_rows[:, None, :, None]
    cidx = key_cols[None, :, None, :]
    k_slab = kg[:, ridx, cidx]
    v_slab = vg[:, ridx, cidx]
    q_cols = j[:, None] * NA_KW + jnp.arange(NA_KW)[None, :]
    win_start = jnp.clip(q_cols - NA_KW // 2, 0, GRID_W - NA_KW)
    kc = key_cols[:, None, :]
    in_win = (kc >= win_start[..., None]) & (kc < win_start[..., None] + NA_KW)
    dc = jnp.clip(kc - q_cols[:, :, None] + NA_KW - 1, 0, 2 * NA_KW - 2)
    dr = key_rows - r[:, None] + NA_KH_MAX - 1
    bias = rpb[:, dr[:, None, None, :, None], dc[None, :, :, None, :]]
    s = jnp.einsum('brjqhd,brjkwhd->bhrjqkw', qg, k_slab,
                   preferred_element_type=jnp.float32) * (NA_HEAD_DIM ** -0.5)
    s = s + bias[None].astype(jnp.float32)
    s = jnp.where(in_win[:, :, None, :], s, -jnp.inf)
    p = jax.nn.softmax(s, axis=(-2, -1))
    out = jnp.einsum('bhrjqkw,brjkwhd->brjqhd', p.astype(v.dtype), v_slab)
    return out.reshape(b, n, NA_WIDTH)


def latent_attention(c_q, c_kv, k_rope, q_norm_g, kv_norm_g, w_uq, w_ukv):
    b, n = c_q.shape[0], c_q.shape[1]
    q = (rms_norm(c_q, q_norm_g) @ w_uq).reshape(b, n, MLA_HEADS, MLA_QK_DIM)
    kv = (rms_norm(c_kv, kv_norm_g) @ w_ukv).reshape(b, n, MLA_HEADS, MLA_NOPE_DIM + MLA_V_DIM)
    q_nope, q_rope = q[..., :MLA_NOPE_DIM], q[..., MLA_NOPE_DIM:]
    k_nope, v = kv[..., :MLA_NOPE_DIM], kv[..., MLA_NOPE_DIM:]
    cos, sin = rope_tables(n)
    q_rope = apply_rope(q_rope, cos[:, None, :], sin[:, None, :])
    k_rope = apply_rope(k_rope, cos, sin)
    nb = n // Q_BLOCK
    qn_blocks = q_nope.reshape(b, nb, Q_BLOCK, MLA_HEADS, MLA_NOPE_DIM).transpose(1, 0, 2, 3, 4)
    qr_blocks = q_rope.reshape(b, nb, Q_BLOCK, MLA_HEADS, MLA_ROPE_DIM).transpose(1, 0, 2, 3, 4)
    scale = MLA_QK_DIM ** -0.5

    def attend(blk):
        qn, qr = blk
        s = (jnp.einsum('bqhd,bkhd->bhqk', qn, k_nope, preferred_element_type=jnp.float32)
             + jnp.einsum('bqhr,bkr->bhqk', qr, k_rope, preferred_element_type=jnp.float32)) * scale
        p = jax.nn.softmax(s, axis=-1)
        return jnp.einsum('bhqk,bkhd->bqhd', p.astype(v.dtype), v)

    out = lax.map(attend, (qn_blocks, qr_blocks))
    return out.transpose(1, 0, 2, 3, 4).reshape(b, n, MLA_WIDTH)


def token_mixer(x, w_in, b_gate, na_rpb, q_norm_g, kv_norm_g, w_uq, w_ukv, w_na_o, w_mla_o, w_out):
    offs = np.cumsum([NA_WIDTH, NA_WIDTH, NA_WIDTH, MLA_Q_RANK, MLA_KV_RANK, MLA_ROPE_DIM]).tolist()
    h = x @ w_in
    na_q, na_k, na_v, c_q, c_kv, k_rope, gate_logits = jnp.split(h, offs, axis=-1)
    b, n = x.shape[0], x.shape[1]
    hs = (b, n, NA_HEADS, NA_HEAD_DIM)
    y_a = neighbourhood_attention(na_q.reshape(hs), na_k.reshape(hs), na_v.reshape(hs), na_rpb) @ w_na_o
    y_b = latent_attention(c_q, c_kv, k_rope, q_norm_g, kv_norm_g, w_uq, w_ukv) @ w_mla_o
    g = jax.nn.sigmoid((gate_logits + b_gate).astype(jnp.float32)).astype(x.dtype)
    g_a, g_b = g[..., :D_MODEL], g[..., D_MODEL:]
    return (g_a * y_a + g_b * y_b) @ w_out


def encoder_layer(x, ffn1_w_in, ffn1_w_out, ln1_g, ln1_b, w_in, b_gate, na_rpb, q_norm_g, kv_norm_g,
                  w_uq, w_ukv, w_na_o, w_mla_o, w_out, ln2_g, ln2_b, ffn2_w_in, ffn2_w_out, ln3_g, ln3_b):
    alpha = (2.0 * DEPTH) ** 0.25
    x = layer_norm(alpha * x + 0.5 * swiglu_ffn(x, ffn1_w_in, ffn1_w_out), ln1_g, ln1_b)
    x = layer_norm(alpha * x + token_mixer(x, w_in, b_gate, na_rpb, q_norm_g, kv_norm_g, w_uq, w_ukv,
                                           w_na_o, w_mla_o, w_out), ln2_g, ln2_b)
    x = layer_norm(alpha * x + 0.5 * swiglu_ffn(x, ffn2_w_in, ffn2_w_out), ln3_g, ln3_b)
    return x


def run_trunk(x, ffn1_w_in, ffn1_w_out, ln1_g, ln1_b, w_in, b_gate, na_rpb, q_norm_g, kv_norm_g,
              w_uq, w_ukv, w_na_o, w_mla_o, w_out, ln2_g, ln2_b, ffn2_w_in, ffn2_w_out, ln3_g, ln3_b):
    for l in range(DEPTH):
        x = encoder_layer(x, ffn1_w_in[l], ffn1_w_out[l], ln1_g[l], ln1_b[l], w_in[l], b_gate[l], na_rpb[l],
                          q_norm_g[l], kv_norm_g[l], w_uq[l], w_ukv[l], w_na_o[l], w_mla_o[l], w_out[l],
                          ln2_g[l], ln2_b[l], ffn2_w_in[l], ffn2_w_out[l], ln3_g[l], ln3_b[l])
    return x


def setup_inputs(seed: int = 0) -> dict:
    key = jax.random.key(seed)
    ks = jax.random.split(key, 24)
    beta = (8.0 * DEPTH) ** -0.25
    f32 = jnp.float32

    def dense(k, shape, fan_in, scale=1.0):
        return jax.random.normal(k, shape, f32) * (fan_in ** -0.5) * scale

    def gain(k, shape):
        return 1.0 + 0.05 * jax.random.normal(k, shape, f32)

    def small(k, shape, s):
        return s * jax.random.normal(k, shape, f32)

    in_col_scale = jnp.concatenate([jnp.ones((2 * NA_WIDTH,), f32), jnp.full((NA_WIDTH,), beta, f32),
                                    jnp.ones((IN_COLS - 3 * NA_WIDTH,), f32)])
    ukv_col_scale = jnp.tile(jnp.concatenate([jnp.ones((MLA_NOPE_DIM,), f32),
                                              jnp.full((MLA_V_DIM,), beta, f32)]), MLA_HEADS)
    return {
        "x_prompt": jax.random.normal(ks[0], (BATCH, SEQ, D_MODEL), f32),
        "x_sample": jax.random.normal(ks[1], (DEC_BATCH, DEC_SEQ, D_MODEL), f32),
        "ffn1_w_in": dense(ks[2], (DEPTH, D_MODEL, 2 * D_FF), D_MODEL),
        "ffn1_w_out": dense(ks[3], (DEPTH, D_FF, D_MODEL), D_FF, beta),
        "ln1_g": gain(ks[4], (DEPTH, D_MODEL)),
        "ln1_b": small(ks[5], (DEPTH, D_MODEL), 0.02),
        "w_in": dense(ks[6], (DEPTH, D_MODEL, IN_COLS), D_MODEL) * in_col_scale,
        "b_gate": small(ks[7], (DEPTH, 2 * D_MODEL), 0.1),
        "na_rpb": small(ks[8], (DEPTH, NA_HEADS, 2 * NA_KH_MAX - 1, 2 * NA_KW - 1), 0.02),
        "q_norm_g": gain(ks[9], (DEPTH, MLA_Q_RANK)),
        "kv_norm_g": gain(ks[10], (DEPTH, MLA_KV_RANK)),
        "w_uq": dense(ks[11], (DEPTH, MLA_Q_RANK, MLA_HEADS * MLA_QK_DIM), MLA_Q_RANK),
        "w_ukv": dense(ks[12], (DEPTH, MLA_KV_RANK, MLA_HEADS * (MLA_NOPE_DIM + MLA_V_DIM)), MLA_KV_RANK) * ukv_col_scale,
        "w_na_o": dense(ks[13], (DEPTH, NA_WIDTH, D_MODEL), NA_WIDTH),
        "w_mla_o": dense(ks[14], (DEPTH, MLA_WIDTH, D_MODEL), MLA_WIDTH),
        "w_out": dense(ks[15], (DEPTH, D_MODEL, D_MODEL), D_MODEL, beta),
        "ln2_g": gain(ks[16], (DEPTH, D_MODEL)),
        "ln2_b": small(ks[17], (DEPTH, D_MODEL), 0.02),
        "ffn2_w_in": dense(ks[18], (DEPTH, D_MODEL, 2 * D_FF), D_MODEL),
        "ffn2_w_out": dense(ks[19], (DEPTH, D_FF, D_MODEL), D_FF, beta),
        "ln3_g": gain(ks[20], (DEPTH, D_MODEL)),
        "ln3_b": small(ks[21], (DEPTH, D_MODEL), 0.02),
    }


def reference(x_prompt, x_sample, ffn1_w_in, ffn1_w_out, ln1_g, ln1_b, w_in, b_gate, na_rpb, q_norm_g,
              kv_norm_g, w_uq, w_ukv, w_na_o, w_mla_o, w_out, ln2_g, ln2_b, ffn2_w_in, ffn2_w_out,
              ln3_g, ln3_b):
    y_prompt = run_trunk(x_prompt, ffn1_w_in, ffn1_w_out, ln1_g, ln1_b, w_in, b_gate, na_rpb, q_norm_g,
                         kv_norm_g, w_uq, w_ukv, w_na_o, w_mla_o, w_out, ln2_g, ln2_b, ffn2_w_in,
                         ffn2_w_out, ln3_g, ln3_b)
    y_sample = run_trunk(x_sample, ffn1_w_in, ffn1_w_out, ln1_g, ln1_b, w_in, b_gate, na_rpb, q_norm_g,
                         kv_norm_g, w_uq, w_ukv, w_na_o, w_mla_o, w_out, ln2_g, ln2_b, ffn2_w_in,
                         ffn2_w_out, ln3_g, ln3_b)
    return (y_prompt, y_sample)
```

```python
import functools
import math

import numpy as np
import jax
import jax.numpy as jnp
from jax import lax
from jax.experimental import pallas as pl
from jax.experimental.pallas import tpu as pltpu

F32 = jnp.float32
BF16 = jnp.bfloat16

DEPTH = 1
GRID_W = 64
NA_HEADS = 8
NA_HEAD_DIM = 64
NA_WIDTH = NA_HEADS * NA_HEAD_DIM
NA_KH = 8
NA_KW = 16
MLA_HEADS = 8
MLA_NOPE = 64
MLA_ROPE = 32
MLA_QK = MLA_NOPE + MLA_ROPE
MLA_V = 64
MLA_Q_RANK = 384
MLA_KV_RANK = 256
ROPE_BASE = 10000.0
LN_EPS = 1e-5
RMS_EPS = 1e-6
ALPHA = (2.0 * DEPTH) ** 0.25
LOG2E = math.log2(math.e)

LANES = 128
QK_PAD = 128
VMEM_LIMIT = 56 * 1024 * 1024

TOKEN_TILE = 512
FFN_CHUNK = 256
NA_ROWS_PER_STEP = 16
MLA_TQ = 512
MLA_TK = 512
NEG_BIG = -1e30


def _const_spec(shape):
    nd = len(shape)
    return pl.BlockSpec(shape, lambda *_: (0,) * nd, pipeline_mode=pl.Buffered(1))


def _layer_norm(y, g, b):
    mu = jnp.mean(y, axis=-1, keepdims=True)
    d = y - mu
    var = jnp.mean(d * d, axis=-1, keepdims=True)
    return d * lax.rsqrt(var + LN_EPS) * g + b


def _rms_norm(y, g):
    return y * lax.rsqrt(jnp.mean(y * y, axis=-1, keepdims=True) + RMS_EPS) * g


def _dot(a, b):
    return jnp.dot(a, b, preferred_element_type=F32)


def _swiglu_ln(x, w_in_ref, w_out_ref, g_ref, b_ref, h_ref, d_ff):
    xb = x.astype(BF16)
    for c in range(d_ff // FFN_CHUNK):
        lo, hi = c * FFN_CHUNK, (c + 1) * FFN_CHUNK
        a = _dot(xb, w_in_ref[:, lo:hi])
        u = _dot(xb, w_in_ref[:, d_ff + lo:d_ff + hi])
        h_ref[:, lo:hi] = (a * jax.nn.sigmoid(a) * u).astype(BF16)
    y = _dot(h_ref[...], w_out_ref[...])
    return _layer_norm(ALPHA * x + 0.5 * y, g_ref[...], b_ref[...])


def _ffn_ln_kernel(x_ref, w_in_ref, w_out_ref, g_ref, b_ref, o_ref, h_ref, *, d_ff):
    o_ref[...] = _swiglu_ln(x_ref[...], w_in_ref, w_out_ref, g_ref, b_ref, h_ref, d_ff)


def ffn_ln(x, w_in, w_out, g, b):
    t, d = x.shape
    d_ff = w_out.shape[0]
    tm = TOKEN_TILE
    return pl.pallas_call(
        functools.partial(_ffn_ln_kernel, d_ff=d_ff),
        out_shape=jax.ShapeDtypeStruct((t, d), F32),
        grid=(t // tm,),
        in_specs=[pl.BlockSpec((tm, d), lambda i: (i, 0)),
                  _const_spec(w_in.shape), _const_spec(w_out.shape),
                  _const_spec(g.shape), _const_spec(b.shape)],
        out_specs=pl.BlockSpec((tm, d), lambda i: (i, 0)),
        scratch_shapes=[pltpu.VMEM((tm, d_ff), BF16)],
        compiler_params=pltpu.CompilerParams(dimension_semantics=("arbitrary",),
                                             vmem_limit_bytes=VMEM_LIMIT),
        name="ffn_ln",
    )(x, w_in, w_out, g, b)


_C_NAQ, _C_NAK, _C_NAV = 0, NA_WIDTH, 2 * NA_WIDTH
_C_CQ = 3 * NA_WIDTH
_C_CKV = _C_CQ + MLA_Q_RANK
_C_KR = _C_CKV + MLA_KV_RANK
_C_GATE = _C_KR + 2 * QK_PAD


def _mixer_in_kernel(x_ref, w_ref, bg_ref, qg_ref, kvg_ref, wk_ref, wvt_ref, wqt_ref, wqs_ref,
                     ck_ref, sk_ref, cq_ref, sq_ref,
                     naq_ref, nak_ref, nav_ref, qt_ref, k_ref, vt_ref, ga_ref, gb_ref, *, d_model):
    xb = x_ref[...].astype(BF16)
    naq_ref[...] = (_dot(xb, w_ref[:, _C_NAQ:_C_NAK]) * (NA_HEAD_DIM ** -0.5)).astype(BF16)
    nak_ref[...] = _dot(xb, w_ref[:, _C_NAK:_C_NAV]).astype(BF16)
    nav_ref[...] = _dot(xb, w_ref[:, _C_NAV:_C_CQ]).astype(BF16)

    cqn = _rms_norm(_dot(xb, w_ref[:, _C_CQ:_C_CKV]), qg_ref[...])
    ckvn = _rms_norm(_dot(xb, w_ref[:, _C_CKV:_C_KR]), kvg_ref[...])

    kr = _dot(xb, w_ref[:, _C_KR:_C_GATE])
    kr_blk = kr[:, :QK_PAD] * ck_ref[...] + kr[:, QK_PAD:] * sk_ref[...]
    kall = _dot(ckvn.astype(BF16), wk_ref[...])
    for h in range(MLA_HEADS):
        k_ref[h] = (kall[:, h * QK_PAD:(h + 1) * QK_PAD] + kr_blk).astype(BF16)

    ckvn_t = ckvn.T.astype(BF16)
    vt = _dot(wvt_ref[...], ckvn_t)
    for h in range(MLA_HEADS):
        vt_ref[h] = vt[h * MLA_V:(h + 1) * MLA_V].astype(BF16)

    cqn_t = cqn.T.astype(BF16)
    q_scale = (MLA_QK ** -0.5) * LOG2E
    qt = _dot(wqt_ref[...], cqn_t) * q_scale
    qs = _dot(wqs_ref[...], cqn_t) * q_scale
    cq, sq = cq_ref[...], sq_ref[...]
    zeros = jnp.zeros((QK_PAD - MLA_QK, qt.shape[1]), BF16)
    for h in range(MLA_HEADS):
        base = h * QK_PAD
        qt_ref[h, 0:MLA_NOPE] = qt[base:base + MLA_NOPE].astype(BF16)
        rope = qt[base + MLA_NOPE:base + MLA_QK] * cq + qs[h * MLA_ROPE:(h + 1) * MLA_ROPE] * sq
        qt_ref[h, MLA_NOPE:MLA_QK] = rope.astype(BF16)
        qt_ref[h, MLA_QK:QK_PAD] = zeros

    for half, out_ref in enumerate((ga_ref, gb_ref)):
        lo = _C_GATE + half * d_model
        logits = _dot(xb, w_ref[:, lo:lo + d_model]) + bg_ref[:, half * d_model:(half + 1) * d_model]
        out_ref[...] = jax.nn.sigmoid(logits).astype(BF16)


def mixer_in(x, wts, tabs):
    b, n, d = x.shape
    tm = TOKEN_TILE
    h = MLA_HEADS
    tok = lambda c: pl.BlockSpec((None, tm, c), lambda bi, i: (bi, i, 0))
    out_shape = (
        jax.ShapeDtypeStruct((b, n, NA_WIDTH), BF16), jax.ShapeDtypeStruct((b, n, NA_WIDTH), BF16),
        jax.ShapeDtypeStruct((b, n, NA_WIDTH), BF16),
        jax.ShapeDtypeStruct((b, h, QK_PAD, n), BF16),
        jax.ShapeDtypeStruct((b, h, n, QK_PAD), BF16),
        jax.ShapeDtypeStruct((b, h, MLA_V, n), BF16),
        jax.ShapeDtypeStruct((b, n, d), BF16), jax.ShapeDtypeStruct((b, n, d), BF16),
    )
    out_specs = (
        tok(NA_WIDTH), tok(NA_WIDTH), tok(NA_WIDTH),
        pl.BlockSpec((None, h, QK_PAD, tm), lambda bi, i: (bi, 0, 0, i)),
        pl.BlockSpec((None, h, tm, QK_PAD), lambda bi, i: (bi, 0, i, 0)),
        pl.BlockSpec((None, h, MLA_V, tm), lambda bi, i: (bi, 0, 0, i)),
        tok(d), tok(d),
    )
    consts = [wts["w_in"], wts["b_gate"], wts["q_norm_g"], wts["kv_norm_g"],
              wts["wk"], wts["wvt"], wts["wqt"], wts["wqs"]]
    in_specs = ([tok(d)] + [_const_spec(c.shape) for c in consts] + [
        pl.BlockSpec((tm, QK_PAD), lambda bi, i: (i, 0)),
        pl.BlockSpec((tm, QK_PAD), lambda bi, i: (i, 0)),
        pl.BlockSpec((MLA_ROPE, tm), lambda bi, i: (0, i)),
        pl.BlockSpec((MLA_ROPE, tm), lambda bi, i: (0, i)),
    ])
    return pl.pallas_call(
        functools.partial(_mixer_in_kernel, d_model=d),
        out_shape=out_shape,
        grid=(b, n // tm),
        in_specs=in_specs,
        out_specs=out_specs,
        compiler_params=pltpu.CompilerParams(dimension_semantics=("arbitrary", "arbitrary"),
                                             vmem_limit_bytes=VMEM_LIMIT),
        name="mixer_in",
    )(x, *consts, tabs["ck"], tabs["sk"], tabs["cq"], tabs["sq"])


def _na_kernel(q_ref, kp_ref, km_ref, kn_ref, vp_ref, vm_ref, vn_ref, bias_ref, o_ref, kbuf, vbuf,
               *, rows):
    g = NA_ROWS_PER_STEP
    halo = NA_KH * GRID_W
    main = g * GRID_W
    i = pl.program_id(1)
    kbuf[0:halo] = kp_ref[...]
    kbuf[halo:halo + main] = km_ref[...]
    kbuf[halo + main:] = kn_ref[...]
    vbuf[0:halo] = vp_ref[...]
    vbuf[halo:halo + main] = vm_ref[...]
    vbuf[halo + main:] = vn_ref[...]
    lane = lax.broadcasted_iota(jnp.int32, (GRID_W, LANES), 1)
    first_head = lane < NA_HEAD_DIM

    def row_body(rho, carry):
        r = i * g + rho
        rs = jnp.clip(r - NA_KH // 2, 0, rows - NA_KH)
        delta = r - rs
        off = pl.multiple_of((rs - (i * g - NA_KH)) * GRID_W, GRID_W)
        qoff = pl.multiple_of(rho * GRID_W, GRID_W)
        for hp in range(NA_HEADS // 2):
            ls = slice(hp * LANES, (hp + 1) * LANES)
            q2 = q_ref[pl.ds(qoff, GRID_W), ls]
            zero = jnp.zeros_like(q2)
            qs = jnp.concatenate([jnp.where(first_head, q2, zero),
                                  jnp.where(first_head, zero, q2)], axis=0)
            k2 = kbuf[pl.ds(off, NA_KH * GRID_W), ls]
            s = lax.dot_general(qs, k2, (((1,), (1,)), ((), ())), preferred_element_type=F32)
            s = s + bias_ref[delta, hp]
            m = jnp.max(s, axis=1, keepdims=True)
            p = jnp.exp(s - m)
            l = jnp.sum(p, axis=1, keepdims=True)
            v2 = vbuf[pl.ds(off, NA_KH * GRID_W), ls]
            o = _dot(p.astype(BF16), v2) / l
            out2 = jnp.where(first_head, o[:GRID_W], o[GRID_W:])
            o_ref[pl.ds(qoff, GRID_W), ls] = out2.astype(BF16)
        return carry

    lax.fori_loop(0, g, row_body, 0)


def na_attention(q, k, v, bias):
    b, n, c = q.shape
    rows = n // GRID_W
    g = NA_ROWS_PER_STEP
    main = g * GRID_W
    halo = NA_KH * GRID_W
    per = main // halo
    last = n // halo - 1
    spec_main = pl.BlockSpec((None, main, c), lambda bi, i: (bi, i, 0))
    spec_prev = pl.BlockSpec((None, halo, c), lambda bi, i: (bi, jnp.maximum(i * per - 1, 0), 0))
    spec_next = pl.BlockSpec((None, halo, c), lambda bi, i: (bi, jnp.minimum((i + 1) * per, last), 0))
    return pl.pallas_call(
        functools.partial(_na_kernel, rows=rows),
        out_shape=jax.ShapeDtypeStruct((b, n, c), BF16),
        grid=(b, rows // g),
        in_specs=[spec_main, spec_prev, spec_main, spec_next, spec_prev, spec_main, spec_next,
                  _const_spec(bias.shape)],
        out_specs=spec_main,
        scratch_shapes=[pltpu.VMEM((main + 2 * halo, c), BF16), pltpu.VMEM((main + 2 * halo, c), BF16)],
        compiler_params=pltpu.CompilerParams(dimension_semantics=("arbitrary", "arbitrary"),
                                             vmem_limit_bytes=VMEM_LIMIT),
        name="na_attention",
    )(q, k, k, k, v, v, v, bias)


def _mla_kernel(qt_ref, k_ref, vt_ref, o_ref, *, nk):
    qt = qt_ref[...]
    tq = qt.shape[1]
    tk = MLA_TK

    def body(j, carry):
        m, l, acc = carry
        off = pl.multiple_of(j * tk, tk)
        s = _dot(k_ref[pl.ds(off, tk), :], qt)
        m_new = jnp.maximum(m, jnp.max(s, axis=0, keepdims=True))
        rescale = jnp.exp2(m - m_new)
        p = jnp.exp2(s - m_new)
        l = rescale * l + jnp.sum(p, axis=0, keepdims=True)
        acc = rescale * acc + _dot(vt_ref[:, pl.ds(off, tk)], p.astype(BF16))
        return m_new, l, acc

    init = (jnp.full((1, tq), -jnp.inf, F32), jnp.zeros((1, tq), F32), jnp.zeros((MLA_V, tq), F32))
    _, l, acc = lax.fori_loop(0, nk, body, init)
    o_ref[...] = (acc / l).astype(BF16)


def mla_attention(qt, k, vt):
    b, h, _, n = qt.shape
    tq = MLA_TQ
    return pl.pallas_call(
        functools.partial(_mla_kernel, nk=n // MLA_TK),
        out_shape=jax.ShapeDtypeStruct((b, h, MLA_V, n), BF16),
        grid=(b, h, n // tq),
        in_specs=[pl.BlockSpec((None, None, QK_PAD, tq), lambda bi, hi, qi: (bi, hi, 0, qi)),
                  pl.BlockSpec((None, None, n, QK_PAD), lambda bi, hi, qi: (bi, hi, 0, 0)),
                  pl.BlockSpec((None, None, MLA_V, n), lambda bi, hi, qi: (bi, hi, 0, 0))],
        out_specs=pl.BlockSpec((None, None, MLA_V, tq), lambda bi, hi, qi: (bi, hi, 0, qi)),
        compiler_params=pltpu.CompilerParams(dimension_semantics=("arbitrary",) * 3,
                                             vmem_limit_bytes=VMEM_LIMIT),
        name="mla_attention",
    )(qt, k, vt)


def _mixer_out_kernel(x_ref, na_ref, at_ref, ga_ref, gb_ref, wna_ref, wmla_ref, wout_ref, g_ref, b_ref,
                      o_ref):
    ya = _dot(na_ref[...], wna_ref[...])
    yb = lax.dot_general(at_ref[...], wmla_ref[...], (((0,), (0,)), ((), ())),
                         preferred_element_type=F32)
    mix = ga_ref[...].astype(F32) * ya + gb_ref[...].astype(F32) * yb
    y = _dot(mix.astype(BF16), wout_ref[...])
    o_ref[...] = _layer_norm(ALPHA * x_ref[...] + y, g_ref[...], b_ref[...])


def mixer_out(x, na, at, ga, gb, w_na_o, w_mla_o, w_out, g, bb):
    b, n, d = x.shape
    tm = TOKEN_TILE
    tok = lambda c: pl.BlockSpec((None, tm, c), lambda bi, i: (bi, i, 0))
    return pl.pallas_call(
        _mixer_out_kernel,
        out_shape=jax.ShapeDtypeStruct((b, n, d), F32),
        grid=(b, n // tm),
        in_specs=[tok(d), tok(NA_WIDTH),
                  pl.BlockSpec((None, at.shape[1], tm), lambda bi, i: (bi, 0, i)),
                  tok(d), tok(d),
                  _const_spec(w_na_o.shape), _const_spec(w_mla_o.shape), _const_spec(w_out.shape),
                  _const_spec(g.shape), _const_spec(bb.shape)],
        out_specs=tok(d),
        compiler_params=pltpu.CompilerParams(dimension_semantics=("arbitrary", "arbitrary"),
                                             vmem_limit_bytes=VMEM_LIMIT),
        name="mixer_out",
    )(x, na, at, ga, gb, w_na_o, w_mla_o, w_out, g, bb)


def _prep_mixer_weights(w_in, b_gate, q_norm_g, kv_norm_g, w_uq, w_ukv):
    d = w_in.shape[0]
    half = MLA_ROPE // 2
    swap = lambda w: jnp.concatenate([-w[..., half:], w[..., :half]], axis=-1)
    pad_rope = lambda w: jnp.pad(w, ((0, 0), (MLA_NOPE, QK_PAD - MLA_QK)))
    c_kr = 3 * NA_WIDTH + MLA_Q_RANK + MLA_KV_RANK
    w_kr = w_in[:, c_kr:c_kr + MLA_ROPE]
    w_packed = jnp.concatenate([w_in[:, :c_kr], pad_rope(w_kr), pad_rope(swap(w_kr)),
                                w_in[:, c_kr + MLA_ROPE:]], axis=1)
    ukv = w_ukv.reshape(MLA_KV_RANK, MLA_HEADS, MLA_NOPE + MLA_V)
    wk = jnp.pad(ukv[..., :MLA_NOPE], ((0, 0), (0, 0), (0, QK_PAD - MLA_NOPE)))
    wk = wk.reshape(MLA_KV_RANK, MLA_HEADS * QK_PAD)
    wvt = ukv[..., MLA_NOPE:].reshape(MLA_KV_RANK, MLA_HEADS * MLA_V).T
    uq = w_uq.reshape(MLA_Q_RANK, MLA_HEADS, MLA_QK)
    wqt = jnp.pad(uq, ((0, 0), (0, 0), (0, QK_PAD - MLA_QK))).reshape(MLA_Q_RANK, MLA_HEADS * QK_PAD).T
    wqs = swap(uq[..., MLA_NOPE:]).reshape(MLA_Q_RANK, MLA_HEADS * MLA_ROPE).T
    return {
        "w_in": w_packed.astype(BF16), "b_gate": b_gate.reshape(1, 2 * d),
        "q_norm_g": q_norm_g.reshape(1, -1), "kv_norm_g": kv_norm_g.reshape(1, -1),
        "wk": wk.astype(BF16), "wvt": wvt.astype(BF16), "wqt": wqt.astype(BF16), "wqs": wqs.astype(BF16),
    }


def _rope_tables(n):
    inv = 1.0 / (ROPE_BASE ** (jnp.arange(0, MLA_ROPE, 2, dtype=F32) / MLA_ROPE))
    ang = jnp.arange(n, dtype=F32)[:, None] * inv[None, :]
    cos2 = jnp.tile(jnp.cos(ang), (1, 2))
    sin2 = jnp.tile(jnp.sin(ang), (1, 2))
    pad = ((0, 0), (MLA_NOPE, QK_PAD - MLA_QK))
    return {"ck": jnp.pad(cos2, pad), "sk": jnp.pad(sin2, pad), "cq": cos2.T, "sq": sin2.T}


def _na_bias_table(rpb):
    delta = np.arange(NA_KH)[:, None, None, None]
    qc = np.arange(GRID_W)[None, :, None, None]
    kr = np.arange(NA_KH)[None, None, :, None]
    kc = np.arange(GRID_W)[None, None, None, :]
    dr = np.broadcast_to(kr - delta + NA_KH - 1, (NA_KH, GRID_W, NA_KH, GRID_W))
    dc = np.broadcast_to(np.clip(kc - qc + NA_KW - 1, 0, 2 * NA_KW - 2), dr.shape)
    win = np.clip(qc - NA_KW // 2, 0, GRID_W - NA_KW)
    in_win = np.broadcast_to((kc >= win) & (kc < win + NA_KW), dr.shape)
    bias = rpb[:, dr, dc]
    bias = jnp.where(in_win[None], bias, NEG_BIG)
    bias = bias.reshape(NA_HEADS // 2, 2, NA_KH, GRID_W, NA_KH * GRID_W)
    return bias.transpose(2, 0, 1, 3, 4).reshape(NA_KH, NA_HEADS // 2, 2 * GRID_W, NA_KH * GRID_W)


def _encoder_layer(x, p, tabs):
    b, n, d = x.shape
    x1 = ffn_ln(x.reshape(b * n, d), p["ffn1_w_in"], p["ffn1_w_out"], p["ln1_g"], p["ln1_b"])
    x1 = x1.reshape(b, n, d)
    naq, nak, nav, qt, k, vt, ga, gb = mixer_in(x1, p["mixer"], tabs)
    na = na_attention(naq, nak, nav, p["na_bias"])
    at = mla_attention(qt, k, vt).reshape(b, MLA_HEADS * MLA_V, n)
    x2 = mixer_out(x1, na, at, ga, gb, p["w_na_o"], p["w_mla_o"], p["w_out"], p["ln2_g"], p["ln2_b"])
    y = ffn_ln(x2.reshape(b * n, d), p["ffn2_w_in"], p["ffn2_w_out"], p["ln3_g"], p["ln3_b"])
    return y.reshape(b, n, d)


def kernel(x_prompt, x_sample, ffn1_w_in, ffn1_w_out, ln1_g, ln1_b, w_in, b_gate, na_rpb, q_norm_g, kv_norm_g, w_uq, w_ukv, w_na_o, w_mla_o, w_out, ln2_g, ln2_b, ffn2_w_in, ffn2_w_out, ln3_g, ln3_b):
    l = 0
    row = lambda a: a[l].reshape(1, -1)
    p = {
        "ffn1_w_in": ffn1_w_in[l].astype(BF16), "ffn1_w_out": ffn1_w_out[l].astype(BF16),
        "ln1_g": row(ln1_g), "ln1_b": row(ln1_b),
        "mixer": _prep_mixer_weights(w_in[l], b_gate[l], q_norm_g[l], kv_norm_g[l], w_uq[l], w_ukv[l]),
        "na_bias": _na_bias_table(na_rpb[l]),
        "w_na_o": w_na_o[l].astype(BF16), "w_mla_o": w_mla_o[l].astype(BF16), "w_out": w_out[l].astype(BF16),
        "ln2_g": row(ln2_g), "ln2_b": row(ln2_b),
        "ffn2_w_in": ffn2_w_in[l].astype(BF16), "ffn2_w_out": ffn2_w_out[l].astype(BF16),
        "ln3_g": row(ln3_g), "ln3_b": row(ln3_b),
    }
    outs = []
    for x in (x_prompt, x_sample):
        outs.append(_encoder_layer(x, p, _rope_tables(x.shape[1])))
    return tuple(outs)
```

```python
import functools
import math

import numpy as np
import jax
import jax.numpy as jnp
from jax import lax
from jax.experimental import pallas as pl
from jax.experimental.pallas import tpu as pltpu

F32 = jnp.float32
BF16 = jnp.bfloat16

DEPTH = 1
GRID_W = 64
NA_HEADS = 8
NA_HEAD_DIM = 64
NA_WIDTH = NA_HEADS * NA_HEAD_DIM
NA_KH = 8
NA_KW = 16
MLA_HEADS = 8
MLA_NOPE = 64
MLA_ROPE = 32
MLA_QK = MLA_NOPE + MLA_ROPE
MLA_V = 64
MLA_VA = MLA_V + 16
MLA_Q_RANK = 384
MLA_KV_RANK = 256
ROPE_BASE = 10000.0
LN_EPS = 1e-5
RMS_EPS = 1e-6
ALPHA = (2.0 * DEPTH) ** 0.25
LOG2E = math.log2(math.e)

LANES = 128
QK_PAD = 128
VMEM_LIMIT = 56 * 1024 * 1024

TOKEN_TILE = 512
FFN_CHUNK = 256
NA_ROWS_PER_STEP = 16
MLA_TQ = 512
MLA_TK = 512
NEG_BIG = -1e30


def _const_spec(shape):
    nd = len(shape)
    return pl.BlockSpec(shape, lambda *_: (0,) * nd, pipeline_mode=pl.Buffered(1))


def _layer_norm(y, g, b):
    mu = jnp.mean(y, axis=-1, keepdims=True)
    d = y - mu
    var = jnp.mean(d * d, axis=-1, keepdims=True)
    return d * lax.rsqrt(var + LN_EPS) * g + b


def _rms_norm(y, g):
    return y * lax.rsqrt(jnp.mean(y * y, axis=-1, keepdims=True) + RMS_EPS) * g


def _dot(a, b):
    return jnp.dot(a, b, preferred_element_type=F32)


def _swiglu_ln(x, w_in_ref, w_out_ref, g_ref, b_ref, h_ref, d_ff):
    xb = x.astype(BF16)
    for c in range(d_ff // FFN_CHUNK):
        lo, hi = c * FFN_CHUNK, (c + 1) * FFN_CHUNK
        a = _dot(xb, w_in_ref[:, lo:hi])
        u = _dot(xb, w_in_ref[:, d_ff + lo:d_ff + hi])
        h_ref[:, lo:hi] = (a * jax.nn.sigmoid(a) * u).astype(BF16)
    y = _dot(h_ref[...], w_out_ref[...])
    return _layer_norm(ALPHA * x + 0.5 * y, g_ref[...], b_ref[...])


def _ffn_ln_kernel(x_ref, w_in_ref, w_out_ref, g_ref, b_ref, o_ref, h_ref, *, d_ff):
    o_ref[...] = _swiglu_ln(x_ref[...], w_in_ref, w_out_ref, g_ref, b_ref, h_ref, d_ff)


def ffn_ln(x, w_in, w_out, g, b):
    t, d = x.shape
    d_ff = w_out.shape[0]
    tm = TOKEN_TILE
    return pl.pallas_call(
        functools.partial(_ffn_ln_kernel, d_ff=d_ff),
        out_shape=jax.ShapeDtypeStruct((t, d), F32),
        grid=(t // tm,),
        in_specs=[pl.BlockSpec((tm, d), lambda i: (i, 0)),
                  _const_spec(w_in.shape), _const_spec(w_out.shape),
                  _const_spec(g.shape), _const_spec(b.shape)],
        out_specs=pl.BlockSpec((tm, d), lambda i: (i, 0)),
        scratch_shapes=[pltpu.VMEM((tm, d_ff), BF16)],
        compiler_params=pltpu.CompilerParams(dimension_semantics=("arbitrary",),
                                             vmem_limit_bytes=VMEM_LIMIT),
        name="ffn_ln",
    )(x, w_in, w_out, g, b)


_C_NAQ, _C_NAK, _C_NAV = 0, NA_WIDTH, 2 * NA_WIDTH
_C_CQ = 3 * NA_WIDTH
_C_CKV = _C_CQ + MLA_Q_RANK
_C_KR = _C_CKV + MLA_KV_RANK
_C_GATE = _C_KR + 2 * QK_PAD


def _mixer_in_kernel(x_ref, w_ref, bg_ref, qg_ref, kvg_ref, wk_ref, wvt_ref, wqt_ref, wqs_ref,
                     ck_ref, sk_ref, cq_ref, sq_ref,
                     naq_ref, nak_ref, nav_ref, qt_ref, k_ref, vt_ref, ga_ref, gb_ref, *, d_model):
    xb = x_ref[...].astype(BF16)
    naq_ref[...] = (_dot(xb, w_ref[:, _C_NAQ:_C_NAK]) * (NA_HEAD_DIM ** -0.5)).astype(BF16)
    nak_ref[...] = _dot(xb, w_ref[:, _C_NAK:_C_NAV]).astype(BF16)
    nav_ref[...] = _dot(xb, w_ref[:, _C_NAV:_C_CQ]).astype(BF16)

    cqn = _rms_norm(_dot(xb, w_ref[:, _C_CQ:_C_CKV]), qg_ref[...])
    ckvn = _rms_norm(_dot(xb, w_ref[:, _C_CKV:_C_KR]), kvg_ref[...])

    kr = _dot(xb, w_ref[:, _C_KR:_C_GATE])
    kr_blk = kr[:, :QK_PAD] * ck_ref[...] + kr[:, QK_PAD:] * sk_ref[...]
    kall = _dot(ckvn.astype(BF16), wk_ref[...])
    for h in range(MLA_HEADS):
        k_ref[h] = (kall[:, h * QK_PAD:(h + 1) * QK_PAD] + kr_blk).astype(BF16)

    ckvn_t = ckvn.T.astype(BF16)
    vt = _dot(wvt_ref[...], ckvn_t)
    ones = jnp.ones((MLA_VA - MLA_V, vt.shape[1]), BF16)
    for h in range(MLA_HEADS):
        vt_ref[h, 0:MLA_V] = vt[h * MLA_V:(h + 1) * MLA_V].astype(BF16)
        vt_ref[h, MLA_V:MLA_VA] = ones

    cqn_t = cqn.T.astype(BF16)
    q_scale = (MLA_QK ** -0.5) * LOG2E
    qt = _dot(wqt_ref[...], cqn_t) * q_scale
    qs = _dot(wqs_ref[...], cqn_t) * q_scale
    cq, sq = cq_ref[...], sq_ref[...]
    zeros = jnp.zeros((QK_PAD - MLA_QK, qt.shape[1]), BF16)
    for h in range(MLA_HEADS):
        base = h * QK_PAD
        qt_ref[h, 0:MLA_NOPE] = qt[base:base + MLA_NOPE].astype(BF16)
        rope = qt[base + MLA_NOPE:base + MLA_QK] * cq + qs[h * MLA_ROPE:(h + 1) * MLA_ROPE] * sq
        qt_ref[h, MLA_NOPE:MLA_QK] = rope.astype(BF16)
        qt_ref[h, MLA_QK:QK_PAD] = zeros

    for half, out_ref in enumerate((ga_ref, gb_ref)):
        lo = _C_GATE + half * d_model
        logits = _dot(xb, w_ref[:, lo:lo + d_model]) + bg_ref[:, half * d_model:(half + 1) * d_model]
        out_ref[...] = jax.nn.sigmoid(logits).astype(BF16)


def mixer_in(x, wts, tabs):
    b, n, d = x.shape
    tm = TOKEN_TILE
    h = MLA_HEADS
    tok = lambda c: pl.BlockSpec((None, tm, c), lambda bi, i: (bi, i, 0))
    out_shape = (
        jax.ShapeDtypeStruct((b, n, NA_WIDTH), BF16), jax.ShapeDtypeStruct((b, n, NA_WIDTH), BF16),
        jax.ShapeDtypeStruct((b, n, NA_WIDTH), BF16),
        jax.ShapeDtypeStruct((b, h, QK_PAD, n), BF16),
        jax.ShapeDtypeStruct((b, h, n, QK_PAD), BF16),
        jax.ShapeDtypeStruct((b, h, MLA_VA, n), BF16),
        jax.ShapeDtypeStruct((b, n, d), BF16), jax.ShapeDtypeStruct((b, n, d), BF16),
    )
    out_specs = (
        tok(NA_WIDTH), tok(NA_WIDTH), tok(NA_WIDTH),
        pl.BlockSpec((None, h, QK_PAD, tm), lambda bi, i: (bi, 0, 0, i)),
        pl.BlockSpec((None, h, tm, QK_PAD), lambda bi, i: (bi, 0, i, 0)),
        pl.BlockSpec((None, h, MLA_VA, tm), lambda bi, i: (bi, 0, 0, i)),
        tok(d), tok(d),
    )
    consts = [wts["w_in"], wts["b_gate"], wts["q_norm_g"], wts["kv_norm_g"],
              wts["wk"], wts["wvt"], wts["wqt"], wts["wqs"]]
    in_specs = ([tok(d)] + [_const_spec(c.shape) for c in consts] + [
        pl.BlockSpec((tm, QK_PAD), lambda bi, i: (i, 0)),
        pl.BlockSpec((tm, QK_PAD), lambda bi, i: (i, 0)),
        pl.BlockSpec((MLA_ROPE, tm), lambda bi, i: (0, i)),
        pl.BlockSpec((MLA_ROPE, tm), lambda bi, i: (0, i)),
    ])
    return pl.pallas_call(
        functools.partial(_mixer_in_kernel, d_model=d),
        out_shape=out_shape,
        grid=(b, n // tm),
        in_specs=in_specs,
        out_specs=out_specs,
        compiler_params=pltpu.CompilerParams(dimension_semantics=("arbitrary", "arbitrary"),
                                             vmem_limit_bytes=VMEM_LIMIT),
        name="mixer_in",
    )(x, *consts, tabs["ck"], tabs["sk"], tabs["cq"], tabs["sq"])


def _na_kernel(q_ref, kp_ref, km_ref, kn_ref, vp_ref, vm_ref, vn_ref, bias_ref, o_ref, kbuf, vbuf,
               *, rows):
    g = NA_ROWS_PER_STEP
    halo = NA_KH * GRID_W
    main = g * GRID_W
    i = pl.program_id(1)
    kbuf[0:halo] = kp_ref[...]
    kbuf[halo:halo + main] = km_ref[...]
    kbuf[halo + main:] = kn_ref[...]
    vbuf[0:halo] = vp_ref[...]
    vbuf[halo:halo + main] = vm_ref[...]
    vbuf[halo + main:] = vn_ref[...]
    lane = lax.broadcasted_iota(jnp.int32, (GRID_W, LANES), 1)
    first_head = lane < NA_HEAD_DIM

    def row_body(rho, carry):
        r = i * g + rho
        rs = jnp.clip(r - NA_KH // 2, 0, rows - NA_KH)
        delta = r - rs
        off = pl.multiple_of((rs - (i * g - NA_KH)) * GRID_W, GRID_W)
        qoff = pl.multiple_of(rho * GRID_W, GRID_W)
        for hp in range(NA_HEADS // 2):
            ls = slice(hp * LANES, (hp + 1) * LANES)
            q2 = q_ref[pl.ds(qoff, GRID_W), ls]
            zero = jnp.zeros_like(q2)
            qs = jnp.concatenate([jnp.where(first_head, q2, zero),
                                  jnp.where(first_head, zero, q2)], axis=0)
            k2 = kbuf[pl.ds(off, NA_KH * GRID_W), ls]
            s = lax.dot_general(qs, k2, (((1,), (1,)), ((), ())), preferred_element_type=F32)
            s = s + bias_ref[delta, hp]
            m = jnp.max(s, axis=1, keepdims=True)
            p = jnp.exp(s - m)
            l = jnp.sum(p, axis=1, keepdims=True)
            v2 = vbuf[pl.ds(off, NA_KH * GRID_W), ls]
            o = _dot(p.astype(BF16), v2) / l
            out2 = jnp.where(first_head, o[:GRID_W], o[GRID_W:])
            o_ref[pl.ds(qoff, GRID_W), ls] = out2.astype(BF16)
        return carry

    lax.fori_loop(0, g, row_body, 0)


def na_attention(q, k, v, bias):
    b, n, c = q.shape
    rows = n // GRID_W
    g = NA_ROWS_PER_STEP
    main = g * GRID_W
    halo = NA_KH * GRID_W
    per = main // halo
    last = n // halo - 1
    spec_main = pl.BlockSpec((None, main, c), lambda bi, i: (bi, i, 0))
    spec_prev = pl.BlockSpec((None, halo, c), lambda bi, i: (bi, jnp.maximum(i * per - 1, 0), 0))
    spec_next = pl.BlockSpec((None, halo, c), lambda bi, i: (bi, jnp.minimum((i + 1) * per, last), 0))
    return pl.pallas_call(
        functools.partial(_na_kernel, rows=rows),
        out_shape=jax.ShapeDtypeStruct((b, n, c), BF16),
        grid=(b, rows // g),
        in_specs=[spec_main, spec_prev, spec_main, spec_next, spec_prev, spec_main, spec_next,
                  _const_spec(bias.shape)],
        out_specs=spec_main,
        scratch_shapes=[pltpu.VMEM((main + 2 * halo, c), BF16), pltpu.VMEM((main + 2 * halo, c), BF16)],
        compiler_params=pltpu.CompilerParams(dimension_semantics=("arbitrary", "arbitrary"),
                                             vmem_limit_bytes=VMEM_LIMIT),
        name="na_attention",
    )(q, k, k, k, v, v, v, bias)


def _mla_kernel(qt_ref, k_ref, vt_ref, o_ref, s_buf, p_buf, *, nk):
    qt = qt_ref[...]
    tq = qt.shape[1]
    tk = MLA_TK
    assert nk % 2 == 0 and nk >= 4

    def scores(j, slot):
        off = pl.multiple_of(j * tk, tk)
        s = _dot(k_ref[pl.ds(off, tk), :], qt)
        s_buf[slot] = s
        return jnp.max(s, axis=0, keepdims=True)

    def softmax(slot, m, mx):
        m_new = jnp.maximum(m, mx)
        p_buf[slot] = jnp.exp2(s_buf[slot] - m_new).astype(BF16)
        return m_new, jnp.exp2(m - m_new)

    def values(j, slot, rescale, acc):
        off = pl.multiple_of(j * tk, tk)
        return rescale * acc + _dot(vt_ref[:, pl.ds(off, tk)], p_buf[slot])

    def step(j, slot, carry):
        m, mx, r_prev, acc = carry
        mx_next = scores(j + 1, 1 - slot)
        m, r = softmax(slot, m, mx)
        acc = values(j - 1, 1 - slot, r_prev, acc)
        return m, mx_next, r, acc

    m = jnp.full((1, tq), -jnp.inf, F32)
    acc = jnp.zeros((MLA_VA, tq), F32)
    mx = scores(0, 0)
    mx_next = scores(1, 1)
    m, r = softmax(0, m, mx)
    carry = (m, mx_next, r, acc)

    def pair(jj, carry):
        j = 2 * jj + 1
        return step(j + 1, 0, step(j, 1, carry))

    m, mx, r_prev, acc = lax.fori_loop(0, (nk - 2) // 2, pair, carry)
    m, r = softmax(1, m, mx)
    acc = values(nk - 2, 0, r_prev, acc)
    acc = values(nk - 1, 1, r, acc)
    o_ref[...] = (acc[:MLA_V] / acc[MLA_V:MLA_V + 1]).astype(BF16)


def mla_attention(qt, k, vt):
    b, h, _, n = qt.shape
    tq = MLA_TQ
    return pl.pallas_call(
        functools.partial(_mla_kernel, nk=n // MLA_TK),
        out_shape=jax.ShapeDtypeStruct((b, h, MLA_V, n), BF16),
        grid=(b, h, n // tq),
        in_specs=[pl.BlockSpec((None, None, QK_PAD, tq), lambda bi, hi, qi: (bi, hi, 0, qi)),
                  pl.BlockSpec((None, None, n, QK_PAD), lambda bi, hi, qi: (bi, hi, 0, 0)),
                  pl.BlockSpec((None, None, MLA_VA, n), lambda bi, hi, qi: (bi, hi, 0, 0))],
        out_specs=pl.BlockSpec((None, None, MLA_V, tq), lambda bi, hi, qi: (bi, hi, 0, qi)),
        scratch_shapes=[pltpu.VMEM((2, MLA_TK, tq), F32), pltpu.VMEM((2, MLA_TK, tq), BF16)],
        compiler_params=pltpu.CompilerParams(dimension_semantics=("arbitrary",) * 3,
                                             vmem_limit_bytes=VMEM_LIMIT),
        name="mla_attention",
    )(qt, k, vt)


def _mixer_out_kernel(x_ref, na_ref, at_ref, ga_ref, gb_ref, wna_ref, wmla_ref, wout_ref, g_ref, b_ref,
                      o_ref):
    ya = _dot(na_ref[...], wna_ref[...])
    yb = lax.dot_general(at_ref[...], wmla_ref[...], (((0,), (0,)), ((), ())),
                         preferred_element_type=F32)
    mix = ga_ref[...].astype(F32) * ya + gb_ref[...].astype(F32) * yb
    y = _dot(mix.astype(BF16), wout_ref[...])
    o_ref[...] = _layer_norm(ALPHA * x_ref[...] + y, g_ref[...], b_ref[...])


def mixer_out(x, na, at, ga, gb, w_na_o, w_mla_o, w_out, g, bb):
    b, n, d = x.shape
    tm = TOKEN_TILE
    tok = lambda c: pl.BlockSpec((None, tm, c), lambda bi, i: (bi, i, 0))
    return pl.pallas_call(
        _mixer_out_kernel,
        out_shape=jax.ShapeDtypeStruct((b, n, d), F32),
        grid=(b, n // tm),
        in_specs=[tok(d), tok(NA_WIDTH),
                  pl.BlockSpec((None, at.shape[1], tm), lambda bi, i: (bi, 0, i)),
                  tok(d), tok(d),
                  _const_spec(w_na_o.shape), _const_spec(w_mla_o.shape), _const_spec(w_out.shape),
                  _const_spec(g.shape), _const_spec(bb.shape)],
        out_specs=tok(d),
        compiler_params=pltpu.CompilerParams(dimension_semantics=("arbitrary", "arbitrary"),
                                             vmem_limit_bytes=VMEM_LIMIT),
        name="mixer_out",
    )(x, na, at, ga, gb, w_na_o, w_mla_o, w_out, g, bb)


def _prep_mixer_weights(w_in, b_gate, q_norm_g, kv_norm_g, w_uq, w_ukv):
    d = w_in.shape[0]
    half = MLA_ROPE // 2
    swap = lambda w: jnp.concatenate([-w[..., half:], w[..., :half]], axis=-1)
    pad_rope = lambda w: jnp.pad(w, ((0, 0), (MLA_NOPE, QK_PAD - MLA_QK)))
    c_kr = 3 * NA_WIDTH + MLA_Q_RANK + MLA_KV_RANK
    w_kr = w_in[:, c_kr:c_kr + MLA_ROPE]
    w_packed = jnp.concatenate([w_in[:, :c_kr], pad_rope(w_kr), pad_rope(swap(w_kr)),
                                w_in[:, c_kr + MLA_ROPE:]], axis=1)
    ukv = w_ukv.reshape(MLA_KV_RANK, MLA_HEADS, MLA_NOPE + MLA_V)
    wk = jnp.pad(ukv[..., :MLA_NOPE], ((0, 0), (0, 0), (0, QK_PAD - MLA_NOPE)))
    wk = wk.reshape(MLA_KV_RANK, MLA_HEADS * QK_PAD)
    wvt = ukv[..., MLA_NOPE:].reshape(MLA_KV_RANK, MLA_HEADS * MLA_V).T
    uq = w_uq.reshape(MLA_Q_RANK, MLA_HEADS, MLA_QK)
    wqt = jnp.pad(uq, ((0, 0), (0, 0), (0, QK_PAD - MLA_QK))).reshape(MLA_Q_RANK, MLA_HEADS * QK_PAD).T
    wqs = swap(uq[..., MLA_NOPE:]).reshape(MLA_Q_RANK, MLA_HEADS * MLA_ROPE).T
    return {
        "w_in": w_packed.astype(BF16), "b_gate": b_gate.reshape(1, 2 * d),
        "q_norm_g": q_norm_g.reshape(1, -1), "kv_norm_g": kv_norm_g.reshape(1, -1),
        "wk": wk.astype(BF16), "wvt": wvt.astype(BF16), "wqt": wqt.astype(BF16), "wqs": wqs.astype(BF16),
    }


def _rope_tables(n):
    inv = 1.0 / (ROPE_BASE ** (jnp.arange(0, MLA_ROPE, 2, dtype=F32) / MLA_ROPE))
    ang = jnp.arange(n, dtype=F32)[:, None] * inv[None, :]
    cos2 = jnp.tile(jnp.cos(ang), (1, 2))
    sin2 = jnp.tile(jnp.sin(ang), (1, 2))
    pad = ((0, 0), (MLA_NOPE, QK_PAD - MLA_QK))
    return {"ck": jnp.pad(cos2, pad), "sk": jnp.pad(sin2, pad), "cq": cos2.T, "sq": sin2.T}


def _na_bias_table(rpb):
    qc = np.arange(GRID_W)[:, None]
    kc = np.arange(GRID_W)[None, :]
    dc = np.clip(kc - qc + NA_KW - 1, 0, 2 * NA_KW - 2)
    onehot = (dc[None] == np.arange(2 * NA_KW - 1)[:, None, None]).astype(np.float32)
    win = np.clip(qc - NA_KW // 2, 0, GRID_W - NA_KW)
    in_win = (kc >= win) & (kc < win + NA_KW)
    t = jnp.einsum("hrc,cqk->hrqk", rpb, jnp.asarray(onehot), precision=lax.Precision.HIGHEST)
    t = jnp.where(in_win, t, NEG_BIG)
    bias = jnp.stack([t[:, NA_KH - 1 - dl:2 * NA_KH - 1 - dl] for dl in range(NA_KH)])
    bias = bias.transpose(0, 1, 3, 2, 4)
    return bias.reshape(NA_KH, NA_HEADS // 2, 2 * GRID_W, NA_KH * GRID_W)


def _encoder_layer(x, p, tabs):
    b, n, d = x.shape
    x1 = ffn_ln(x.reshape(b * n, d), p["ffn1_w_in"], p["ffn1_w_out"], p["ln1_g"], p["ln1_b"])
    x1 = x1.reshape(b, n, d)
    naq, nak, nav, qt, k, vt, ga, gb = mixer_in(x1, p["mixer"], tabs)
    na = na_attention(naq, nak, nav, p["na_bias"])
    at = mla_attention(qt, k, vt).reshape(b, MLA_HEADS * MLA_V, n)
    x2 = mixer_out(x1, na, at, ga, gb, p["w_na_o"], p["w_mla_o"], p["w_out"], p["ln2_g"], p["ln2_b"])
    y = ffn_ln(x2.reshape(b * n, d), p["ffn2_w_in"], p["ffn2_w_out"], p["ln3_g"], p["ln3_b"])
    return y.reshape(b, n, d)


def kernel(x_prompt, x_sample, ffn1_w_in, ffn1_w_out, ln1_g, ln1_b, w_in, b_gate, na_rpb, q_norm_g, kv_norm_g, w_uq, w_ukv, w_na_o, w_mla_o, w_out, ln2_g, ln2_b, ffn2_w_in, ffn2_w_out, ln3_g, ln3_b):
    l = 0
    row = lambda a: a[l].reshape(1, -1)
    p = {
        "ffn1_w_in": ffn1_w_in[l].astype(BF16), "ffn1_w_out": ffn1_w_out[l].astype(BF16),
        "ln1_g": row(ln1_g), "ln1_b": row(ln1_b),
        "mixer": _prep_mixer_weights(w_in[l], b_gate[l], q_norm_g[l], kv_norm_g[l], w_uq[l], w_ukv[l]),
        "na_bias": _na_bias_table(na_rpb[l]),
        "w_na_o": w_na_o[l].astype(BF16), "w_mla_o": w_mla_o[l].astype(BF16), "w_out": w_out[l].astype(BF16),
        "ln2_g": row(ln2_g), "ln2_b": row(ln2_b),
        "ffn2_w_in": ffn2_w_in[l].astype(BF16), "ffn2_w_out": ffn2_w_out[l].astype(BF16),
        "ln3_g": row(ln3_g), "ln3_b": row(ln3_b),
    }
    outs = []
    for x in (x_prompt, x_sample):
        outs.append(_encoder_layer(x, p, _rope_tables(x.shape[1])))
    return tuple(outs)
```

```python
import functools
import math

import numpy as np
import jax
import jax.numpy as jnp
from jax import lax
from jax.experimental import pallas as pl
from jax.experimental.pallas import tpu as pltpu

F32 = jnp.float32
BF16 = jnp.bfloat16

DEPTH = 1
GRID_W = 64
NA_HEADS = 8
NA_HEAD_DIM = 64
NA_WIDTH = NA_HEADS * NA_HEAD_DIM
NA_KH = 8
NA_KW = 16
MLA_HEADS = 8
MLA_NOPE = 64
MLA_ROPE = 32
MLA_QK = MLA_NOPE + MLA_ROPE
MLA_V = 64
MLA_VA = MLA_V + 16
MLA_Q_RANK = 384
MLA_KV_RANK = 256
ROPE_BASE = 10000.0
LN_EPS = 1e-5
RMS_EPS = 1e-6
ALPHA = (2.0 * DEPTH) ** 0.25
LOG2E = math.log2(math.e)

LANES = 128
QK_PAD = 128
VMEM_LIMIT = 56 * 1024 * 1024

TOKEN_TILE = 512
FFN_CHUNK = 256
NA_ROWS_PER_STEP = 16
MLA_TQ = 512
MLA_TK = 256
MLA_GROUP = 4
NEG_BIG = -1e30


def _const_spec(shape):
    nd = len(shape)
    return pl.BlockSpec(shape, lambda *_: (0,) * nd, pipeline_mode=pl.Buffered(1))


def _layer_norm(y, g, b):
    mu = jnp.mean(y, axis=-1, keepdims=True)
    d = y - mu
    var = jnp.mean(d * d, axis=-1, keepdims=True)
    return d * lax.rsqrt(var + LN_EPS) * g + b


def _rms_norm(y, g):
    return y * lax.rsqrt(jnp.mean(y * y, axis=-1, keepdims=True) + RMS_EPS) * g


def _dot(a, b):
    return jnp.dot(a, b, preferred_element_type=F32)


def _swiglu_ln(x, w_in_ref, w_out_ref, g_ref, b_ref, h_ref, d_ff):
    xb = x.astype(BF16)
    for c in range(d_ff // FFN_CHUNK):
        lo, hi = c * FFN_CHUNK, (c + 1) * FFN_CHUNK
        a = _dot(xb, w_in_ref[:, lo:hi])
        u = _dot(xb, w_in_ref[:, d_ff + lo:d_ff + hi])
        h_ref[:, lo:hi] = (a * jax.nn.sigmoid(a) * u).astype(BF16)
    y = _dot(h_ref[...], w_out_ref[...])
    return _layer_norm(ALPHA * x + 0.5 * y, g_ref[...], b_ref[...])


def _ffn_ln_kernel(x_ref, w_in_ref, w_out_ref, g_ref, b_ref, o_ref, h_ref, *, d_ff):
    o_ref[...] = _swiglu_ln(x_ref[...], w_in_ref, w_out_ref, g_ref, b_ref, h_ref, d_ff)


def ffn_ln(x, w_in, w_out, g, b):
    t, d = x.shape
    d_ff = w_out.shape[0]
    tm = TOKEN_TILE
    return pl.pallas_call(
        functools.partial(_ffn_ln_kernel, d_ff=d_ff),
        out_shape=jax.ShapeDtypeStruct((t, d), F32),
        grid=(t // tm,),
        in_specs=[pl.BlockSpec((tm, d), lambda i: (i, 0)),
                  _const_spec(w_in.shape), _const_spec(w_out.shape),
                  _const_spec(g.shape), _const_spec(b.shape)],
        out_specs=pl.BlockSpec((tm, d), lambda i: (i, 0)),
        scratch_shapes=[pltpu.VMEM((tm, d_ff), BF16)],
        compiler_params=pltpu.CompilerParams(dimension_semantics=("arbitrary",),
                                             vmem_limit_bytes=VMEM_LIMIT),
        name="ffn_ln",
    )(x, w_in, w_out, g, b)


_C_NAQ, _C_NAK, _C_NAV = 0, NA_WIDTH, 2 * NA_WIDTH
_C_CQ = 3 * NA_WIDTH
_C_CKV = _C_CQ + MLA_Q_RANK
_C_KR = _C_CKV + MLA_KV_RANK
_C_GATE = _C_KR + 2 * QK_PAD


def _mixer_in_kernel(x_ref, w_ref, bg_ref, qg_ref, kvg_ref, wk_ref, wvt_ref, wqt_ref, wqs_ref,
                     ck_ref, sk_ref, cq_ref, sq_ref,
                     naq_ref, nak_ref, nav_ref, qt_ref, k_ref, vt_ref, ga_ref, gb_ref, *, d_model):
    xb = x_ref[...].astype(BF16)
    naq_ref[...] = (_dot(xb, w_ref[:, _C_NAQ:_C_NAK]) * (NA_HEAD_DIM ** -0.5)).astype(BF16)
    nak_ref[...] = _dot(xb, w_ref[:, _C_NAK:_C_NAV]).astype(BF16)
    nav_ref[...] = _dot(xb, w_ref[:, _C_NAV:_C_CQ]).astype(BF16)

    cqn = _rms_norm(_dot(xb, w_ref[:, _C_CQ:_C_CKV]), qg_ref[...])
    ckvn = _rms_norm(_dot(xb, w_ref[:, _C_CKV:_C_KR]), kvg_ref[...])

    kr = _dot(xb, w_ref[:, _C_KR:_C_GATE])
    kr_blk = kr[:, :QK_PAD] * ck_ref[...] + kr[:, QK_PAD:] * sk_ref[...]
    kall = _dot(ckvn.astype(BF16), wk_ref[...])
    for h in range(MLA_HEADS):
        k_ref[h] = (kall[:, h * QK_PAD:(h + 1) * QK_PAD] + kr_blk).astype(BF16)

    ckvn_t = ckvn.T.astype(BF16)
    vt = _dot(wvt_ref[...], ckvn_t)
    ones = jnp.ones((MLA_VA - MLA_V, vt.shape[1]), BF16)
    for h in range(MLA_HEADS):
        vt_ref[h, 0:MLA_V] = vt[h * MLA_V:(h + 1) * MLA_V].astype(BF16)
        vt_ref[h, MLA_V:MLA_VA] = ones

    cqn_t = cqn.T.astype(BF16)
    q_scale = (MLA_QK ** -0.5) * LOG2E
    qt = _dot(wqt_ref[...], cqn_t) * q_scale
    qs = _dot(wqs_ref[...], cqn_t) * q_scale
    cq, sq = cq_ref[...], sq_ref[...]
    zeros = jnp.zeros((QK_PAD - MLA_QK, qt.shape[1]), BF16)
    for h in range(MLA_HEADS):
        base = h * QK_PAD
        qt_ref[h, 0:MLA_NOPE] = qt[base:base + MLA_NOPE].astype(BF16)
        rope = qt[base + MLA_NOPE:base + MLA_QK] * cq + qs[h * MLA_ROPE:(h + 1) * MLA_ROPE] * sq
        qt_ref[h, MLA_NOPE:MLA_QK] = rope.astype(BF16)
        qt_ref[h, MLA_QK:QK_PAD] = zeros

    for half, out_ref in enumerate((ga_ref, gb_ref)):
        lo = _C_GATE + half * d_model
        logits = _dot(xb, w_ref[:, lo:lo + d_model]) + bg_ref[:, half * d_model:(half + 1) * d_model]
        out_ref[...] = jax.nn.sigmoid(logits).astype(BF16)


def mixer_in(x, wts, tabs):
    b, n, d = x.shape
    tm = TOKEN_TILE
    h = MLA_HEADS
    tok = lambda c: pl.BlockSpec((None, tm, c), lambda bi, i: (bi, i, 0))
    out_shape = (
        jax.ShapeDtypeStruct((b, n, NA_WIDTH), BF16), jax.ShapeDtypeStruct((b, n, NA_WIDTH), BF16),
        jax.ShapeDtypeStruct((b, n, NA_WIDTH), BF16),
        jax.ShapeDtypeStruct((b, h, QK_PAD, n), BF16),
        jax.ShapeDtypeStruct((b, h, n, QK_PAD), BF16),
        jax.ShapeDtypeStruct((b, h, MLA_VA, n), BF16),
        jax.ShapeDtypeStruct((b, n, d), BF16), jax.ShapeDtypeStruct((b, n, d), BF16),
    )
    out_specs = (
        tok(NA_WIDTH), tok(NA_WIDTH), tok(NA_WIDTH),
        pl.BlockSpec((None, h, QK_PAD, tm), lambda bi, i: (bi, 0, 0, i)),
        pl.BlockSpec((None, h, tm, QK_PAD), lambda bi, i: (bi, 0, i, 0)),
        pl.BlockSpec((None, h, MLA_VA, tm), lambda bi, i: (bi, 0, 0, i)),
        tok(d), tok(d),
    )
    consts = [wts["w_in"], wts["b_gate"], wts["q_norm_g"], wts["kv_norm_g"],
              wts["wk"], wts["wvt"], wts["wqt"], wts["wqs"]]
    in_specs = ([tok(d)] + [_const_spec(c.shape) for c in consts] + [
        pl.BlockSpec((tm, QK_PAD), lambda bi, i: (i, 0)),
        pl.BlockSpec((tm, QK_PAD), lambda bi, i: (i, 0)),
        pl.BlockSpec((MLA_ROPE, tm), lambda bi, i: (0, i)),
        pl.BlockSpec((MLA_ROPE, tm), lambda bi, i: (0, i)),
    ])
    return pl.pallas_call(
        functools.partial(_mixer_in_kernel, d_model=d),
        out_shape=out_shape,
        grid=(b, n // tm),
        in_specs=in_specs,
        out_specs=out_specs,
        compiler_params=pltpu.CompilerParams(dimension_semantics=("arbitrary", "arbitrary"),
                                             vmem_limit_bytes=VMEM_LIMIT),
        name="mixer_in",
    )(x, *consts, tabs["ck"], tabs["sk"], tabs["cq"], tabs["sq"])


def _na_kernel(q_ref, kp_ref, km_ref, kn_ref, vp_ref, vm_ref, vn_ref, bias_ref, o_ref, kbuf, vbuf,
               *, rows):
    g = NA_ROWS_PER_STEP
    halo = NA_KH * GRID_W
    main = g * GRID_W
    i = pl.program_id(1)
    kbuf[0:halo] = kp_ref[...]
    kbuf[halo:halo + main] = km_ref[...]
    kbuf[halo + main:] = kn_ref[...]
    vbuf[0:halo] = vp_ref[...]
    vbuf[halo:halo + main] = vm_ref[...]
    vbuf[halo + main:] = vn_ref[...]
    lane = lax.broadcasted_iota(jnp.int32, (GRID_W, LANES), 1)
    first_head = lane < NA_HEAD_DIM

    def row_body(rho, carry):
        r = i * g + rho
        rs = jnp.clip(r - NA_KH // 2, 0, rows - NA_KH)
        delta = r - rs
        off = pl.multiple_of((rs - (i * g - NA_KH)) * GRID_W, GRID_W)
        qoff = pl.multiple_of(rho * GRID_W, GRID_W)
        for hp in range(NA_HEADS // 2):
            ls = slice(hp * LANES, (hp + 1) * LANES)
            q2 = q_ref[pl.ds(qoff, GRID_W), ls]
            zero = jnp.zeros_like(q2)
            qs = jnp.concatenate([jnp.where(first_head, q2, zero),
                                  jnp.where(first_head, zero, q2)], axis=0)
            k2 = kbuf[pl.ds(off, NA_KH * GRID_W), ls]
            s = lax.dot_general(qs, k2, (((1,), (1,)), ((), ())), preferred_element_type=F32)
            s = s + bias_ref[delta, hp]
            m = jnp.max(s, axis=1, keepdims=True)
            p = jnp.exp(s - m)
            l = jnp.sum(p, axis=1, keepdims=True)
            v2 = vbuf[pl.ds(off, NA_KH * GRID_W), ls]
            o = _dot(p.astype(BF16), v2) / l
            out2 = jnp.where(first_head, o[:GRID_W], o[GRID_W:])
            o_ref[pl.ds(qoff, GRID_W), ls] = out2.astype(BF16)
        return carry

    lax.fori_loop(0, g, row_body, 0)


def na_attention(q, k, v, bias):
    b, n, c = q.shape
    rows = n // GRID_W
    g = NA_ROWS_PER_STEP
    main = g * GRID_W
    halo = NA_KH * GRID_W
    per = main // halo
    last = n // halo - 1
    spec_main = pl.BlockSpec((None, main, c), lambda bi, i: (bi, i, 0))
    spec_prev = pl.BlockSpec((None, halo, c), lambda bi, i: (bi, jnp.maximum(i * per - 1, 0), 0))
    spec_next = pl.BlockSpec((None, halo, c), lambda bi, i: (bi, jnp.minimum((i + 1) * per, last), 0))
    return pl.pallas_call(
        functools.partial(_na_kernel, rows=rows),
        out_shape=jax.ShapeDtypeStruct((b, n, c), BF16),
        grid=(b, rows // g),
        in_specs=[spec_main, spec_prev, spec_main, spec_next, spec_prev, spec_main, spec_next,
                  _const_spec(bias.shape)],
        out_specs=spec_main,
        scratch_shapes=[pltpu.VMEM((main + 2 * halo, c), BF16), pltpu.VMEM((main + 2 * halo, c), BF16)],
        compiler_params=pltpu.CompilerParams(dimension_semantics=("arbitrary", "arbitrary"),
                                             vmem_limit_bytes=VMEM_LIMIT),
        name="na_attention",
    )(q, k, k, k, v, v, v, bias)


def _mla_kernel(qt_ref, k_ref, vt_ref, o_ref, s0, s1, p0, p1, *, n):
    tq, tk, g = MLA_TQ, MLA_TK, MLA_GROUP
    ng = n // (tk * g)
    total = (n // tq) * ng
    assert total % 2 == 0 and total >= 4
    s_bufs, p_bufs = (s0, s1), (p0, p1)

    def split(u):
        return lax.div(u, ng), lax.rem(u, ng)

    def scores_chunk(u, slot, c):
        qi, kg = split(u)
        qt = qt_ref[:, pl.ds(pl.multiple_of(qi * tq, tq), tq)]
        off = pl.multiple_of((kg * g + c) * tk, tk)
        s = _dot(k_ref[pl.ds(off, tk), :], qt)
        s_bufs[slot][c * tk:(c + 1) * tk] = s
        return jnp.max(s, axis=0, keepdims=True)

    def first_group_reset(u, m):
        _, kg = split(u)
        return jnp.where(kg == 0, -jnp.inf, m)

    def softmax_chunk(slot, c, m, mx):
        m_new = jnp.maximum(m, mx)
        p_bufs[slot][c * tk:(c + 1) * tk] = jnp.exp2(s_bufs[slot][c * tk:(c + 1) * tk] - m_new).astype(BF16)
        return m_new, jnp.exp2(m - m_new)

    def values_chunk(u, slot, c, r, acc):
        _, kg = split(u)
        off = pl.multiple_of((kg * g + c) * tk, tk)
        return r * acc + _dot(vt_ref[:, pl.ds(off, tk)], p_bufs[slot][c * tk:(c + 1) * tk])

    def write_out(u, acc):
        qi, _ = split(u)
        out = acc[:MLA_V] * (1.0 / acc[MLA_V:MLA_V + 1])
        o_ref[:, pl.ds(pl.multiple_of(qi * tq, tq), tq)] = out.astype(BF16)

    def block(t, slot, carry, do_scores=True, do_softmax=True, do_values=True):
        m, mxs, rs_prev, acc = carry
        if do_softmax:
            m = first_group_reset(t, m)
        mxs_next, rs = list(mxs), list(rs_prev)
        if do_scores:
            for c in range(g):
                mxs_next[c] = scores_chunk(t + 1, 1 - slot, c)
        if do_softmax:
            for c in range(g):
                m, rs[c] = softmax_chunk(slot, c, m, mxs[c])
        if do_values:
            for c in range(g):
                acc = values_chunk(t - 1, 1 - slot, c, rs_prev[c], acc)
            write_out(t - 1, acc)
        return m, tuple(mxs_next), tuple(rs), acc

    row = jnp.zeros((1, tq), F32)
    carry = (row, (row,) * g, (row,) * g, jnp.zeros((MLA_VA, tq), F32))
    carry = block(-1, 1, carry, do_softmax=False, do_values=False)
    carry = block(0, 0, carry, do_values=False)

    def pair(i, carry):
        t = 2 * i + 1
        return block(t + 1, 0, block(t, 1, carry))

    carry = lax.fori_loop(0, (total - 2) // 2, pair, carry)
    carry = block(total - 1, 1, carry, do_scores=False)
    block(total, 0, carry, do_scores=False, do_softmax=False)


def mla_attention(qt, k, vt):
    b, h, _, n = qt.shape
    rows = MLA_GROUP * MLA_TK
    head = lambda r, c: pl.BlockSpec((None, None, r, c), lambda bi, hi: (bi, hi, 0, 0))
    return pl.pallas_call(
        functools.partial(_mla_kernel, n=n),
        out_shape=jax.ShapeDtypeStruct((b, h, MLA_V, n), BF16),
        grid=(b, h),
        in_specs=[head(QK_PAD, n), head(n, QK_PAD), head(MLA_VA, n)],
        out_specs=head(MLA_V, n),
        scratch_shapes=[pltpu.VMEM((rows, MLA_TQ), F32), pltpu.VMEM((rows, MLA_TQ), F32),
                        pltpu.VMEM((rows, MLA_TQ), BF16), pltpu.VMEM((rows, MLA_TQ), BF16)],
        compiler_params=pltpu.CompilerParams(dimension_semantics=("arbitrary",) * 2,
                                             vmem_limit_bytes=VMEM_LIMIT),
        name="mla_attention",
    )(qt, k, vt)


def _mixer_out_kernel(x_ref, na_ref, at_ref, ga_ref, gb_ref, wna_ref, wmla_ref, wout_ref, g_ref, b_ref,
                      o_ref):
    ya = _dot(na_ref[...], wna_ref[...])
    yb = lax.dot_general(at_ref[...], wmla_ref[...], (((0,), (0,)), ((), ())),
                         preferred_element_type=F32)
    mix = ga_ref[...].astype(F32) * ya + gb_ref[...].astype(F32) * yb
    y = _dot(mix.astype(BF16), wout_ref[...])
    o_ref[...] = _layer_norm(ALPHA * x_ref[...] + y, g_ref[...], b_ref[...])


def mixer_out(x, na, at, ga, gb, w_na_o, w_mla_o, w_out, g, bb):
    b, n, d = x.shape
    tm = TOKEN_TILE
    tok = lambda c: pl.BlockSpec((None, tm, c), lambda bi, i: (bi, i, 0))
    return pl.pallas_call(
        _mixer_out_kernel,
        out_shape=jax.ShapeDtypeStruct((b, n, d), F32),
        grid=(b, n // tm),
        in_specs=[tok(d), tok(NA_WIDTH),
                  pl.BlockSpec((None, at.shape[1], tm), lambda bi, i: (bi, 0, i)),
                  tok(d), tok(d),
                  _const_spec(w_na_o.shape), _const_spec(w_mla_o.shape), _const_spec(w_out.shape),
                  _const_spec(g.shape), _const_spec(bb.shape)],
        out_specs=tok(d),
        compiler_params=pltpu.CompilerParams(dimension_semantics=("arbitrary", "arbitrary"),
                                             vmem_limit_bytes=VMEM_LIMIT),
        name="mixer_out",
    )(x, na, at, ga, gb, w_na_o, w_mla_o, w_out, g, bb)


def _prep_mixer_weights(w_in, b_gate, q_norm_g, kv_norm_g, w_uq, w_ukv):
    d = w_in.shape[0]
    half = MLA_ROPE // 2
    swap = lambda w: jnp.concatenate([-w[..., half:], w[..., :half]], axis=-1)
    pad_rope = lambda w: jnp.pad(w, ((0, 0), (MLA_NOPE, QK_PAD - MLA_QK)))
    c_kr = 3 * NA_WIDTH + MLA_Q_RANK + MLA_KV_RANK
    w_kr = w_in[:, c_kr:c_kr + MLA_ROPE]
    w_packed = jnp.concatenate([w_in[:, :c_kr], pad_rope(w_kr), pad_rope(swap(w_kr)),
                                w_in[:, c_kr + MLA_ROPE:]], axis=1)
    ukv = w_ukv.reshape(MLA_KV_RANK, MLA_HEADS, MLA_NOPE + MLA_V)
    wk = jnp.pad(ukv[..., :MLA_NOPE], ((0, 0), (0, 0), (0, QK_PAD - MLA_NOPE)))
    wk = wk.reshape(MLA_KV_RANK, MLA_HEADS * QK_PAD)
    wvt = ukv[..., MLA_NOPE:].reshape(MLA_KV_RANK, MLA_HEADS * MLA_V).T
    uq = w_uq.reshape(MLA_Q_RANK, MLA_HEADS, MLA_QK)
    wqt = jnp.pad(uq, ((0, 0), (0, 0), (0, QK_PAD - MLA_QK))).reshape(MLA_Q_RANK, MLA_HEADS * QK_PAD).T
    wqs = swap(uq[..., MLA_NOPE:]).reshape(MLA_Q_RANK, MLA_HEADS * MLA_ROPE).T
    return {
        "w_in": w_packed.astype(BF16), "b_gate": b_gate.reshape(1, 2 * d),
        "q_norm_g": q_norm_g.reshape(1, -1), "kv_norm_g": kv_norm_g.reshape(1, -1),
        "wk": wk.astype(BF16), "wvt": wvt.astype(BF16), "wqt": wqt.astype(BF16), "wqs": wqs.astype(BF16),
    }


def _rope_tables(n):
    inv = 1.0 / (ROPE_BASE ** (jnp.arange(0, MLA_ROPE, 2, dtype=F32) / MLA_ROPE))
    ang = jnp.arange(n, dtype=F32)[:, None] * inv[None, :]
    cos2 = jnp.tile(jnp.cos(ang), (1, 2))
    sin2 = jnp.tile(jnp.sin(ang), (1, 2))
    pad = ((0, 0), (MLA_NOPE, QK_PAD - MLA_QK))
    return {"ck": jnp.pad(cos2, pad), "sk": jnp.pad(sin2, pad), "cq": cos2.T, "sq": sin2.T}


def _na_bias_table(rpb):
    qc = np.arange(GRID_W)[:, None]
    kc = np.arange(GRID_W)[None, :]
    dc = np.clip(kc - qc + NA_KW - 1, 0, 2 * NA_KW - 2)
    onehot = (dc[None] == np.arange(2 * NA_KW - 1)[:, None, None]).astype(np.float32)
    win = np.clip(qc - NA_KW // 2, 0, GRID_W - NA_KW)
    in_win = (kc >= win) & (kc < win + NA_KW)
    t = jnp.einsum("hrc,cqk->hrqk", rpb, jnp.asarray(onehot), precision=lax.Precision.HIGHEST)
    t = jnp.where(in_win, t, NEG_BIG)
    bias = jnp.stack([t[:, NA_KH - 1 - dl:2 * NA_KH - 1 - dl] for dl in range(NA_KH)])
    bias = bias.transpose(0, 1, 3, 2, 4)
    return bias.reshape(NA_KH, NA_HEADS // 2, 2 * GRID_W, NA_KH * GRID_W)


def _encoder_layer(x, p, tabs):
    b, n, d = x.shape
    x1 = ffn_ln(x.reshape(b * n, d), p["ffn1_w_in"], p["ffn1_w_out"], p["ln1_g"], p["ln1_b"])
    x1 = x1.reshape(b, n, d)
    naq, nak, nav, qt, k, vt, ga, gb = mixer_in(x1, p["mixer"], tabs)
    na = na_attention(naq, nak, nav, p["na_bias"])
    at = mla_attention(qt, k, vt).reshape(b, MLA_HEADS * MLA_V, n)
    x2 = mixer_out(x1, na, at, ga, gb, p["w_na_o"], p["w_mla_o"], p["w_out"], p["ln2_g"], p["ln2_b"])
    y = ffn_ln(x2.reshape(b * n, d), p["ffn2_w_in"], p["ffn2_w_out"], p["ln3_g"], p["ln3_b"])
    return y.reshape(b, n, d)


def kernel(x_prompt, x_sample, ffn1_w_in, ffn1_w_out, ln1_g, ln1_b, w_in, b_gate, na_rpb, q_norm_g, kv_norm_g, w_uq, w_ukv, w_na_o, w_mla_o, w_out, ln2_g, ln2_b, ffn2_w_in, ffn2_w_out, ln3_g, ln3_b):
    l = 0
    row = lambda a: a[l].reshape(1, -1)
    p = {
        "ffn1_w_in": ffn1_w_in[l].astype(BF16), "ffn1_w_out": ffn1_w_out[l].astype(BF16),
        "ln1_g": row(ln1_g), "ln1_b": row(ln1_b),
        "mixer": _prep_mixer_weights(w_in[l], b_gate[l], q_norm_g[l], kv_norm_g[l], w_uq[l], w_ukv[l]),
        "na_bias": _na_bias_table(na_rpb[l]),
        "w_na_o": w_na_o[l].astype(BF16), "w_mla_o": w_mla_o[l].astype(BF16), "w_out": w_out[l].astype(BF16),
        "ln2_g": row(ln2_g), "ln2_b": row(ln2_b),
        "ffn2_w_in": ffn2_w_in[l].astype(BF16), "ffn2_w_out": ffn2_w_out[l].astype(BF16),
        "ln3_g": row(ln3_g), "ln3_b": row(ln3_b),
    }
    outs = []
    for x in (x_prompt, x_sample):
        outs.append(_encoder_layer(x, p, _rope_tables(x.shape[1])))
    return tuple(outs)
```

```python
import functools
import math

import numpy as np
import jax
import jax.numpy as jnp
from jax import lax
from jax.experimental import pallas as pl
from jax.experimental.pallas import tpu as pltpu

F32 = jnp.float32
BF16 = jnp.bfloat16

DEPTH = 1
GRID_W = 64
NA_HEADS = 8
NA_HEAD_DIM = 64
NA_WIDTH = NA_HEADS * NA_HEAD_DIM
NA_KH = 8
NA_KW = 16
MLA_HEADS = 8
MLA_NOPE = 64
MLA_ROPE = 32
MLA_QK = MLA_NOPE + MLA_ROPE
MLA_V = 64
MLA_VA = MLA_V + 16
MLA_Q_RANK = 384
MLA_KV_RANK = 256
ROPE_BASE = 10000.0
LN_EPS = 1e-5
RMS_EPS = 1e-6
ALPHA = (2.0 * DEPTH) ** 0.25
LOG2E = math.log2(math.e)

LANES = 128
QK_PAD = 128
VMEM_LIMIT = 56 * 1024 * 1024

TOKEN_TILE = 512
FFN_CHUNK = 256
NA_ROWS_PER_STEP = 32
MLA_TQ = 512
MLA_TK = 256
MLA_GROUP = 4
MLA_BLOCKS_PER_ITER = 6
NEG_BIG = -1e30


def _const_spec(shape):
    nd = len(shape)
    return pl.BlockSpec(shape, lambda *_: (0,) * nd, pipeline_mode=pl.Buffered(1))


def _layer_norm(y, g, b):
    mu = jnp.mean(y, axis=-1, keepdims=True)
    d = y - mu
    var = jnp.mean(d * d, axis=-1, keepdims=True)
    return d * lax.rsqrt(var + LN_EPS) * g + b


def _rms_norm(y, g):
    return y * lax.rsqrt(jnp.mean(y * y, axis=-1, keepdims=True) + RMS_EPS) * g


def _dot(a, b):
    return jnp.dot(a, b, preferred_element_type=F32)


def _swiglu_ln(x, w_in_ref, w_out_ref, g_ref, b_ref, h_ref, d_ff):
    xb = x.astype(BF16)
    for c in range(d_ff // FFN_CHUNK):
        lo, hi = c * FFN_CHUNK, (c + 1) * FFN_CHUNK
        a = _dot(xb, w_in_ref[:, lo:hi])
        u = _dot(xb, w_in_ref[:, d_ff + lo:d_ff + hi])
        h_ref[:, lo:hi] = (a * jax.nn.sigmoid(a) * u).astype(BF16)
    y = _dot(h_ref[...], w_out_ref[...])
    return _layer_norm(ALPHA * x + 0.5 * y, g_ref[...], b_ref[...])


def _ffn_ln_kernel(x_ref, w_in_ref, w_out_ref, g_ref, b_ref, o_ref, h_ref, *, d_ff):
    o_ref[...] = _swiglu_ln(x_ref[...], w_in_ref, w_out_ref, g_ref, b_ref, h_ref, d_ff)


def ffn_ln(x, w_in, w_out, g, b):
    t, d = x.shape
    d_ff = w_out.shape[0]
    tm = TOKEN_TILE
    return pl.pallas_call(
        functools.partial(_ffn_ln_kernel, d_ff=d_ff),
        out_shape=jax.ShapeDtypeStruct((t, d), F32),
        grid=(t // tm,),
        in_specs=[pl.BlockSpec((tm, d), lambda i: (i, 0)),
                  _const_spec(w_in.shape), _const_spec(w_out.shape),
                  _const_spec(g.shape), _const_spec(b.shape)],
        out_specs=pl.BlockSpec((tm, d), lambda i: (i, 0)),
        scratch_shapes=[pltpu.VMEM((tm, d_ff), BF16)],
        compiler_params=pltpu.CompilerParams(dimension_semantics=("arbitrary",),
                                             vmem_limit_bytes=VMEM_LIMIT),
        name="ffn_ln",
    )(x, w_in, w_out, g, b)


_C_NAQ, _C_NAK, _C_NAV = 0, NA_WIDTH, 2 * NA_WIDTH
_C_CQ = 3 * NA_WIDTH
_C_CKV = _C_CQ + MLA_Q_RANK
_C_KR = _C_CKV + MLA_KV_RANK
_C_GATE = _C_KR + 2 * QK_PAD


def _mixer_in_kernel(x_ref, w_ref, bg_ref, qg_ref, kvg_ref, wk_ref, wvt_ref, wqt_ref, wqs_ref,
                     ck_ref, sk_ref, cq_ref, sq_ref,
                     naq_ref, nak_ref, nav_ref, qt_ref, k_ref, vt_ref, ga_ref, gb_ref, *, d_model):
    xb = x_ref[...].astype(BF16)
    naq_ref[...] = (_dot(xb, w_ref[:, _C_NAQ:_C_NAK]) * (NA_HEAD_DIM ** -0.5)).astype(BF16)
    nak_ref[...] = _dot(xb, w_ref[:, _C_NAK:_C_NAV]).astype(BF16)
    nav_ref[...] = _dot(xb, w_ref[:, _C_NAV:_C_CQ]).astype(BF16)

    cqn = _rms_norm(_dot(xb, w_ref[:, _C_CQ:_C_CKV]), qg_ref[...])
    ckvn = _rms_norm(_dot(xb, w_ref[:, _C_CKV:_C_KR]), kvg_ref[...])

    kr = _dot(xb, w_ref[:, _C_KR:_C_GATE])
    kr_blk = kr[:, :QK_PAD] * ck_ref[...] + kr[:, QK_PAD:] * sk_ref[...]
    kall = _dot(ckvn.astype(BF16), wk_ref[...])
    for h in range(MLA_HEADS):
        k_ref[h] = (kall[:, h * QK_PAD:(h + 1) * QK_PAD] + kr_blk).astype(BF16)

    ckvn_t = ckvn.T.astype(BF16)
    vt = _dot(wvt_ref[...], ckvn_t)
    ones = jnp.ones((MLA_VA - MLA_V, vt.shape[1]), BF16)
    for h in range(MLA_HEADS):
        vt_ref[h, 0:MLA_V] = vt[h * MLA_V:(h + 1) * MLA_V].astype(BF16)
        vt_ref[h, MLA_V:MLA_VA] = ones

    cqn_t = cqn.T.astype(BF16)
    q_scale = (MLA_QK ** -0.5) * LOG2E
    qt = _dot(wqt_ref[...], cqn_t) * q_scale
    qs = _dot(wqs_ref[...], cqn_t) * q_scale
    cq, sq = cq_ref[...], sq_ref[...]
    zeros = jnp.zeros((QK_PAD - MLA_QK, qt.shape[1]), BF16)
    for h in range(MLA_HEADS):
        base = h * QK_PAD
        qt_ref[h, 0:MLA_NOPE] = qt[base:base + MLA_NOPE].astype(BF16)
        rope = qt[base + MLA_NOPE:base + MLA_QK] * cq + qs[h * MLA_ROPE:(h + 1) * MLA_ROPE] * sq
        qt_ref[h, MLA_NOPE:MLA_QK] = rope.astype(BF16)
        qt_ref[h, MLA_QK:QK_PAD] = zeros

    for half, out_ref in enumerate((ga_ref, gb_ref)):
        lo = _C_GATE + half * d_model
        logits = _dot(xb, w_ref[:, lo:lo + d_model]) + bg_ref[:, half * d_model:(half + 1) * d_model]
        out_ref[...] = jax.nn.sigmoid(logits).astype(BF16)


def mixer_in(x, wts, tabs):
    b, n, d = x.shape
    tm = TOKEN_TILE
    h = MLA_HEADS
    tok = lambda c: pl.BlockSpec((None, tm, c), lambda bi, i: (bi, i, 0))
    out_shape = (
        jax.ShapeDtypeStruct((b, n, NA_WIDTH), BF16), jax.ShapeDtypeStruct((b, n, NA_WIDTH), BF16),
        jax.ShapeDtypeStruct((b, n, NA_WIDTH), BF16),
        jax.ShapeDtypeStruct((b, h, QK_PAD, n), BF16),
        jax.ShapeDtypeStruct((b, h, n, QK_PAD), BF16),
        jax.ShapeDtypeStruct((b, h, MLA_VA, n), BF16),
        jax.ShapeDtypeStruct((b, n, d), BF16), jax.ShapeDtypeStruct((b, n, d), BF16),
    )
    out_specs = (
        tok(NA_WIDTH), tok(NA_WIDTH), tok(NA_WIDTH),
        pl.BlockSpec((None, h, QK_PAD, tm), lambda bi, i: (bi, 0, 0, i)),
        pl.BlockSpec((None, h, tm, QK_PAD), lambda bi, i: (bi, 0, i, 0)),
        pl.BlockSpec((None, h, MLA_VA, tm), lambda bi, i: (bi, 0, 0, i)),
        tok(d), tok(d),
    )
    consts = [wts["w_in"], wts["b_gate"], wts["q_norm_g"], wts["kv_norm_g"],
              wts["wk"], wts["wvt"], wts["wqt"], wts["wqs"]]
    in_specs = ([tok(d)] + [_const_spec(c.shape) for c in consts] + [
        pl.BlockSpec((tm, QK_PAD), lambda bi, i: (i, 0)),
        pl.BlockSpec((tm, QK_PAD), lambda bi, i: (i, 0)),
        pl.BlockSpec((MLA_ROPE, tm), lambda bi, i: (0, i)),
        pl.BlockSpec((MLA_ROPE, tm), lambda bi, i: (0, i)),
    ])
    return pl.pallas_call(
        functools.partial(_mixer_in_kernel, d_model=d),
        out_shape=out_shape,
        grid=(b, n // tm),
        in_specs=in_specs,
        out_specs=out_specs,
        compiler_params=pltpu.CompilerParams(dimension_semantics=("arbitrary", "arbitrary"),
                                             vmem_limit_bytes=VMEM_LIMIT),
        name="mixer_in",
    )(x, *consts, tabs["ck"], tabs["sk"], tabs["cq"], tabs["sq"])


def _na_kernel(q_ref, kp_ref, km_ref, kn_ref, vp_ref, vm_ref, vn_ref, bias_ref, o_ref, kbuf, vbuf,
               s0, s1, p0, p1, *, rows):
    g = NA_ROWS_PER_STEP
    halo = NA_KH * GRID_W
    main = g * GRID_W
    window = NA_KH * GRID_W
    npair = NA_HEADS // 2
    i = pl.program_id(1)
    kbuf[0:halo] = kp_ref[...]
    kbuf[halo:halo + main] = km_ref[...]
    kbuf[halo + main:] = kn_ref[...]
    vbuf[0:halo] = vp_ref[...]
    vbuf[halo:halo + main] = vm_ref[...]
    vbuf[halo + main:] = vn_ref[...]
    lane = lax.broadcasted_iota(jnp.int32, (GRID_W, LANES), 1)
    first_head = lane < NA_HEAD_DIM
    s_bufs, p_bufs = (s0, s1), (p0, p1)
    lanes_of = lambda hp: slice(hp * LANES, (hp + 1) * LANES)

    def geometry(rho):
        r = i * g + rho
        rs = jnp.clip(r - NA_KH // 2, 0, rows - NA_KH)
        off = pl.multiple_of((rs - (i * g - NA_KH)) * GRID_W, GRID_W)
        return r - rs, off, pl.multiple_of(rho * GRID_W, GRID_W)

    def scores(rho, slot):
        delta, off, qoff = geometry(rho)
        for hp in range(npair):
            q2 = q_ref[pl.ds(qoff, GRID_W), lanes_of(hp)]
            zero = jnp.zeros_like(q2)
            qs = jnp.concatenate([jnp.where(first_head, q2, zero),
                                  jnp.where(first_head, zero, q2)], axis=0)
            k2 = kbuf[pl.ds(off, window), lanes_of(hp)]
            s = lax.dot_general(qs, k2, (((1,), (1,)), ((), ())), preferred_element_type=F32)
            s_bufs[slot][hp] = s + bias_ref[delta, hp]

    def softmax(slot):
        for hp in range(npair):
            s = s_bufs[slot][hp]
            p = jnp.exp(s - jnp.max(s, axis=1, keepdims=True))
            p_bufs[slot][hp] = (p * (1.0 / jnp.sum(p, axis=1, keepdims=True))).astype(BF16)

    def values(rho, slot):
        _, off, qoff = geometry(rho)
        for hp in range(npair):
            o = _dot(p_bufs[slot][hp], vbuf[pl.ds(off, window), lanes_of(hp)])
            out2 = jnp.where(first_head, o[:GRID_W], o[GRID_W:])
            o_ref[pl.ds(qoff, GRID_W), lanes_of(hp)] = out2.astype(BF16)

    def block(t, slot, do_scores=True, do_softmax=True, do_values=True):
        if do_scores:
            scores(t + 1, 1 - slot)
        if do_softmax:
            softmax(slot)
        if do_values:
            values(t - 1, 1 - slot)

    assert g % 2 == 0 and g >= 4
    block(-1, 1, do_softmax=False, do_values=False)
    block(0, 0, do_values=False)

    def pair(j, carry):
        t = 2 * j + 1
        block(t, 1)
        block(t + 1, 0)
        return carry

    lax.fori_loop(0, (g - 2) // 2, pair, 0)
    block(g - 1, 1, do_scores=False)
    block(g, 0, do_scores=False, do_softmax=False)


def na_attention(q, k, v, bias):
    b, n, c = q.shape
    rows = n // GRID_W
    g = NA_ROWS_PER_STEP
    main = g * GRID_W
    halo = NA_KH * GRID_W
    per = main // halo
    last = n // halo - 1
    spec_main = pl.BlockSpec((None, main, c), lambda bi, i: (bi, i, 0))
    spec_prev = pl.BlockSpec((None, halo, c), lambda bi, i: (bi, jnp.maximum(i * per - 1, 0), 0))
    spec_next = pl.BlockSpec((None, halo, c), lambda bi, i: (bi, jnp.minimum((i + 1) * per, last), 0))
    return pl.pallas_call(
        functools.partial(_na_kernel, rows=rows),
        out_shape=jax.ShapeDtypeStruct((b, n, c), BF16),
        grid=(b, rows // g),
        in_specs=[spec_main, spec_prev, spec_main, spec_next, spec_prev, spec_main, spec_next,
                  _const_spec(bias.shape)],
        out_specs=spec_main,
        scratch_shapes=[pltpu.VMEM((main + 2 * halo, c), BF16), pltpu.VMEM((main + 2 * halo, c), BF16)]
        + [pltpu.VMEM(bias.shape[1:], F32)] * 2 + [pltpu.VMEM(bias.shape[1:], BF16)] * 2,
        compiler_params=pltpu.CompilerParams(dimension_semantics=("arbitrary", "arbitrary"),
                                             vmem_limit_bytes=VMEM_LIMIT),
        name="na_attention",
    )(q, k, k, k, v, v, v, bias)


def _mla_kernel(qt_ref, k_ref, vt_ref, o_ref, s0, s1, p0, p1, *, n):
    tq, tk, g = MLA_TQ, MLA_TK, MLA_GROUP
    ng = n // (tk * g)
    total = (n // tq) * ng
    assert total % 2 == 0 and total >= 4
    s_bufs, p_bufs = (s0, s1), (p0, p1)

    def split(u):
        return lax.div(u, ng), lax.rem(u, ng)

    def scores_chunk(u, slot, c):
        qi, kg = split(u)
        qt = qt_ref[:, pl.ds(pl.multiple_of(qi * tq, tq), tq)]
        off = pl.multiple_of((kg * g + c) * tk, tk)
        s = _dot(k_ref[pl.ds(off, tk), :], qt)
        s_bufs[slot][c * tk:(c + 1) * tk] = s
        return jnp.max(s, axis=0, keepdims=True)

    def first_group_reset(u, m):
        _, kg = split(u)
        return jnp.where(kg == 0, -jnp.inf, m)

    def softmax_chunk(slot, c, m, mx):
        m_new = jnp.maximum(m, mx)
        p_bufs[slot][c * tk:(c + 1) * tk] = jnp.exp2(s_bufs[slot][c * tk:(c + 1) * tk] - m_new).astype(BF16)
        return m_new, jnp.exp2(m - m_new)

    def values_chunk(u, slot, c, r, acc):
        _, kg = split(u)
        off = pl.multiple_of((kg * g + c) * tk, tk)
        return r * acc + _dot(vt_ref[:, pl.ds(off, tk)], p_bufs[slot][c * tk:(c + 1) * tk])

    def write_out(u, acc):
        qi, _ = split(u)
        out = acc[:MLA_V] * (1.0 / acc[MLA_V:MLA_V + 1])
        o_ref[:, pl.ds(pl.multiple_of(qi * tq, tq), tq)] = out.astype(BF16)

    def block(t, slot, carry, do_scores=True, do_softmax=True, do_values=True):
        m, mxs, rs_prev, acc = carry
        if do_softmax:
            m = first_group_reset(t, m)
        mxs_next, rs = list(mxs), list(rs_prev)
        if do_scores:
            for c in range(g):
                mxs_next[c] = scores_chunk(t + 1, 1 - slot, c)
        if do_softmax:
            for c in range(g):
                m, rs[c] = softmax_chunk(slot, c, m, mxs[c])
        if do_values:
            for c in range(g):
                acc = values_chunk(t - 1, 1 - slot, c, rs_prev[c], acc)
            write_out(t - 1, acc)
        return m, tuple(mxs_next), tuple(rs), acc

    row = jnp.zeros((1, tq), F32)
    carry = (row, (row,) * g, (row,) * g, jnp.zeros((MLA_VA, tq), F32))
    carry = block(-1, 1, carry, do_softmax=False, do_values=False)
    carry = block(0, 0, carry, do_values=False)

    unroll = MLA_BLOCKS_PER_ITER
    n_iter, n_rest = divmod(total - 2, unroll)

    def blocks(i, carry):
        t0 = unroll * i + 1
        for kk in range(unroll):
            carry = block(t0 + kk, (1 + kk) % 2, carry)
        return carry

    carry = lax.fori_loop(0, n_iter, blocks, carry)
    for t in range(n_iter * unroll + 1, n_iter * unroll + 1 + n_rest):
        carry = block(t, t % 2, carry)
    carry = block(total - 1, 1, carry, do_scores=False)
    block(total, 0, carry, do_scores=False, do_softmax=False)


def mla_attention(qt, k, vt):
    b, h, _, n = qt.shape
    rows = MLA_GROUP * MLA_TK
    head = lambda r, c: pl.BlockSpec((None, None, r, c), lambda bi, hi: (bi, hi, 0, 0))
    return pl.pallas_call(
        functools.partial(_mla_kernel, n=n),
        out_shape=jax.ShapeDtypeStruct((b, h, MLA_V, n), BF16),
        grid=(b, h),
        in_specs=[head(QK_PAD, n), head(n, QK_PAD), head(MLA_VA, n)],
        out_specs=head(MLA_V, n),
        scratch_shapes=[pltpu.VMEM((rows, MLA_TQ), F32), pltpu.VMEM((rows, MLA_TQ), F32),
                        pltpu.VMEM((rows, MLA_TQ), BF16), pltpu.VMEM((rows, MLA_TQ), BF16)],
        compiler_params=pltpu.CompilerParams(dimension_semantics=("arbitrary",) * 2,
                                             vmem_limit_bytes=VMEM_LIMIT),
        name="mla_attention",
    )(qt, k, vt)


def _mixer_out_kernel(x_ref, na_ref, at_ref, ga_ref, gb_ref, wna_ref, wmla_ref, wout_ref, g_ref, b_ref,
                      o_ref):
    ya = _dot(na_ref[...], wna_ref[...])
    yb = lax.dot_general(at_ref[...], wmla_ref[...], (((0,), (0,)), ((), ())),
                         preferred_element_type=F32)
    mix = ga_ref[...].astype(F32) * ya + gb_ref[...].astype(F32) * yb
    y = _dot(mix.astype(BF16), wout_ref[...])
    o_ref[...] = _layer_norm(ALPHA * x_ref[...] + y, g_ref[...], b_ref[...])


def mixer_out(x, na, at, ga, gb, w_na_o, w_mla_o, w_out, g, bb):
    b, n, d = x.shape
    tm = TOKEN_TILE
    tok = lambda c: pl.BlockSpec((None, tm, c), lambda bi, i: (bi, i, 0))
    return pl.pallas_call(
        _mixer_out_kernel,
        out_shape=jax.ShapeDtypeStruct((b, n, d), F32),
        grid=(b, n // tm),
        in_specs=[tok(d), tok(NA_WIDTH),
                  pl.BlockSpec((None, at.shape[1], tm), lambda bi, i: (bi, 0, i)),
                  tok(d), tok(d),
                  _const_spec(w_na_o.shape), _const_spec(w_mla_o.shape), _const_spec(w_out.shape),
                  _const_spec(g.shape), _const_spec(bb.shape)],
        out_specs=tok(d),
        compiler_params=pltpu.CompilerParams(dimension_semantics=("arbitrary", "arbitrary"),
                                             vmem_limit_bytes=VMEM_LIMIT),
        name="mixer_out",
    )(x, na, at, ga, gb, w_na_o, w_mla_o, w_out, g, bb)


def _prep_mixer_weights(w_in, b_gate, q_norm_g, kv_norm_g, w_uq, w_ukv):
    d = w_in.shape[0]
    half = MLA_ROPE // 2
    swap = lambda w: jnp.concatenate([-w[..., half:], w[..., :half]], axis=-1)
    pad_rope = lambda w: jnp.pad(w, ((0, 0), (MLA_NOPE, QK_PAD - MLA_QK)))
    c_kr = 3 * NA_WIDTH + MLA_Q_RANK + MLA_KV_RANK
    w_kr = w_in[:, c_kr:c_kr + MLA_ROPE]
    w_packed = jnp.concatenate([w_in[:, :c_kr], pad_rope(w_kr), pad_rope(swap(w_kr)),
                                w_in[:, c_kr + MLA_ROPE:]], axis=1)
    ukv = w_ukv.reshape(MLA_KV_RANK, MLA_HEADS, MLA_NOPE + MLA_V)
    wk = jnp.pad(ukv[..., :MLA_NOPE], ((0, 0), (0, 0), (0, QK_PAD - MLA_NOPE)))
    wk = wk.reshape(MLA_KV_RANK, MLA_HEADS * QK_PAD)
    wvt = ukv[..., MLA_NOPE:].reshape(MLA_KV_RANK, MLA_HEADS * MLA_V).T
    uq = w_uq.reshape(MLA_Q_RANK, MLA_HEADS, MLA_QK)
    wqt = jnp.pad(uq, ((0, 0), (0, 0), (0, QK_PAD - MLA_QK))).reshape(MLA_Q_RANK, MLA_HEADS * QK_PAD).T
    wqs = swap(uq[..., MLA_NOPE:]).reshape(MLA_Q_RANK, MLA_HEADS * MLA_ROPE).T
    return {
        "w_in": w_packed.astype(BF16), "b_gate": b_gate.reshape(1, 2 * d),
        "q_norm_g": q_norm_g.reshape(1, -1), "kv_norm_g": kv_norm_g.reshape(1, -1),
        "wk": wk.astype(BF16), "wvt": wvt.astype(BF16), "wqt": wqt.astype(BF16), "wqs": wqs.astype(BF16),
    }


def _rope_tables(n):
    inv = 1.0 / (ROPE_BASE ** (jnp.arange(0, MLA_ROPE, 2, dtype=F32) / MLA_ROPE))
    ang = jnp.arange(n, dtype=F32)[:, None] * inv[None, :]
    cos2 = jnp.tile(jnp.cos(ang), (1, 2))
    sin2 = jnp.tile(jnp.sin(ang), (1, 2))
    pad = ((0, 0), (MLA_NOPE, QK_PAD - MLA_QK))
    return {"ck": jnp.pad(cos2, pad), "sk": jnp.pad(sin2, pad), "cq": cos2.T, "sq": sin2.T}


def _na_bias_table(rpb):
    qc = np.arange(GRID_W)[:, None]
    kc = np.arange(GRID_W)[None, :]
    dc = np.clip(kc - qc + NA_KW - 1, 0, 2 * NA_KW - 2)
    onehot = (dc[None] == np.arange(2 * NA_KW - 1)[:, None, None]).astype(np.float32)
    win = np.clip(qc - NA_KW // 2, 0, GRID_W - NA_KW)
    in_win = (kc >= win) & (kc < win + NA_KW)
    t = jnp.einsum("hrc,cqk->hrqk", rpb, jnp.asarray(onehot), precision=lax.Precision.HIGHEST)
    t = jnp.where(in_win, t, NEG_BIG)
    bias = jnp.stack([t[:, NA_KH - 1 - dl:2 * NA_KH - 1 - dl] for dl in range(NA_KH)])
    bias = bias.transpose(0, 1, 3, 2, 4)
    return bias.reshape(NA_KH, NA_HEADS // 2, 2 * GRID_W, NA_KH * GRID_W)


def _encoder_layer(x, p, tabs):
    b, n, d = x.shape
    x1 = ffn_ln(x.reshape(b * n, d), p["ffn1_w_in"], p["ffn1_w_out"], p["ln1_g"], p["ln1_b"])
    x1 = x1.reshape(b, n, d)
    naq, nak, nav, qt, k, vt, ga, gb = mixer_in(x1, p["mixer"], tabs)
    na = na_attention(naq, nak, nav, p["na_bias"])
    at = mla_attention(qt, k, vt).reshape(b, MLA_HEADS * MLA_V, n)
    x2 = mixer_out(x1, na, at, ga, gb, p["w_na_o"], p["w_mla_o"], p["w_out"], p["ln2_g"], p["ln2_b"])
    y = ffn_ln(x2.reshape(b * n, d), p["ffn2_w_in"], p["ffn2_w_out"], p["ln3_g"], p["ln3_b"])
    return y.reshape(b, n, d)


def kernel(x_prompt, x_sample, ffn1_w_in, ffn1_w_out, ln1_g, ln1_b, w_in, b_gate, na_rpb, q_norm_g, kv_norm_g, w_uq, w_ukv, w_na_o, w_mla_o, w_out, ln2_g, ln2_b, ffn2_w_in, ffn2_w_out, ln3_g, ln3_b):
    l = 0
    row = lambda a: a[l].reshape(1, -1)
    p = {
        "ffn1_w_in": ffn1_w_in[l].astype(BF16), "ffn1_w_out": ffn1_w_out[l].astype(BF16),
        "ln1_g": row(ln1_g), "ln1_b": row(ln1_b),
        "mixer": _prep_mixer_weights(w_in[l], b_gate[l], q_norm_g[l], kv_norm_g[l], w_uq[l], w_ukv[l]),
        "na_bias": _na_bias_table(na_rpb[l]),
        "w_na_o": w_na_o[l].astype(BF16), "w_mla_o": w_mla_o[l].astype(BF16), "w_out": w_out[l].astype(BF16),
        "ln2_g": row(ln2_g), "ln2_b": row(ln2_b),
        "ffn2_w_in": ffn2_w_in[l].astype(BF16), "ffn2_w_out": ffn2_w_out[l].astype(BF16),
        "ln3_g": row(ln3_g), "ln3_b": row(ln3_b),
    }
    outs = []
    for x in (x_prompt, x_sample):
        outs.append(_encoder_layer(x, p, _rope_tables(x.shape[1])))
    return tuple(outs)
```

```python
import functools
import math

import numpy as np
import jax
import jax.numpy as jnp
from jax import lax
from jax.experimental import pallas as pl
from jax.experimental.pallas import tpu as pltpu

F32 = jnp.float32
BF16 = jnp.bfloat16

DEPTH = 1
GRID_W = 64
NA_HEADS = 8
NA_HEAD_DIM = 64
NA_WIDTH = NA_HEADS * NA_HEAD_DIM
NA_KH = 8
NA_KW = 16
MLA_HEADS = 8
MLA_NOPE = 64
MLA_ROPE = 32
MLA_QK = MLA_NOPE + MLA_ROPE
MLA_V = 64
MLA_VA = MLA_V + 16
MLA_Q_RANK = 384
MLA_KV_RANK = 256
ROPE_BASE = 10000.0
LN_EPS = 1e-5
RMS_EPS = 1e-6
ALPHA = (2.0 * DEPTH) ** 0.25
LOG2E = math.log2(math.e)

LANES = 128
QK_PAD = 128
VMEM_LIMIT = 56 * 1024 * 1024

TOKEN_TILE = 512
FFN_CHUNK = 256
NA_ROWS_PER_STEP = 32
MLA_TQ = 512
MLA_TK = 256
MLA_GROUP = 4
MLA_NBUF = 2
MLA_BLOCKS_PER_ITER = 6
MLA_BOUND_ROWS = 512
MLA_BOUND_LANES = 2048
MLA_PROBE_KEYS = 16
MLA_BOUND_INFLATE = 1.02
MLA_MAX_GAP = 64.0
NEG_BIG = -1e30


def _const_spec(shape):
    nd = len(shape)
    return pl.BlockSpec(shape, lambda *_: (0,) * nd, pipeline_mode=pl.Buffered(1))


def _layer_norm(y, g, b):
    mu = jnp.mean(y, axis=-1, keepdims=True)
    d = y - mu
    var = jnp.mean(d * d, axis=-1, keepdims=True)
    return d * lax.rsqrt(var + LN_EPS) * g + b


def _rms_norm(y, g):
    return y * lax.rsqrt(jnp.mean(y * y, axis=-1, keepdims=True) + RMS_EPS) * g


def _dot(a, b):
    return jnp.dot(a, b, preferred_element_type=F32)


def _swiglu_ln(x, w_in_ref, w_out_ref, g_ref, b_ref, h_ref, d_ff):
    xb = x.astype(BF16)
    for c in range(d_ff // FFN_CHUNK):
        lo, hi = c * FFN_CHUNK, (c + 1) * FFN_CHUNK
        a = _dot(xb, w_in_ref[:, lo:hi])
        u = _dot(xb, w_in_ref[:, d_ff + lo:d_ff + hi])
        h_ref[:, lo:hi] = (a * jax.nn.sigmoid(a) * u).astype(BF16)
    y = _dot(h_ref[...], w_out_ref[...])
    return _layer_norm(ALPHA * x + 0.5 * y, g_ref[...], b_ref[...])


def _ffn_ln_kernel(x_ref, w_in_ref, w_out_ref, g_ref, b_ref, o_ref, h_ref, *, d_ff):
    o_ref[...] = _swiglu_ln(x_ref[...], w_in_ref, w_out_ref, g_ref, b_ref, h_ref, d_ff)


def ffn_ln(x, w_in, w_out, g, b):
    t, d = x.shape
    d_ff = w_out.shape[0]
    tm = TOKEN_TILE
    return pl.pallas_call(
        functools.partial(_ffn_ln_kernel, d_ff=d_ff),
        out_shape=jax.ShapeDtypeStruct((t, d), F32),
        grid=(t // tm,),
        in_specs=[pl.BlockSpec((tm, d), lambda i: (i, 0)),
                  _const_spec(w_in.shape), _const_spec(w_out.shape),
                  _const_spec(g.shape), _const_spec(b.shape)],
        out_specs=pl.BlockSpec((tm, d), lambda i: (i, 0)),
        scratch_shapes=[pltpu.VMEM((tm, d_ff), BF16)],
        compiler_params=pltpu.CompilerParams(dimension_semantics=("arbitrary",),
                                             vmem_limit_bytes=VMEM_LIMIT),
        name="ffn_ln",
    )(x, w_in, w_out, g, b)


_C_NAQ, _C_NAK, _C_NAV = 0, NA_WIDTH, 2 * NA_WIDTH
_C_CQ = 3 * NA_WIDTH
_C_CKV = _C_CQ + MLA_Q_RANK
_C_KR = _C_CKV + MLA_KV_RANK
_C_GATE = _C_KR + 2 * QK_PAD


def _mixer_in_kernel(x_ref, w_ref, bg_ref, qg_ref, kvg_ref, wk_ref, wvt_ref, wqt_ref, wqs_ref,
                     ck_ref, sk_ref, cq_ref, sq_ref,
                     naq_ref, nak_ref, nav_ref, qt_ref, k_ref, vt_ref, ga_ref, gb_ref, *, d_model):
    xb = x_ref[...].astype(BF16)
    naq_ref[...] = (_dot(xb, w_ref[:, _C_NAQ:_C_NAK]) * (NA_HEAD_DIM ** -0.5)).astype(BF16)
    nak_ref[...] = _dot(xb, w_ref[:, _C_NAK:_C_NAV]).astype(BF16)
    nav_ref[...] = _dot(xb, w_ref[:, _C_NAV:_C_CQ]).astype(BF16)

    cqn = _rms_norm(_dot(xb, w_ref[:, _C_CQ:_C_CKV]), qg_ref[...])
    ckvn = _rms_norm(_dot(xb, w_ref[:, _C_CKV:_C_KR]), kvg_ref[...])

    kr = _dot(xb, w_ref[:, _C_KR:_C_GATE])
    kr_blk = kr[:, :QK_PAD] * ck_ref[...] + kr[:, QK_PAD:] * sk_ref[...]
    kall = _dot(ckvn.astype(BF16), wk_ref[...])
    for h in range(MLA_HEADS):
        k_ref[h] = (kall[:, h * QK_PAD:(h + 1) * QK_PAD] + kr_blk).astype(BF16)

    ckvn_t = ckvn.T.astype(BF16)
    vt = _dot(wvt_ref[...], ckvn_t)
    ones = jnp.ones((MLA_VA - MLA_V, vt.shape[1]), BF16)
    for h in range(MLA_HEADS):
        vt_ref[h, 0:MLA_V] = vt[h * MLA_V:(h + 1) * MLA_V].astype(BF16)
        vt_ref[h, MLA_V:MLA_VA] = ones

    cqn_t = cqn.T.astype(BF16)
    q_scale = (MLA_QK ** -0.5) * LOG2E
    qt = _dot(wqt_ref[...], cqn_t) * q_scale
    qs = _dot(wqs_ref[...], cqn_t) * q_scale
    cq, sq = cq_ref[...], sq_ref[...]
    zeros = jnp.zeros((QK_PAD - MLA_QK, qt.shape[1]), BF16)
    for h in range(MLA_HEADS):
        base = h * QK_PAD
        qt_ref[h, 0:MLA_NOPE] = qt[base:base + MLA_NOPE].astype(BF16)
        rope = qt[base + MLA_NOPE:base + MLA_QK] * cq + qs[h * MLA_ROPE:(h + 1) * MLA_ROPE] * sq
        qt_ref[h, MLA_NOPE:MLA_QK] = rope.astype(BF16)
        qt_ref[h, MLA_QK:QK_PAD] = zeros

    for half, out_ref in enumerate((ga_ref, gb_ref)):
        lo = _C_GATE + half * d_model
        logits = _dot(xb, w_ref[:, lo:lo + d_model]) + bg_ref[:, half * d_model:(half + 1) * d_model]
        out_ref[...] = jax.nn.sigmoid(logits).astype(BF16)


def mixer_in(x, wts, tabs):
    b, n, d = x.shape
    tm = TOKEN_TILE
    h = MLA_HEADS
    tok = lambda c: pl.BlockSpec((None, tm, c), lambda bi, i: (bi, i, 0))
    out_shape = (
        jax.ShapeDtypeStruct((b, n, NA_WIDTH), BF16), jax.ShapeDtypeStruct((b, n, NA_WIDTH), BF16),
        jax.ShapeDtypeStruct((b, n, NA_WIDTH), BF16),
        jax.ShapeDtypeStruct((b, h, QK_PAD, n), BF16),
        jax.ShapeDtypeStruct((b, h, n, QK_PAD), BF16),
        jax.ShapeDtypeStruct((b, h, MLA_VA, n), BF16),
        jax.ShapeDtypeStruct((b, n, d), BF16), jax.ShapeDtypeStruct((b, n, d), BF16),
    )
    out_specs = (
        tok(NA_WIDTH), tok(NA_WIDTH), tok(NA_WIDTH),
        pl.BlockSpec((None, h, QK_PAD, tm), lambda bi, i: (bi, 0, 0, i)),
        pl.BlockSpec((None, h, tm, QK_PAD), lambda bi, i: (bi, 0, i, 0)),
        pl.BlockSpec((None, h, MLA_VA, tm), lambda bi, i: (bi, 0, 0, i)),
        tok(d), tok(d),
    )
    consts = [wts["w_in"], wts["b_gate"], wts["q_norm_g"], wts["kv_norm_g"],
              wts["wk"], wts["wvt"], wts["wqt"], wts["wqs"]]
    in_specs = ([tok(d)] + [_const_spec(c.shape) for c in consts] + [
        pl.BlockSpec((tm, QK_PAD), lambda bi, i: (i, 0)),
        pl.BlockSpec((tm, QK_PAD), lambda bi, i: (i, 0)),
        pl.BlockSpec((MLA_ROPE, tm), lambda bi, i: (0, i)),
        pl.BlockSpec((MLA_ROPE, tm), lambda bi, i: (0, i)),
    ])
    return pl.pallas_call(
        functools.partial(_mixer_in_kernel, d_model=d),
        out_shape=out_shape,
        grid=(b, n // tm),
        in_specs=in_specs,
        out_specs=out_specs,
        compiler_params=pltpu.CompilerParams(dimension_semantics=("arbitrary", "arbitrary"),
                                             vmem_limit_bytes=VMEM_LIMIT),
        name="mixer_in",
    )(x, *consts, tabs["ck"], tabs["sk"], tabs["cq"], tabs["sq"])


def _na_kernel(q_ref, kp_ref, km_ref, kn_ref, vp_ref, vm_ref, vn_ref, bias_ref, o_ref, kbuf, vbuf,
               s0, s1, p0, p1, *, rows):
    g = NA_ROWS_PER_STEP
    halo = NA_KH * GRID_W
    main = g * GRID_W
    window = NA_KH * GRID_W
    npair = NA_HEADS // 2
    i = pl.program_id(1)
    kbuf[0:halo] = kp_ref[...]
    kbuf[halo:halo + main] = km_ref[...]
    kbuf[halo + main:] = kn_ref[...]
    vbuf[0:halo] = vp_ref[...]
    vbuf[halo:halo + main] = vm_ref[...]
    vbuf[halo + main:] = vn_ref[...]
    lane = lax.broadcasted_iota(jnp.int32, (GRID_W, LANES), 1)
    first_head = lane < NA_HEAD_DIM
    s_bufs, p_bufs = (s0, s1), (p0, p1)
    lanes_of = lambda hp: slice(hp * LANES, (hp + 1) * LANES)

    def geometry(rho):
        r = i * g + rho
        rs = jnp.clip(r - NA_KH // 2, 0, rows - NA_KH)
        off = pl.multiple_of((rs - (i * g - NA_KH)) * GRID_W, GRID_W)
        return r - rs, off, pl.multiple_of(rho * GRID_W, GRID_W)

    def scores(rho, slot):
        delta, off, qoff = geometry(rho)
        for hp in range(npair):
            q2 = q_ref[pl.ds(qoff, GRID_W), lanes_of(hp)]
            zero = jnp.zeros_like(q2)
            qs = jnp.concatenate([jnp.where(first_head, q2, zero),
                                  jnp.where(first_head, zero, q2)], axis=0)
            k2 = kbuf[pl.ds(off, window), lanes_of(hp)]
            s = lax.dot_general(qs, k2, (((1,), (1,)), ((), ())), preferred_element_type=F32)
            s_bufs[slot][hp] = s + bias_ref[delta, hp]

    def softmax(slot):
        for hp in range(npair):
            s = s_bufs[slot][hp]
            p = jnp.exp(s - jnp.max(s, axis=1, keepdims=True))
            p_bufs[slot][hp] = (p * (1.0 / jnp.sum(p, axis=1, keepdims=True))).astype(BF16)

    def values(rho, slot):
        _, off, qoff = geometry(rho)
        for hp in range(npair):
            o = _dot(p_bufs[slot][hp], vbuf[pl.ds(off, window), lanes_of(hp)])
            out2 = jnp.where(first_head, o[:GRID_W], o[GRID_W:])
            o_ref[pl.ds(qoff, GRID_W), lanes_of(hp)] = out2.astype(BF16)

    def block(t, slot, do_scores=True, do_softmax=True, do_values=True):
        if do_scores:
            scores(t + 1, 1 - slot)
        if do_softmax:
            softmax(slot)
        if do_values:
            values(t - 1, 1 - slot)

    assert g % 2 == 0 and g >= 4
    block(-1, 1, do_softmax=False, do_values=False)
    block(0, 0, do_values=False)

    def pair(j, carry):
        t = 2 * j + 1
        block(t, 1)
        block(t + 1, 0)
        return carry

    lax.fori_loop(0, (g - 2) // 2, pair, 0)
    block(g - 1, 1, do_scores=False)
    block(g, 0, do_scores=False, do_softmax=False)


def na_attention(q, k, v, bias):
    b, n, c = q.shape
    rows = n // GRID_W
    g = NA_ROWS_PER_STEP
    main = g * GRID_W
    halo = NA_KH * GRID_W
    per = main // halo
    last = n // halo - 1
    spec_main = pl.BlockSpec((None, main, c), lambda bi, i: (bi, i, 0))
    spec_prev = pl.BlockSpec((None, halo, c), lambda bi, i: (bi, jnp.maximum(i * per - 1, 0), 0))
    spec_next = pl.BlockSpec((None, halo, c), lambda bi, i: (bi, jnp.minimum((i + 1) * per, last), 0))
    return pl.pallas_call(
        functools.partial(_na_kernel, rows=rows),
        out_shape=jax.ShapeDtypeStruct((b, n, c), BF16),
        grid=(b, rows // g),
        in_specs=[spec_main, spec_prev, spec_main, spec_next, spec_prev, spec_main, spec_next,
                  _const_spec(bias.shape)],
        out_specs=spec_main,
        scratch_shapes=[pltpu.VMEM((main + 2 * halo, c), BF16), pltpu.VMEM((main + 2 * halo, c), BF16)]
        + [pltpu.VMEM(bias.shape[1:], F32)] * 2 + [pltpu.VMEM(bias.shape[1:], BF16)] * 2,
        compiler_params=pltpu.CompilerParams(dimension_semantics=("arbitrary", "arbitrary"),
                                             vmem_limit_bytes=VMEM_LIMIT),
        name="na_attention",
    )(q, k, k, k, v, v, v, bias)


def _mla_kernel(qt_ref, k_ref, vt_ref, o_ref, *bufs, n):
    tq, tk, g = MLA_TQ, MLA_TK, MLA_GROUP
    ng = n // (tk * g)
    total = (n // tq) * ng
    nbuf = MLA_NBUF
    assert total >= 4
    s_bufs, p_bufs = bufs[:nbuf], bufs[nbuf:]

    def split(u):
        return lax.div(u, ng), lax.rem(u, ng)

    def scores_chunk(u, slot, c):
        qi, kg = split(u)
        qt = qt_ref[:, pl.ds(pl.multiple_of(qi * tq, tq), tq)]
        off = pl.multiple_of((kg * g + c) * tk, tk)
        s = _dot(k_ref[pl.ds(off, tk), :], qt)
        s_bufs[slot][c * tk:(c + 1) * tk] = s
        return jnp.max(s, axis=0, keepdims=True)

    def first_group_reset(u, m):
        _, kg = split(u)
        return jnp.where(kg == 0, -jnp.inf, m)

    def softmax_chunk(slot, c, m, mx):
        m_new = jnp.maximum(m, mx)
        p_bufs[slot][c * tk:(c + 1) * tk] = jnp.exp2(s_bufs[slot][c * tk:(c + 1) * tk] - m_new).astype(BF16)
        return m_new, jnp.exp2(m - m_new)

    def values_chunk(u, slot, c, r, acc):
        _, kg = split(u)
        off = pl.multiple_of((kg * g + c) * tk, tk)
        return r * acc + _dot(vt_ref[:, pl.ds(off, tk)], p_bufs[slot][c * tk:(c + 1) * tk])

    def write_out(u, acc):
        qi, _ = split(u)
        out = acc[:MLA_V] * (1.0 / acc[MLA_V:MLA_V + 1])
        o_ref[:, pl.ds(pl.multiple_of(qi * tq, tq), tq)] = out.astype(BF16)

    def block(t, slot, carry, do_scores=True, do_softmax=True, do_values=True):
        m, mxs, rs_prev, acc = carry
        if do_softmax:
            m = first_group_reset(t, m)
        mxs_next, rs = list(mxs), list(rs_prev)
        if do_scores:
            for c in range(g):
                mxs_next[c] = scores_chunk(t + 1, (slot + 1) % nbuf, c)
        if do_softmax:
            for c in range(g):
                m, rs[c] = softmax_chunk(slot, c, m, mxs[c])
        if do_values:
            for c in range(g):
                acc = values_chunk(t - 1, (slot - 1) % nbuf, c, rs_prev[c], acc)
            write_out(t - 1, acc)
        return m, tuple(mxs_next), tuple(rs), acc

    row = jnp.zeros((1, tq), F32)
    carry = (row, (row,) * g, (row,) * g, jnp.zeros((MLA_VA, tq), F32))
    carry = block(-1, -1 % nbuf, carry, do_softmax=False, do_values=False)
    carry = block(0, 0, carry, do_values=False)

    unroll = MLA_BLOCKS_PER_ITER
    assert unroll % nbuf == 0
    n_iter, n_rest = divmod(total - 2, unroll)

    def blocks(i, carry):
        t0 = unroll * i + 1
        for kk in range(unroll):
            carry = block(t0 + kk, (1 + kk) % nbuf, carry)
        return carry

    carry = lax.fori_loop(0, n_iter, blocks, carry)
    for t in range(n_iter * unroll + 1, n_iter * unroll + 1 + n_rest):
        carry = block(t, t % nbuf, carry)
    carry = block(total - 1, (total - 1) % nbuf, carry, do_scores=False)
    block(total, total % nbuf, carry, do_scores=False, do_softmax=False)


def _mla_bound_kernel(qt_ref, k_ref, m_ref, gap_ref, *, n):
    rows = MLA_BOUND_ROWS
    lanes = MLA_BOUND_LANES

    def key_body(j, best):
        kk = k_ref[pl.ds(pl.multiple_of(j * rows, rows), rows), :].astype(F32)
        return jnp.maximum(best, jnp.max(jnp.sum(kk * kk, axis=1, keepdims=True), axis=0, keepdims=True))

    k2max = lax.fori_loop(0, n // rows, key_body, jnp.zeros((1, 1), F32))
    k_probe = k_ref[0:MLA_PROBE_KEYS, :]

    def query_body(j, gap):
        off = pl.multiple_of(j * lanes, lanes)
        qt = qt_ref[:, pl.ds(off, lanes)]
        qf = qt.astype(F32)
        q2 = jnp.sum(qf * qf, axis=0, keepdims=True)
        m = jnp.sqrt(q2 * k2max) * MLA_BOUND_INFLATE
        lb = jnp.max(_dot(k_probe, qt), axis=0, keepdims=True)
        m_ref[:, pl.ds(off, lanes)] = m
        return jnp.maximum(gap, jnp.max(m - lb, axis=1, keepdims=True))

    gap = lax.fori_loop(0, n // lanes, query_body, jnp.full((1, 1), -jnp.inf, F32))
    gap_ref[...] = jnp.broadcast_to(gap, gap_ref.shape)


def _mla_fast_kernel(qt_ref, k_ref, vt_ref, m_ref, o_ref, p0, p1, *, n):
    tq, tk, g = MLA_TQ, MLA_TK, MLA_GROUP
    ng = n // (tk * g)
    total = (n // tq) * ng
    p_bufs = (p0, p1)

    def split(u):
        return lax.div(u, ng), lax.rem(u, ng)

    def probs(u, slot):
        qi, kg = split(u)
        qoff = pl.multiple_of(qi * tq, tq)
        qt = qt_ref[:, pl.ds(qoff, tq)]
        m = m_ref[:, pl.ds(qoff, tq)]
        for c in range(g):
            off = pl.multiple_of((kg * g + c) * tk, tk)
            s = _dot(k_ref[pl.ds(off, tk), :], qt)
            p_bufs[slot][c * tk:(c + 1) * tk] = jnp.exp2(s - m).astype(BF16)

    def values(u, slot, acc):
        qi, kg = split(u)
        acc = jnp.where(kg == 0, 0.0, acc)
        for c in range(g):
            off = pl.multiple_of((kg * g + c) * tk, tk)
            acc = acc + _dot(vt_ref[:, pl.ds(off, tk)], p_bufs[slot][c * tk:(c + 1) * tk])
        out = acc[:MLA_V] * (1.0 / acc[MLA_V:MLA_V + 1])
        o_ref[:, pl.ds(pl.multiple_of(qi * tq, tq), tq)] = out.astype(BF16)
        return acc

    def block(t, slot, acc):
        probs(t + 1, 1 - slot)
        return values(t, slot, acc)

    unroll = MLA_BLOCKS_PER_ITER
    assert unroll % 2 == 0 and total >= 2
    probs(0, 0)
    acc = block(0, 0, jnp.zeros((MLA_VA, tq), F32))
    n_iter, n_rest = divmod(total - 2, unroll)

    def blocks(i, acc):
        t0 = unroll * i + 1
        for kk in range(unroll):
            acc = block(t0 + kk, (1 + kk) % 2, acc)
        return acc

    acc = lax.fori_loop(0, n_iter, blocks, acc)
    for t in range(n_iter * unroll + 1, total - 1):
        acc = block(t, t % 2, acc)
    values(total - 1, (total - 1) % 2, acc)


def mla_attention(qt, k, vt):
    b, h, _, n = qt.shape
    rows = MLA_GROUP * MLA_TK
    head = lambda r, c: pl.BlockSpec((None, None, r, c), lambda bi, hi: (bi, hi, 0, 0))
    params = pltpu.CompilerParams(dimension_semantics=("arbitrary",) * 2, vmem_limit_bytes=VMEM_LIMIT)
    out_shape = jax.ShapeDtypeStruct((b, h, MLA_V, n), BF16)

    m, gap = pl.pallas_call(
        functools.partial(_mla_bound_kernel, n=n),
        out_shape=(jax.ShapeDtypeStruct((b, h, 1, n), F32), jax.ShapeDtypeStruct((b, h, 1, LANES), F32)),
        grid=(b, h),
        in_specs=[head(QK_PAD, n), head(n, QK_PAD)],
        out_specs=(head(1, n), head(1, LANES)),
        compiler_params=params,
        name="mla_bound",
    )(qt, k)

    def fast(qt, k, vt, m):
        return pl.pallas_call(
            functools.partial(_mla_fast_kernel, n=n),
            out_shape=out_shape,
            grid=(b, h),
            in_specs=[head(QK_PAD, n), head(n, QK_PAD), head(MLA_VA, n), head(1, n)],
            out_specs=head(MLA_V, n),
            scratch_shapes=[pltpu.VMEM((rows, MLA_TQ), BF16)] * 2,
            compiler_params=params,
            name="mla_attention_fast",
        )(qt, k, vt, m)

    def exact(qt, k, vt, m):
        return pl.pallas_call(
            functools.partial(_mla_kernel, n=n),
            out_shape=out_shape,
            grid=(b, h),
            in_specs=[head(QK_PAD, n), head(n, QK_PAD), head(MLA_VA, n)],
            out_specs=head(MLA_V, n),
            scratch_shapes=[pltpu.VMEM((rows, MLA_TQ), F32)] * MLA_NBUF + [pltpu.VMEM((rows, MLA_TQ), BF16)] * MLA_NBUF,
            compiler_params=params,
            name="mla_attention",
        )(qt, k, vt)

    return lax.cond(jnp.max(gap) < MLA_MAX_GAP, fast, exact, qt, k, vt, m)


def _mixer_out_kernel(x_ref, na_ref, at_ref, ga_ref, gb_ref, wna_ref, wmla_ref, wout_ref, g_ref, b_ref,
                      o_ref):
    ya = _dot(na_ref[...], wna_ref[...])
    yb = lax.dot_general(at_ref[...], wmla_ref[...], (((0,), (0,)), ((), ())),
                         preferred_element_type=F32)
    mix = ga_ref[...].astype(F32) * ya + gb_ref[...].astype(F32) * yb
    y = _dot(mix.astype(BF16), wout_ref[...])
    o_ref[...] = _layer_norm(ALPHA * x_ref[...] + y, g_ref[...], b_ref[...])


def mixer_out(x, na, at, ga, gb, w_na_o, w_mla_o, w_out, g, bb):
    b, n, d = x.shape
    tm = TOKEN_TILE
    tok = lambda c: pl.BlockSpec((None, tm, c), lambda bi, i: (bi, i, 0))
    return pl.pallas_call(
        _mixer_out_kernel,
        out_shape=jax.ShapeDtypeStruct((b, n, d), F32),
        grid=(b, n // tm),
        in_specs=[tok(d), tok(NA_WIDTH),
                  pl.BlockSpec((None, at.shape[1], tm), lambda bi, i: (bi, 0, i)),
                  tok(d), tok(d),
                  _const_spec(w_na_o.shape), _const_spec(w_mla_o.shape), _const_spec(w_out.shape),
                  _const_spec(g.shape), _const_spec(bb.shape)],
        out_specs=tok(d),
        compiler_params=pltpu.CompilerParams(dimension_semantics=("arbitrary", "arbitrary"),
                                             vmem_limit_bytes=VMEM_LIMIT),
        name="mixer_out",
    )(x, na, at, ga, gb, w_na_o, w_mla_o, w_out, g, bb)


def _prep_mixer_weights(w_in, b_gate, q_norm_g, kv_norm_g, w_uq, w_ukv):
    d = w_in.shape[0]
    half = MLA_ROPE // 2
    swap = lambda w: jnp.concatenate([-w[..., half:], w[..., :half]], axis=-1)
    pad_rope = lambda w: jnp.pad(w, ((0, 0), (MLA_NOPE, QK_PAD - MLA_QK)))
    c_kr = 3 * NA_WIDTH + MLA_Q_RANK + MLA_KV_RANK
    w_kr = w_in[:, c_kr:c_kr + MLA_ROPE]
    w_packed = jnp.concatenate([w_in[:, :c_kr], pad_rope(w_kr), pad_rope(swap(w_kr)),
                                w_in[:, c_kr + MLA_ROPE:]], axis=1)
    ukv = w_ukv.reshape(MLA_KV_RANK, MLA_HEADS, MLA_NOPE + MLA_V)
    wk = jnp.pad(ukv[..., :MLA_NOPE], ((0, 0), (0, 0), (0, QK_PAD - MLA_NOPE)))
    wk = wk.reshape(MLA_KV_RANK, MLA_HEADS * QK_PAD)
    wvt = ukv[..., MLA_NOPE:].reshape(MLA_KV_RANK, MLA_HEADS * MLA_V).T
    uq = w_uq.reshape(MLA_Q_RANK, MLA_HEADS, MLA_QK)
    wqt = jnp.pad(uq, ((0, 0), (0, 0), (0, QK_PAD - MLA_QK))).reshape(MLA_Q_RANK, MLA_HEADS * QK_PAD).T
    wqs = swap(uq[..., MLA_NOPE:]).reshape(MLA_Q_RANK, MLA_HEADS * MLA_ROPE).T
    return {
        "w_in": w_packed.astype(BF16), "b_gate": b_gate.reshape(1, 2 * d),
        "q_norm_g": q_norm_g.reshape(1, -1), "kv_norm_g": kv_norm_g.reshape(1, -1),
        "wk": wk.astype(BF16), "wvt": wvt.astype(BF16), "wqt": wqt.astype(BF16), "wqs": wqs.astype(BF16),
    }


def _rope_tables(n):
    inv = 1.0 / (ROPE_BASE ** (jnp.arange(0, MLA_ROPE, 2, dtype=F32) / MLA_ROPE))
    ang = jnp.arange(n, dtype=F32)[:, None] * inv[None, :]
    cos2 = jnp.tile(jnp.cos(ang), (1, 2))
    sin2 = jnp.tile(jnp.sin(ang), (1, 2))
    pad = ((0, 0), (MLA_NOPE, QK_PAD - MLA_QK))
    return {"ck": jnp.pad(cos2, pad), "sk": jnp.pad(sin2, pad), "cq": cos2.T, "sq": sin2.T}


def _na_bias_table(rpb):
    qc = np.arange(GRID_W)[:, None]
    kc = np.arange(GRID_W)[None, :]
    dc = np.clip(kc - qc + NA_KW - 1, 0, 2 * NA_KW - 2)
    onehot = (dc[None] == np.arange(2 * NA_KW - 1)[:, None, None]).astype(np.float32)
    win = np.clip(qc - NA_KW // 2, 0, GRID_W - NA_KW)
    in_win = (kc >= win) & (kc < win + NA_KW)
    t = jnp.einsum("hrc,cqk->hrqk", rpb, jnp.asarray(onehot), precision=lax.Precision.HIGHEST)
    t = jnp.where(in_win, t, NEG_BIG)
    bias = jnp.stack([t[:, NA_KH - 1 - dl:2 * NA_KH - 1 - dl] for dl in range(NA_KH)])
    bias = bias.transpose(0, 1, 3, 2, 4)
    return bias.reshape(NA_KH, NA_HEADS // 2, 2 * GRID_W, NA_KH * GRID_W)


def _encoder_layer(x, p, tabs):
    b, n, d = x.shape
    x1 = ffn_ln(x.reshape(b * n, d), p["ffn1_w_in"], p["ffn1_w_out"], p["ln1_g"], p["ln1_b"])
    x1 = x1.reshape(b, n, d)
    naq, nak, nav, qt, k, vt, ga, gb = mixer_in(x1, p["mixer"], tabs)
    na = na_attention(naq, nak, nav, p["na_bias"])
    at = mla_attention(qt, k, vt).reshape(b, MLA_HEADS * MLA_V, n)
    x2 = mixer_out(x1, na, at, ga, gb, p["w_na_o"], p["w_mla_o"], p["w_out"], p["ln2_g"], p["ln2_b"])
    y = ffn_ln(x2.reshape(b * n, d), p["ffn2_w_in"], p["ffn2_w_out"], p["ln3_g"], p["ln3_b"])
    return y.reshape(b, n, d)


def kernel(x_prompt, x_sample, ffn1_w_in, ffn1_w_out, ln1_g, ln1_b, w_in, b_gate, na_rpb, q_norm_g, kv_norm_g, w_uq, w_ukv, w_na_o, w_mla_o, w_out, ln2_g, ln2_b, ffn2_w_in, ffn2_w_out, ln3_g, ln3_b):
    l = 0
    row = lambda a: a[l].reshape(1, -1)
    p = {
        "ffn1_w_in": ffn1_w_in[l].astype(BF16), "ffn1_w_out": ffn1_w_out[l].astype(BF16),
        "ln1_g": row(ln1_g), "ln1_b": row(ln1_b),
        "mixer": _prep_mixer_weights(w_in[l], b_gate[l], q_norm_g[l], kv_norm_g[l], w_uq[l], w_ukv[l]),
        "na_bias": _na_bias_table(na_rpb[l]),
        "w_na_o": w_na_o[l].astype(BF16), "w_mla_o": w_mla_o[l].astype(BF16), "w_out": w_out[l].astype(BF16),
        "ln2_g": row(ln2_g), "ln2_b": row(ln2_b),
        "ffn2_w_in": ffn2_w_in[l].astype(BF16), "ffn2_w_out": ffn2_w_out[l].astype(BF16),
        "ln3_g": row(ln3_g), "ln3_b": row(ln3_b),
    }
    outs = []
    for x in (x_prompt, x_sample):
        outs.append(_encoder_layer(x, p, _rope_tables(x.shape[1])))
    return tuple(outs)
```

```python
import functools
import math

import numpy as np
import jax
import jax.numpy as jnp
from jax import lax
from jax.experimental import pallas as pl
from jax.experimental.pallas import tpu as pltpu

F32 = jnp.float32
BF16 = jnp.bfloat16

DEPTH = 1
GRID_W = 64
NA_HEADS = 8
NA_HEAD_DIM = 64
NA_WIDTH = NA_HEADS * NA_HEAD_DIM
NA_KH = 8
NA_KW = 16
MLA_HEADS = 8
MLA_NOPE = 64
MLA_ROPE = 32
MLA_QK = MLA_NOPE + MLA_ROPE
MLA_V = 64
MLA_VA = MLA_V + 16
MLA_Q_RANK = 384
MLA_KV_RANK = 256
ROPE_BASE = 10000.0
LN_EPS = 1e-5
RMS_EPS = 1e-6
ALPHA = (2.0 * DEPTH) ** 0.25
LOG2E = math.log2(math.e)

LANES = 128
QK_PAD = 128
VMEM_LIMIT = 56 * 1024 * 1024

TOKEN_TILE = 512
FFN_TILE = 1024
FFN_CHUNK = 256
MIX_CHUNK = 256
NA_ROWS_PER_STEP = 32
NA_BLOCKS_PER_ITER = 6
MLA_TQ = 512
MLA_TK = 256
MLA_GROUP = 4
MLA_NBUF = 2
MLA_BLOCKS_PER_ITER = 6
MLA_PROBE_KEYS = 16
MLA_BOUND_INFLATE = 1.02
MLA_MAX_GAP = 64.0
NEG_BIG = -1e30


def _const_spec(shape):
    nd = len(shape)
    return pl.BlockSpec(shape, lambda *_: (0,) * nd, pipeline_mode=pl.Buffered(1))


def _layer_norm(y, g, b):
    mu = jnp.mean(y, axis=-1, keepdims=True)
    d = y - mu
    var = jnp.mean(d * d, axis=-1, keepdims=True)
    return d * lax.rsqrt(var + LN_EPS) * g + b


def _rms_norm(y, g):
    return y * lax.rsqrt(jnp.mean(y * y, axis=-1, keepdims=True) + RMS_EPS) * g


def _dot(a, b):
    return jnp.dot(a, b, preferred_element_type=F32)


def _swiglu_ln(x, w_in_ref, w_out_ref, g_ref, b_ref, h_ref, d_ff):
    xb = x.astype(BF16)
    for c in range(d_ff // FFN_CHUNK):
        lo, hi = c * FFN_CHUNK, (c + 1) * FFN_CHUNK
        a = _dot(xb, w_in_ref[:, lo:hi])
        u = _dot(xb, w_in_ref[:, d_ff + lo:d_ff + hi])
        h_ref[:, lo:hi] = (a * jax.nn.sigmoid(a) * u).astype(BF16)
    y = _dot(h_ref[...], w_out_ref[...])
    return _layer_norm(ALPHA * x + 0.5 * y, g_ref[...], b_ref[...])


def _ffn_ln_kernel(x_ref, w_in_ref, w_out_ref, g_ref, b_ref, o_ref, h_ref, *, d_ff):
    o_ref[...] = _swiglu_ln(x_ref[...], w_in_ref, w_out_ref, g_ref, b_ref, h_ref, d_ff)


def ffn_ln(x, w_in, w_out, g, b):
    t, d = x.shape
    d_ff = w_out.shape[0]
    tm = FFN_TILE
    return pl.pallas_call(
        functools.partial(_ffn_ln_kernel, d_ff=d_ff),
        out_shape=jax.ShapeDtypeStruct((t, d), F32),
        grid=(t // tm,),
        in_specs=[pl.BlockSpec((tm, d), lambda i: (i, 0)),
                  _const_spec(w_in.shape), _const_spec(w_out.shape),
                  _const_spec(g.shape), _const_spec(b.shape)],
        out_specs=pl.BlockSpec((tm, d), lambda i: (i, 0)),
        scratch_shapes=[pltpu.VMEM((tm, d_ff), BF16)],
        compiler_params=pltpu.CompilerParams(dimension_semantics=("arbitrary",),
                                             vmem_limit_bytes=VMEM_LIMIT),
        name="ffn_ln",
    )(x, w_in, w_out, g, b)


_C_NAQ, _C_NAK, _C_NAV = 0, NA_WIDTH, 2 * NA_WIDTH
_C_CQ = 3 * NA_WIDTH
_C_CKV = _C_CQ + MLA_Q_RANK
_C_KR = _C_CKV + MLA_KV_RANK
_C_GATE = _C_KR + 2 * QK_PAD


def _mixer_in_kernel(x_ref, w_ref, bg_ref, qg_ref, kvg_ref, wk_ref, wvt_ref, wqt_ref, wqs_ref,
                     ck_ref, sk_ref, cq_ref, sq_ref,
                     naq_ref, nak_ref, nav_ref, qt_ref, k_ref, vt_ref, ga_ref, gb_ref,
                     qn2_ref, lb_ref, kn2_ref, *, d_model):
    xb = x_ref[...].astype(BF16)
    naq_ref[...] = (_dot(xb, w_ref[:, _C_NAQ:_C_NAK]) * (NA_HEAD_DIM ** -0.5)).astype(BF16)
    nak_ref[...] = _dot(xb, w_ref[:, _C_NAK:_C_NAV]).astype(BF16)
    nav_ref[...] = _dot(xb, w_ref[:, _C_NAV:_C_CQ]).astype(BF16)

    cqn = _rms_norm(_dot(xb, w_ref[:, _C_CQ:_C_CKV]), qg_ref[...])
    ckvn = _rms_norm(_dot(xb, w_ref[:, _C_CKV:_C_KR]), kvg_ref[...])

    kr = _dot(xb, w_ref[:, _C_KR:_C_GATE])
    kr_blk = kr[:, :QK_PAD] * ck_ref[...] + kr[:, QK_PAD:] * sk_ref[...]
    kall = _dot(ckvn.astype(BF16), wk_ref[...])
    k_probe = []
    for h in range(MLA_HEADS):
        kf = kall[:, h * QK_PAD:(h + 1) * QK_PAD] + kr_blk
        kb = kf.astype(BF16)
        k_ref[h] = kb
        k2 = jnp.max(jnp.sum(kf * kf, axis=1, keepdims=True), axis=0, keepdims=True)
        kn2_ref[h] = jnp.broadcast_to(k2, (1, LANES))
        k_probe.append(kb[0:MLA_PROBE_KEYS])

    ckvn_t = ckvn.T.astype(BF16)
    vt = _dot(wvt_ref[...], ckvn_t)
    ones = jnp.ones((MLA_VA - MLA_V, vt.shape[1]), BF16)
    for h in range(MLA_HEADS):
        vt_ref[h, 0:MLA_V] = vt[h * MLA_V:(h + 1) * MLA_V].astype(BF16)
        vt_ref[h, MLA_V:MLA_VA] = ones

    cqn_t = cqn.T.astype(BF16)
    q_scale = (MLA_QK ** -0.5) * LOG2E
    qt = _dot(wqt_ref[...], cqn_t) * q_scale
    qs = _dot(wqs_ref[...], cqn_t) * q_scale
    cq, sq = cq_ref[...], sq_ref[...]
    zeros = jnp.zeros((QK_PAD - MLA_QK, qt.shape[1]), BF16)
    for h in range(MLA_HEADS):
        base = h * QK_PAD
        nope = qt[base:base + MLA_NOPE]
        rope = qt[base + MLA_NOPE:base + MLA_QK] * cq + qs[h * MLA_ROPE:(h + 1) * MLA_ROPE] * sq
        qh = jnp.concatenate([nope.astype(BF16), rope.astype(BF16), zeros], axis=0)
        qt_ref[h] = qh
        qn2_ref[h] = (jnp.sum(nope * nope, axis=0, keepdims=True)
                      + jnp.sum(rope * rope, axis=0, keepdims=True))
        lb_ref[h] = jnp.max(_dot(k_probe[h], qh), axis=0, keepdims=True)

    for half, out_ref in enumerate((ga_ref, gb_ref)):
        lo = _C_GATE + half * d_model
        logits = _dot(xb, w_ref[:, lo:lo + d_model]) + bg_ref[:, half * d_model:(half + 1) * d_model]
        out_ref[...] = jax.nn.sigmoid(logits).astype(BF16)


def mixer_in(x, wts, tabs):
    b, n, d = x.shape
    tm = TOKEN_TILE
    h = MLA_HEADS
    tok = lambda c: pl.BlockSpec((None, tm, c), lambda bi, i: (bi, i, 0))
    out_shape = (
        jax.ShapeDtypeStruct((b, n, NA_WIDTH), BF16), jax.ShapeDtypeStruct((b, n, NA_WIDTH), BF16),
        jax.ShapeDtypeStruct((b, n, NA_WIDTH), BF16),
        jax.ShapeDtypeStruct((b, h, QK_PAD, n), BF16),
        jax.ShapeDtypeStruct((b, h, n, QK_PAD), BF16),
        jax.ShapeDtypeStruct((b, h, MLA_VA, n), BF16),
        jax.ShapeDtypeStruct((b, n, d), BF16), jax.ShapeDtypeStruct((b, n, d), BF16),
        jax.ShapeDtypeStruct((b, h, 1, n), F32), jax.ShapeDtypeStruct((b, h, 1, n), F32),
        jax.ShapeDtypeStruct((b, n // tm, h, 1, LANES), F32),
    )
    stat = pl.BlockSpec((None, h, 1, tm), lambda bi, i: (bi, 0, 0, i))
    out_specs = (
        tok(NA_WIDTH), tok(NA_WIDTH), tok(NA_WIDTH),
        pl.BlockSpec((None, h, QK_PAD, tm), lambda bi, i: (bi, 0, 0, i)),
        pl.BlockSpec((None, h, tm, QK_PAD), lambda bi, i: (bi, 0, i, 0)),
        pl.BlockSpec((None, h, MLA_VA, tm), lambda bi, i: (bi, 0, 0, i)),
        tok(d), tok(d),
        stat, stat,
        pl.BlockSpec((None, None, h, 1, LANES), lambda bi, i: (bi, i, 0, 0, 0)),
    )
    consts = [wts["w_in"], wts["b_gate"], wts["q_norm_g"], wts["kv_norm_g"],
              wts["wk"], wts["wvt"], wts["wqt"], wts["wqs"]]
    in_specs = ([tok(d)] + [_const_spec(c.shape) for c in consts] + [
        pl.BlockSpec((tm, QK_PAD), lambda bi, i: (i, 0)),
        pl.BlockSpec((tm, QK_PAD), lambda bi, i: (i, 0)),
        pl.BlockSpec((MLA_ROPE, tm), lambda bi, i: (0, i)),
        pl.BlockSpec((MLA_ROPE, tm), lambda bi, i: (0, i)),
    ])
    return pl.pallas_call(
        functools.partial(_mixer_in_kernel, d_model=d),
        out_shape=out_shape,
        grid=(b, n // tm),
        in_specs=in_specs,
        out_specs=out_specs,
        compiler_params=pltpu.CompilerParams(dimension_semantics=("arbitrary", "arbitrary"),
                                             vmem_limit_bytes=VMEM_LIMIT),
        name="mixer_in",
    )(x, *consts, tabs["ck"], tabs["sk"], tabs["cq"], tabs["sq"])


def _na_kernel(q_ref, kp_ref, km_ref, kn_ref, vp_ref, vm_ref, vn_ref, bias_ref, o_ref, kbuf, vbuf,
               s0, s1, p0, p1, *, rows):
    g = NA_ROWS_PER_STEP
    halo = NA_KH * GRID_W
    main = g * GRID_W
    window = NA_KH * GRID_W
    npair = NA_HEADS // 2
    i = pl.program_id(1)
    kbuf[0:halo] = kp_ref[...]
    kbuf[halo:halo + main] = km_ref[...]
    kbuf[halo + main:] = kn_ref[...]
    vbuf[0:halo] = vp_ref[...]
    vbuf[halo:halo + main] = vm_ref[...]
    vbuf[halo + main:] = vn_ref[...]
    lane = lax.broadcasted_iota(jnp.int32, (GRID_W, LANES), 1)
    first_head = lane < NA_HEAD_DIM
    s_bufs, p_bufs = (s0, s1), (p0, p1)
    lanes_of = lambda hp: slice(hp * LANES, (hp + 1) * LANES)

    def geometry(rho):
        r = i * g + rho
        rs = jnp.clip(r - NA_KH // 2, 0, rows - NA_KH)
        off = pl.multiple_of((rs - (i * g - NA_KH)) * GRID_W, GRID_W)
        return r - rs, off, pl.multiple_of(rho * GRID_W, GRID_W)

    def scores(rho, slot):
        delta, off, qoff = geometry(rho)
        for hp in range(npair):
            q2 = q_ref[pl.ds(qoff, GRID_W), lanes_of(hp)]
            zero = jnp.zeros_like(q2)
            qs = jnp.concatenate([jnp.where(first_head, q2, zero),
                                  jnp.where(first_head, zero, q2)], axis=0)
            k2 = kbuf[pl.ds(off, window), lanes_of(hp)]
            s = lax.dot_general(qs, k2, (((1,), (1,)), ((), ())), preferred_element_type=F32)
            s_bufs[slot][hp] = s + bias_ref[delta, hp]

    def softmax(slot):
        for hp in range(npair):
            s = s_bufs[slot][hp]
            p = jnp.exp(s - jnp.max(s, axis=1, keepdims=True))
            p_bufs[slot][hp] = (p * (1.0 / jnp.sum(p, axis=1, keepdims=True))).astype(BF16)

    def values(rho, slot):
        _, off, qoff = geometry(rho)
        for hp in range(npair):
            o = _dot(p_bufs[slot][hp], vbuf[pl.ds(off, window), lanes_of(hp)])
            out2 = jnp.where(first_head, o[:GRID_W], o[GRID_W:])
            o_ref[pl.ds(qoff, GRID_W), lanes_of(hp)] = out2.astype(BF16)

    def block(t, slot, do_scores=True, do_softmax=True, do_values=True):
        if do_scores:
            scores(t + 1, 1 - slot)
        if do_softmax:
            softmax(slot)
        if do_values:
            values(t - 1, 1 - slot)

    assert g % 2 == 0 and g >= 4
    block(-1, 1, do_softmax=False, do_values=False)
    block(0, 0, do_values=False)

    unroll = NA_BLOCKS_PER_ITER
    assert unroll % 2 == 0 and (g - 2) % unroll == 0

    def blocks(j, carry):
        for kk in range(unroll):
            block(unroll * j + 1 + kk, (1 + kk) % 2)
        return carry

    lax.fori_loop(0, (g - 2) // unroll, blocks, 0)
    block(g - 1, 1, do_scores=False)
    block(g, 0, do_scores=False, do_softmax=False)


def na_attention(q, k, v, bias):
    b, n, c = q.shape
    rows = n // GRID_W
    g = NA_ROWS_PER_STEP
    main = g * GRID_W
    halo = NA_KH * GRID_W
    per = main // halo
    last = n // halo - 1
    spec_main = pl.BlockSpec((None, main, c), lambda bi, i: (bi, i, 0))
    spec_prev = pl.BlockSpec((None, halo, c), lambda bi, i: (bi, jnp.maximum(i * per - 1, 0), 0))
    spec_next = pl.BlockSpec((None, halo, c), lambda bi, i: (bi, jnp.minimum((i + 1) * per, last), 0))
    return pl.pallas_call(
        functools.partial(_na_kernel, rows=rows),
        out_shape=jax.ShapeDtypeStruct((b, n, c), BF16),
        grid=(b, rows // g),
        in_specs=[spec_main, spec_prev, spec_main, spec_next, spec_prev, spec_main, spec_next,
                  _const_spec(bias.shape)],
        out_specs=spec_main,
        scratch_shapes=[pltpu.VMEM((main + 2 * halo, c), BF16), pltpu.VMEM((main + 2 * halo, c), BF16)]
        + [pltpu.VMEM(bias.shape[1:], F32)] * 2 + [pltpu.VMEM(bias.shape[1:], BF16)] * 2,
        compiler_params=pltpu.CompilerParams(dimension_semantics=("arbitrary", "arbitrary"),
                                             vmem_limit_bytes=VMEM_LIMIT),
        name="na_attention",
    )(q, k, k, k, v, v, v, bias)


def _mla_kernel(qt_ref, k_ref, vt_ref, o_ref, *bufs, n):
    tq, tk, g = MLA_TQ, MLA_TK, MLA_GROUP
    ng = n // (tk * g)
    total = (n // tq) * ng
    nbuf = MLA_NBUF
    assert total >= 4
    s_bufs, p_bufs = bufs[:nbuf], bufs[nbuf:]

    def split(u):
        return lax.div(u, ng), lax.rem(u, ng)

    def scores_chunk(u, slot, c):
        qi, kg = split(u)
        qt = qt_ref[:, pl.ds(pl.multiple_of(qi * tq, tq), tq)]
        off = pl.multiple_of((kg * g + c) * tk, tk)
        s = _dot(k_ref[pl.ds(off, tk), :], qt)
        s_bufs[slot][c * tk:(c + 1) * tk] = s
        return jnp.max(s, axis=0, keepdims=True)

    def first_group_reset(u, m):
        _, kg = split(u)
        return jnp.where(kg == 0, -jnp.inf, m)

    def softmax_chunk(slot, c, m, mx):
        m_new = jnp.maximum(m, mx)
        p_bufs[slot][c * tk:(c + 1) * tk] = jnp.exp2(s_bufs[slot][c * tk:(c + 1) * tk] - m_new).astype(BF16)
        return m_new, jnp.exp2(m - m_new)

    def values_chunk(u, slot, c, r, acc):
        _, kg = split(u)
        off = pl.multiple_of((kg * g + c) * tk, tk)
        return r * acc + _dot(vt_ref[:, pl.ds(off, tk)], p_bufs[slot][c * tk:(c + 1) * tk])

    def write_out(u, acc):
        qi, _ = split(u)
        out = acc[:MLA_V] * (1.0 / acc[MLA_V:MLA_V + 1])
        o_ref[:, pl.ds(pl.multiple_of(qi * tq, tq), tq)] = out.astype(BF16)

    def block(t, slot, carry, do_scores=True, do_softmax=True, do_values=True):
        m, mxs, rs_prev, acc = carry
        if do_softmax:
            m = first_group_reset(t, m)
        mxs_next, rs = list(mxs), list(rs_prev)
        if do_scores:
            for c in range(g):
                mxs_next[c] = scores_chunk(t + 1, (slot + 1) % nbuf, c)
        if do_softmax:
            for c in range(g):
                m, rs[c] = softmax_chunk(slot, c, m, mxs[c])
        if do_values:
            for c in range(g):
                acc = values_chunk(t - 1, (slot - 1) % nbuf, c, rs_prev[c], acc)
            write_out(t - 1, acc)
        return m, tuple(mxs_next), tuple(rs), acc

    row = jnp.zeros((1, tq), F32)
    carry = (row, (row,) * g, (row,) * g, jnp.zeros((MLA_VA, tq), F32))
    carry = block(-1, -1 % nbuf, carry, do_softmax=False, do_values=False)
    carry = block(0, 0, carry, do_values=False)

    unroll = MLA_BLOCKS_PER_ITER
    assert unroll % nbuf == 0
    n_iter, n_rest = divmod(total - 2, unroll)

    def blocks(i, carry):
        t0 = unroll * i + 1
        for kk in range(unroll):
            carry = block(t0 + kk, (1 + kk) % nbuf, carry)
        return carry

    carry = lax.fori_loop(0, n_iter, blocks, carry)
    for t in range(n_iter * unroll + 1, n_iter * unroll + 1 + n_rest):
        carry = block(t, t % nbuf, carry)
    carry = block(total - 1, (total - 1) % nbuf, carry, do_scores=False)
    block(total, total % nbuf, carry, do_scores=False, do_softmax=False)


def _mla_fast_kernel(qt_ref, k_ref, vt_ref, m_ref, o_ref, p0, p1, *, n):
    tq, tk, g = MLA_TQ, MLA_TK, MLA_GROUP
    ng = n // (tk * g)
    total = (n // tq) * ng
    p_bufs = (p0, p1)

    def split(u):
        return lax.div(u, ng), lax.rem(u, ng)

    def probs(u, slot):
        qi, kg = split(u)
        qoff = pl.multiple_of(qi * tq, tq)
        qt = qt_ref[:, pl.ds(qoff, tq)]
        m = m_ref[:, pl.ds(qoff, tq)]
        for c in range(g):
            off = pl.multiple_of((kg * g + c) * tk, tk)
            s = _dot(k_ref[pl.ds(off, tk), :], qt)
            p_bufs[slot][c * tk:(c + 1) * tk] = jnp.exp2(s - m).astype(BF16)

    def values(u, slot, acc):
        qi, kg = split(u)
        acc = jnp.where(kg == 0, 0.0, acc)
        for c in range(g):
            off = pl.multiple_of((kg * g + c) * tk, tk)
            acc = acc + _dot(vt_ref[:, pl.ds(off, tk)], p_bufs[slot][c * tk:(c + 1) * tk])
        out = acc[:MLA_V] * (1.0 / acc[MLA_V:MLA_V + 1])
        o_ref[:, pl.ds(pl.multiple_of(qi * tq, tq), tq)] = out.astype(BF16)
        return acc

    def block(t, slot, acc):
        probs(t + 1, 1 - slot)
        return values(t, slot, acc)

    unroll = MLA_BLOCKS_PER_ITER
    assert unroll % 2 == 0 and total >= 2
    probs(0, 0)
    acc = block(0, 0, jnp.zeros((MLA_VA, tq), F32))
    n_iter, n_rest = divmod(total - 2, unroll)

    def blocks(i, acc):
        t0 = unroll * i + 1
        for kk in range(unroll):
            acc = block(t0 + kk, (1 + kk) % 2, acc)
        return acc

    acc = lax.fori_loop(0, n_iter, blocks, acc)
    for t in range(n_iter * unroll + 1, total - 1):
        acc = block(t, t % 2, acc)
    values(total - 1, (total - 1) % 2, acc)


def mla_attention(qt, k, vt, qn2, lb, kn2):
    b, h, _, n = qt.shape
    rows = MLA_GROUP * MLA_TK
    head = lambda r, c: pl.BlockSpec((None, None, r, c), lambda bi, hi: (bi, hi, 0, 0))
    params = pltpu.CompilerParams(dimension_semantics=("arbitrary",) * 2, vmem_limit_bytes=VMEM_LIMIT)
    out_shape = jax.ShapeDtypeStruct((b, h, MLA_V, n), BF16)

    k2max = jnp.max(kn2, axis=(1, 3, 4))
    m = jnp.sqrt(qn2 * k2max[:, :, None, None]) * MLA_BOUND_INFLATE
    gap = m - lb

    def fast(qt, k, vt, m):
        return pl.pallas_call(
            functools.partial(_mla_fast_kernel, n=n),
            out_shape=out_shape,
            grid=(b, h),
            in_specs=[head(QK_PAD, n), head(n, QK_PAD), head(MLA_VA, n), head(1, n)],
            out_specs=head(MLA_V, n),
            scratch_shapes=[pltpu.VMEM((rows, MLA_TQ), BF16)] * 2,
            compiler_params=params,
            name="mla_attention_fast",
        )(qt, k, vt, m)

    def exact(qt, k, vt, m):
        return pl.pallas_call(
            functools.partial(_mla_kernel, n=n),
            out_shape=out_shape,
            grid=(b, h),
            in_specs=[head(QK_PAD, n), head(n, QK_PAD), head(MLA_VA, n)],
            out_specs=head(MLA_V, n),
            scratch_shapes=[pltpu.VMEM((rows, MLA_TQ), F32)] * MLA_NBUF + [pltpu.VMEM((rows, MLA_TQ), BF16)] * MLA_NBUF,
            compiler_params=params,
            name="mla_attention",
        )(qt, k, vt)

    return lax.cond(jnp.max(gap) < MLA_MAX_GAP, fast, exact, qt, k, vt, m)


def _mixer_out_kernel(x_ref, na_ref, at_ref, ga_ref, gb_ref, wna_ref, wmla_ref, wout_ref, g_ref, b_ref,
                      o_ref, mix_ref):
    na = na_ref[...]
    at = at_ref[...].T
    d = o_ref.shape[1]
    for c in range(d // MIX_CHUNK):
        cs = slice(c * MIX_CHUNK, (c + 1) * MIX_CHUNK)
        ya = _dot(na, wna_ref[:, cs])
        yb = _dot(at, wmla_ref[:, cs])
        mix_ref[:, cs] = (ga_ref[:, cs].astype(F32) * ya + gb_ref[:, cs].astype(F32) * yb).astype(BF16)
    y = _dot(mix_ref[...], wout_ref[...])
    o_ref[...] = _layer_norm(ALPHA * x_ref[...] + y, g_ref[...], b_ref[...])


def mixer_out(x, na, at, ga, gb, w_na_o, w_mla_o, w_out, g, bb):
    b, n, d = x.shape
    tm = TOKEN_TILE
    tok = lambda c: pl.BlockSpec((None, tm, c), lambda bi, i: (bi, i, 0))
    return pl.pallas_call(
        _mixer_out_kernel,
        out_shape=jax.ShapeDtypeStruct((b, n, d), F32),
        grid=(b, n // tm),
        in_specs=[tok(d), tok(NA_WIDTH),
                  pl.BlockSpec((None, at.shape[1], tm), lambda bi, i: (bi, 0, i)),
                  tok(d), tok(d),
                  _const_spec(w_na_o.shape), _const_spec(w_mla_o.shape), _const_spec(w_out.shape),
                  _const_spec(g.shape), _const_spec(bb.shape)],
        out_specs=tok(d),
        scratch_shapes=[pltpu.VMEM((tm, d), BF16)],
        compiler_params=pltpu.CompilerParams(dimension_semantics=("arbitrary", "arbitrary"),
                                             vmem_limit_bytes=VMEM_LIMIT),
        name="mixer_out",
    )(x, na, at, ga, gb, w_na_o, w_mla_o, w_out, g, bb)


def _prep_mixer_weights(w_in, b_gate, q_norm_g, kv_norm_g, w_uq, w_ukv):
    d = w_in.shape[0]
    half = MLA_ROPE // 2
    swap = lambda w: jnp.concatenate([-w[..., half:], w[..., :half]], axis=-1)
    pad_rope = lambda w: jnp.pad(w, ((0, 0), (MLA_NOPE, QK_PAD - MLA_QK)))
    c_kr = 3 * NA_WIDTH + MLA_Q_RANK + MLA_KV_RANK
    w_kr = w_in[:, c_kr:c_kr + MLA_ROPE]
    w_packed = jnp.concatenate([w_in[:, :c_kr], pad_rope(w_kr), pad_rope(swap(w_kr)),
                                w_in[:, c_kr + MLA_ROPE:]], axis=1)
    ukv = w_ukv.reshape(MLA_KV_RANK, MLA_HEADS, MLA_NOPE + MLA_V)
    wk = jnp.pad(ukv[..., :MLA_NOPE], ((0, 0), (0, 0), (0, QK_PAD - MLA_NOPE)))
    wk = wk.reshape(MLA_KV_RANK, MLA_HEADS * QK_PAD)
    wvt = ukv[..., MLA_NOPE:].reshape(MLA_KV_RANK, MLA_HEADS * MLA_V).T
    uq = w_uq.reshape(MLA_Q_RANK, MLA_HEADS, MLA_QK)
    wqt = jnp.pad(uq, ((0, 0), (0, 0), (0, QK_PAD - MLA_QK))).reshape(MLA_Q_RANK, MLA_HEADS * QK_PAD).T
    wqs = swap(uq[..., MLA_NOPE:]).reshape(MLA_Q_RANK, MLA_HEADS * MLA_ROPE).T
    return {
        "w_in": w_packed.astype(BF16), "b_gate": b_gate.reshape(1, 2 * d),
        "q_norm_g": q_norm_g.reshape(1, -1), "kv_norm_g": kv_norm_g.reshape(1, -1),
        "wk": wk.astype(BF16), "wvt": wvt.astype(BF16), "wqt": wqt.astype(BF16), "wqs": wqs.astype(BF16),
    }


def _rope_tables(n):
    inv = 1.0 / (ROPE_BASE ** (jnp.arange(0, MLA_ROPE, 2, dtype=F32) / MLA_ROPE))
    ang = jnp.arange(n, dtype=F32)[:, None] * inv[None, :]
    cos2 = jnp.tile(jnp.cos(ang), (1, 2))
    sin2 = jnp.tile(jnp.sin(ang), (1, 2))
    pad = ((0, 0), (MLA_NOPE, QK_PAD - MLA_QK))
    return {"ck": jnp.pad(cos2, pad), "sk": jnp.pad(sin2, pad), "cq": cos2.T, "sq": sin2.T}


def _na_bias_table(rpb):
    qc = np.arange(GRID_W)[:, None]
    kc = np.arange(GRID_W)[None, :]
    dc = np.clip(kc - qc + NA_KW - 1, 0, 2 * NA_KW - 2)
    onehot = (dc[None] == np.arange(2 * NA_KW - 1)[:, None, None]).astype(np.float32)
    win = np.clip(qc - NA_KW // 2, 0, GRID_W - NA_KW)
    in_win = (kc >= win) & (kc < win + NA_KW)
    t = jnp.einsum("hrc,cqk->hrqk", rpb, jnp.asarray(onehot), precision=lax.Precision.HIGHEST)
    t = jnp.where(in_win, t, NEG_BIG)
    bias = jnp.stack([t[:, NA_KH - 1 - dl:2 * NA_KH - 1 - dl] for dl in range(NA_KH)])
    bias = bias.transpose(0, 1, 3, 2, 4)
    return bias.reshape(NA_KH, NA_HEADS // 2, 2 * GRID_W, NA_KH * GRID_W)


def _encoder_layer(x, p, tabs):
    b, n, d = x.shape
    x1 = ffn_ln(x.reshape(b * n, d), p["ffn1_w_in"], p["ffn1_w_out"], p["ln1_g"], p["ln1_b"])
    x1 = x1.reshape(b, n, d)
    naq, nak, nav, qt, k, vt, ga, gb, qn2, lb, kn2 = mixer_in(x1, p["mixer"], tabs)
    na = na_attention(naq, nak, nav, p["na_bias"])
    at = mla_attention(qt, k, vt, qn2, lb, kn2).reshape(b, MLA_HEADS * MLA_V, n)
    x2 = mixer_out(x1, na, at, ga, gb, p["w_na_o"], p["w_mla_o"], p["w_out"], p["ln2_g"], p["ln2_b"])
    y = ffn_ln(x2.reshape(b * n, d), p["ffn2_w_in"], p["ffn2_w_out"], p["ln3_g"], p["ln3_b"])
    return y.reshape(b, n, d)


def kernel(x_prompt, x_sample, ffn1_w_in, ffn1_w_out, ln1_g, ln1_b, w_in, b_gate, na_rpb, q_norm_g, kv_norm_g, w_uq, w_ukv, w_na_o, w_mla_o, w_out, ln2_g, ln2_b, ffn2_w_in, ffn2_w_out, ln3_g, ln3_b):
    l = 0
    row = lambda a: a[l].reshape(1, -1)
    p = {
        "ffn1_w_in": ffn1_w_in[l].astype(BF16), "ffn1_w_out": ffn1_w_out[l].astype(BF16),
        "ln1_g": row(ln1_g), "ln1_b": row(ln1_b),
        "mixer": _prep_mixer_weights(w_in[l], b_gate[l], q_norm_g[l], kv_norm_g[l], w_uq[l], w_ukv[l]),
        "na_bias": _na_bias_table(na_rpb[l]),
        "w_na_o": w_na_o[l].astype(BF16), "w_mla_o": w_mla_o[l].astype(BF16), "w_out": w_out[l].astype(BF16),
        "ln2_g": row(ln2_g), "ln2_b": row(ln2_b),
        "ffn2_w_in": ffn2_w_in[l].astype(BF16), "ffn2_w_out": ffn2_w_out[l].astype(BF16),
        "ln3_g": row(ln3_g), "ln3_b": row(ln3_b),
    }
    outs = []
    for x in (x_prompt, x_sample):
        outs.append(_encoder_layer(x, p, _rope_tables(x.shape[1])))
    return tuple(outs)
```

```python
import functools
import math

import numpy as np
import jax
import jax.numpy as jnp
from jax import lax
from jax.experimental import pallas as pl
from jax.experimental.pallas import tpu as pltpu

F32 = jnp.float32
BF16 = jnp.bfloat16

DEPTH = 1
GRID_W = 64
NA_HEADS = 8
NA_HEAD_DIM = 64
NA_WIDTH = NA_HEADS * NA_HEAD_DIM
NA_KH = 8
NA_KW = 16
MLA_HEADS = 8
MLA_NOPE = 64
MLA_ROPE = 32
MLA_QK = MLA_NOPE + MLA_ROPE
MLA_V = 64
MLA_VA = MLA_V + 16
MLA_Q_RANK = 384
MLA_KV_RANK = 256
ROPE_BASE = 10000.0
LN_EPS = 1e-5
RMS_EPS = 1e-6
ALPHA = (2.0 * DEPTH) ** 0.25
LOG2E = math.log2(math.e)

LANES = 128
QK_PAD = 128
VMEM_LIMIT = 56 * 1024 * 1024

TOKEN_TILE = 512
FFN_TILE = 1024
FFN_CHUNK = 256
MIX_CHUNK = 256
NA_ROWS_PER_STEP = 32
NA_BLOCKS_PER_ITER = 6
MLA_TQ = 512
MLA_TK = 256
MLA_GROUP = 4
MLA_NBUF = 2
MLA_BLOCKS_PER_ITER = 6
MLA_PROBE_KEYS = 16
MLA_BOUND_INFLATE = 1.02
MLA_MAX_GAP = 64.0
NEG_BIG = -1e30


def _const_spec(shape):
    nd = len(shape)
    return pl.BlockSpec(shape, lambda *_: (0,) * nd, pipeline_mode=pl.Buffered(1))


def _layer_norm(y, g, b):
    mu = jnp.mean(y, axis=-1, keepdims=True)
    d = y - mu
    var = jnp.mean(d * d, axis=-1, keepdims=True)
    return d * lax.rsqrt(var + LN_EPS) * g + b


def _rms_norm(y, g):
    return y * lax.rsqrt(jnp.mean(y * y, axis=-1, keepdims=True) + RMS_EPS) * g


def _dot(a, b):
    return jnp.dot(a, b, preferred_element_type=F32)


def _swiglu_ln(x, w_in_ref, w_out_ref, g_ref, b_ref, h_ref, d_ff):
    xb = x.astype(BF16)
    for c in range(d_ff // FFN_CHUNK):
        lo, hi = c * FFN_CHUNK, (c + 1) * FFN_CHUNK
        a = _dot(xb, w_in_ref[:, lo:hi])
        u = _dot(xb, w_in_ref[:, d_ff + lo:d_ff + hi])
        h_ref[:, lo:hi] = (a * jax.nn.sigmoid(a) * u).astype(BF16)
    y = _dot(h_ref[...], w_out_ref[...])
    return _layer_norm(ALPHA * x + 0.5 * y, g_ref[...], b_ref[...])


def _ffn_ln_kernel(x_ref, w_in_ref, w_out_ref, g_ref, b_ref, o_ref, h_ref, *, d_ff):
    o_ref[...] = _swiglu_ln(x_ref[...], w_in_ref, w_out_ref, g_ref, b_ref, h_ref, d_ff)


def ffn_ln(x, w_in, w_out, g, b):
    t, d = x.shape
    d_ff = w_out.shape[0]
    tm = FFN_TILE
    return pl.pallas_call(
        functools.partial(_ffn_ln_kernel, d_ff=d_ff),
        out_shape=jax.ShapeDtypeStruct((t, d), F32),
        grid=(t // tm,),
        in_specs=[pl.BlockSpec((tm, d), lambda i: (i, 0)),
                  _const_spec(w_in.shape), _const_spec(w_out.shape),
                  _const_spec(g.shape), _const_spec(b.shape)],
        out_specs=pl.BlockSpec((tm, d), lambda i: (i, 0)),
        scratch_shapes=[pltpu.VMEM((tm, d_ff), BF16)],
        compiler_params=pltpu.CompilerParams(dimension_semantics=("arbitrary",),
                                             vmem_limit_bytes=VMEM_LIMIT),
        name="ffn_ln",
    )(x, w_in, w_out, g, b)


_C_NAQ, _C_NAK, _C_NAV = 0, NA_WIDTH, 2 * NA_WIDTH
_C_CQ = 3 * NA_WIDTH
_C_CKV = _C_CQ + MLA_Q_RANK
_C_KR = _C_CKV + MLA_KV_RANK
_C_GATE = _C_KR + 2 * QK_PAD


def _mixer_in_kernel(x_ref, w_ref, bg_ref, qg_ref, kvg_ref, wk_ref, wvt_ref, wqt_ref, wqs_ref,
                     ck_ref, sk_ref, cq_ref, sq_ref,
                     naq_ref, nak_ref, nav_ref, qt_ref, k_ref, vt_ref, ga_ref, gb_ref,
                     qn2_ref, lb_ref, kn2_ref, *, d_model):
    xb = x_ref[...].astype(BF16)
    naq_ref[...] = (_dot(xb, w_ref[:, _C_NAQ:_C_NAK]) * (NA_HEAD_DIM ** -0.5)).astype(BF16)
    nak_ref[...] = _dot(xb, w_ref[:, _C_NAK:_C_NAV]).astype(BF16)
    nav_ref[...] = _dot(xb, w_ref[:, _C_NAV:_C_CQ]).astype(BF16)

    cqn = _rms_norm(_dot(xb, w_ref[:, _C_CQ:_C_CKV]), qg_ref[...])
    ckvn = _rms_norm(_dot(xb, w_ref[:, _C_CKV:_C_KR]), kvg_ref[...])

    kr = _dot(xb, w_ref[:, _C_KR:_C_GATE])
    kr_blk = kr[:, :QK_PAD] * ck_ref[...] + kr[:, QK_PAD:] * sk_ref[...]
    kall = _dot(ckvn.astype(BF16), wk_ref[...])
    k_probe = []
    for h in range(MLA_HEADS):
        kf = kall[:, h * QK_PAD:(h + 1) * QK_PAD] + kr_blk
        kb = kf.astype(BF16)
        k_ref[h] = kb
        k2 = jnp.max(jnp.sum(kf * kf, axis=1, keepdims=True), axis=0, keepdims=True)
        kn2_ref[h] = jnp.broadcast_to(k2, (1, LANES))
        k_probe.append(kb[0:MLA_PROBE_KEYS])

    ckvn_t = ckvn.T.astype(BF16)
    vt = _dot(wvt_ref[...], ckvn_t)
    ones = jnp.ones((MLA_VA - MLA_V, vt.shape[1]), BF16)
    for h in range(MLA_HEADS):
        vt_ref[h, 0:MLA_V] = vt[h * MLA_V:(h + 1) * MLA_V].astype(BF16)
        vt_ref[h, MLA_V:MLA_VA] = ones

    cqn_t = cqn.T.astype(BF16)
    q_scale = (MLA_QK ** -0.5) * LOG2E
    qt = _dot(wqt_ref[...], cqn_t) * q_scale
    qs = _dot(wqs_ref[...], cqn_t) * q_scale
    cq, sq = cq_ref[...], sq_ref[...]
    zeros = jnp.zeros((QK_PAD - MLA_QK, qt.shape[1]), BF16)
    for h in range(MLA_HEADS):
        base = h * QK_PAD
        nope = qt[base:base + MLA_NOPE]
        rope = qt[base + MLA_NOPE:base + MLA_QK] * cq + qs[h * MLA_ROPE:(h + 1) * MLA_ROPE] * sq
        qh = jnp.concatenate([nope.astype(BF16), rope.astype(BF16), zeros], axis=0)
        qt_ref[h] = qh
        qn2_ref[h] = (jnp.sum(nope * nope, axis=0, keepdims=True)
                      + jnp.sum(rope * rope, axis=0, keepdims=True))
        lb_ref[h] = jnp.max(_dot(k_probe[h], qh), axis=0, keepdims=True)

    for half, out_ref in enumerate((ga_ref, gb_ref)):
        lo = _C_GATE + half * d_model
        logits = _dot(xb, w_ref[:, lo:lo + d_model]) + bg_ref[:, half * d_model:(half + 1) * d_model]
        out_ref[...] = jax.nn.sigmoid(logits).astype(BF16)


def mixer_in(x, wts, tabs):
    b, n, d = x.shape
    tm = TOKEN_TILE
    h = MLA_HEADS
    tok = lambda c: pl.BlockSpec((None, tm, c), lambda bi, i: (bi, i, 0))
    out_shape = (
        jax.ShapeDtypeStruct((b, n, NA_WIDTH), BF16), jax.ShapeDtypeStruct((b, n, NA_WIDTH), BF16),
        jax.ShapeDtypeStruct((b, n, NA_WIDTH), BF16),
        jax.ShapeDtypeStruct((b, h, QK_PAD, n), BF16),
        jax.ShapeDtypeStruct((b, h, n, QK_PAD), BF16),
        jax.ShapeDtypeStruct((b, h, MLA_VA, n), BF16),
        jax.ShapeDtypeStruct((b, n, d), BF16), jax.ShapeDtypeStruct((b, n, d), BF16),
        jax.ShapeDtypeStruct((b, h, 1, n), F32), jax.ShapeDtypeStruct((b, h, 1, n), F32),
        jax.ShapeDtypeStruct((b, n // tm, h, 1, LANES), F32),
    )
    stat = pl.BlockSpec((None, h, 1, tm), lambda bi, i: (bi, 0, 0, i))
    out_specs = (
        tok(NA_WIDTH), tok(NA_WIDTH), tok(NA_WIDTH),
        pl.BlockSpec((None, h, QK_PAD, tm), lambda bi, i: (bi, 0, 0, i)),
        pl.BlockSpec((None, h, tm, QK_PAD), lambda bi, i: (bi, 0, i, 0)),
        pl.BlockSpec((None, h, MLA_VA, tm), lambda bi, i: (bi, 0, 0, i)),
        tok(d), tok(d),
        stat, stat,
        pl.BlockSpec((None, None, h, 1, LANES), lambda bi, i: (bi, i, 0, 0, 0)),
    )
    consts = [wts["w_in"], wts["b_gate"], wts["q_norm_g"], wts["kv_norm_g"],
              wts["wk"], wts["wvt"], wts["wqt"], wts["wqs"]]
    in_specs = ([tok(d)] + [_const_spec(c.shape) for c in consts] + [
        pl.BlockSpec((tm, QK_PAD), lambda bi, i: (i, 0)),
        pl.BlockSpec((tm, QK_PAD), lambda bi, i: (i, 0)),
        pl.BlockSpec((MLA_ROPE, tm), lambda bi, i: (0, i)),
        pl.BlockSpec((MLA_ROPE, tm), lambda bi, i: (0, i)),
    ])
    return pl.pallas_call(
        functools.partial(_mixer_in_kernel, d_model=d),
        out_shape=out_shape,
        grid=(b, n // tm),
        in_specs=in_specs,
        out_specs=out_specs,
        compiler_params=pltpu.CompilerParams(dimension_semantics=("arbitrary", "arbitrary"),
                                             vmem_limit_bytes=VMEM_LIMIT),
        name="mixer_in",
    )(x, *consts, tabs["ck"], tabs["sk"], tabs["cq"], tabs["sq"])


def _na_kernel(q_ref, kp_ref, km_ref, kn_ref, vp_ref, vm_ref, vn_ref, bias_ref, o_ref, kbuf, vbuf,
               s0, s1, p0, p1, r0, r1, *, rows):
    g = NA_ROWS_PER_STEP
    halo = NA_KH * GRID_W
    main = g * GRID_W
    window = NA_KH * GRID_W
    npair = NA_HEADS // 2
    i = pl.program_id(1)
    kbuf[0:halo] = kp_ref[...]
    kbuf[halo:halo + main] = km_ref[...]
    kbuf[halo + main:] = kn_ref[...]
    vbuf[0:halo] = vp_ref[...]
    vbuf[halo:halo + main] = vm_ref[...]
    vbuf[halo + main:] = vn_ref[...]
    lane = lax.broadcasted_iota(jnp.int32, (GRID_W, LANES), 1)
    first_head = lane < NA_HEAD_DIM
    s_bufs, p_bufs, r_bufs = (s0, s1), (p0, p1), (r0, r1)
    lanes_of = lambda hp: slice(hp * LANES, (hp + 1) * LANES)

    def geometry(rho):
        r = i * g + rho
        rs = jnp.clip(r - NA_KH // 2, 0, rows - NA_KH)
        off = pl.multiple_of((rs - (i * g - NA_KH)) * GRID_W, GRID_W)
        return r - rs, off, pl.multiple_of(rho * GRID_W, GRID_W)

    def scores(rho, slot):
        delta, off, qoff = geometry(rho)
        for hp in range(npair):
            q2 = q_ref[pl.ds(qoff, GRID_W), lanes_of(hp)]
            zero = jnp.zeros_like(q2)
            qs = jnp.concatenate([jnp.where(first_head, q2, zero),
                                  jnp.where(first_head, zero, q2)], axis=0)
            k2 = kbuf[pl.ds(off, window), lanes_of(hp)]
            s = lax.dot_general(qs, k2, (((1,), (1,)), ((), ())), preferred_element_type=F32)
            s_bufs[slot][hp] = s + bias_ref[delta, hp]

    def softmax(slot):
        for hp in range(npair):
            for rows_ in (slice(0, GRID_W), slice(GRID_W, 2 * GRID_W)):
                s = s_bufs[slot][hp, rows_]
                p = jnp.exp(s - jnp.max(s, axis=1, keepdims=True))
                p_bufs[slot][hp, rows_] = p.astype(BF16)
                r_bufs[slot][hp, rows_] = jnp.broadcast_to(1.0 / jnp.sum(p, axis=1, keepdims=True),
                                                           (GRID_W, LANES))

    def values(rho, slot):
        _, off, qoff = geometry(rho)
        for hp in range(npair):
            o = _dot(p_bufs[slot][hp], vbuf[pl.ds(off, window), lanes_of(hp)])
            o = o * r_bufs[slot][hp]
            out2 = jnp.where(first_head, o[:GRID_W], o[GRID_W:])
            o_ref[pl.ds(qoff, GRID_W), lanes_of(hp)] = out2.astype(BF16)

    def block(t, slot, do_scores=True, do_softmax=True, do_values=True):
        if do_scores:
            scores(t + 1, 1 - slot)
        if do_softmax:
            softmax(slot)
        if do_values:
            values(t - 1, 1 - slot)

    assert g % 2 == 0 and g >= 4
    block(-1, 1, do_softmax=False, do_values=False)
    block(0, 0, do_values=False)

    unroll = NA_BLOCKS_PER_ITER
    assert unroll % 2 == 0 and (g - 2) % unroll == 0

    def blocks(j, carry):
        for kk in range(unroll):
            block(unroll * j + 1 + kk, (1 + kk) % 2)
        return carry

    lax.fori_loop(0, (g - 2) // unroll, blocks, 0)
    block(g - 1, 1, do_scores=False)
    block(g, 0, do_scores=False, do_softmax=False)


def na_attention(q, k, v, bias):
    b, n, c = q.shape
    rows = n // GRID_W
    g = NA_ROWS_PER_STEP
    main = g * GRID_W
    halo = NA_KH * GRID_W
    per = main // halo
    last = n // halo - 1
    spec_main = pl.BlockSpec((None, main, c), lambda bi, i: (bi, i, 0))
    spec_prev = pl.BlockSpec((None, halo, c), lambda bi, i: (bi, jnp.maximum(i * per - 1, 0), 0))
    spec_next = pl.BlockSpec((None, halo, c), lambda bi, i: (bi, jnp.minimum((i + 1) * per, last), 0))
    return pl.pallas_call(
        functools.partial(_na_kernel, rows=rows),
        out_shape=jax.ShapeDtypeStruct((b, n, c), BF16),
        grid=(b, rows // g),
        in_specs=[spec_main, spec_prev, spec_main, spec_next, spec_prev, spec_main, spec_next,
                  _const_spec(bias.shape)],
        out_specs=spec_main,
        scratch_shapes=[pltpu.VMEM((main + 2 * halo, c), BF16), pltpu.VMEM((main + 2 * halo, c), BF16)]
        + [pltpu.VMEM(bias.shape[1:], F32)] * 2 + [pltpu.VMEM(bias.shape[1:], BF16)] * 2
        + [pltpu.VMEM(bias.shape[1:3] + (LANES,), F32)] * 2,
        compiler_params=pltpu.CompilerParams(dimension_semantics=("arbitrary", "arbitrary"),
                                             vmem_limit_bytes=VMEM_LIMIT),
        name="na_attention",
    )(q, k, k, k, v, v, v, bias)


def _mla_kernel(qt_ref, k_ref, vt_ref, o_ref, *bufs, n):
    tq, tk, g = MLA_TQ, MLA_TK, MLA_GROUP
    ng = n // (tk * g)
    total = (n // tq) * ng
    nbuf = MLA_NBUF
    assert total >= 4
    s_bufs, p_bufs = bufs[:nbuf], bufs[nbuf:]

    def split(u):
        return lax.div(u, ng), lax.rem(u, ng)

    def scores_chunk(u, slot, c):
        qi, kg = split(u)
        qt = qt_ref[:, pl.ds(pl.multiple_of(qi * tq, tq), tq)]
        off = pl.multiple_of((kg * g + c) * tk, tk)
        s = _dot(k_ref[pl.ds(off, tk), :], qt)
        s_bufs[slot][c * tk:(c + 1) * tk] = s
        return jnp.max(s, axis=0, keepdims=True)

    def first_group_reset(u, m):
        _, kg = split(u)
        return jnp.where(kg == 0, -jnp.inf, m)

    def softmax_chunk(slot, c, m, mx):
        m_new = jnp.maximum(m, mx)
        p_bufs[slot][c * tk:(c + 1) * tk] = jnp.exp2(s_bufs[slot][c * tk:(c + 1) * tk] - m_new).astype(BF16)
        return m_new, jnp.exp2(m - m_new)

    def values_chunk(u, slot, c, r, acc):
        _, kg = split(u)
        off = pl.multiple_of((kg * g + c) * tk, tk)
        return r * acc + _dot(vt_ref[:, pl.ds(off, tk)], p_bufs[slot][c * tk:(c + 1) * tk])

    def write_out(u, acc):
        qi, _ = split(u)
        out = acc[:MLA_V] * (1.0 / acc[MLA_V:MLA_V + 1])
        o_ref[:, pl.ds(pl.multiple_of(qi * tq, tq), tq)] = out.astype(BF16)

    def block(t, slot, carry, do_scores=True, do_softmax=True, do_values=True):
        m, mxs, rs_prev, acc = carry
        if do_softmax:
            m = first_group_reset(t, m)
        mxs_next, rs = list(mxs), list(rs_prev)
        if do_scores:
            for c in range(g):
                mxs_next[c] = scores_chunk(t + 1, (slot + 1) % nbuf, c)
        if do_softmax:
            for c in range(g):
                m, rs[c] = softmax_chunk(slot, c, m, mxs[c])
        if do_values:
            for c in range(g):
                acc = values_chunk(t - 1, (slot - 1) % nbuf, c, rs_prev[c], acc)
            write_out(t - 1, acc)
        return m, tuple(mxs_next), tuple(rs), acc

    row = jnp.zeros((1, tq), F32)
    carry = (row, (row,) * g, (row,) * g, jnp.zeros((MLA_VA, tq), F32))
    carry = block(-1, -1 % nbuf, carry, do_softmax=False, do_values=False)
    carry = block(0, 0, carry, do_values=False)

    unroll = MLA_BLOCKS_PER_ITER
    assert unroll % nbuf == 0
    n_iter, n_rest = divmod(total - 2, unroll)

    def blocks(i, carry):
        t0 = unroll * i + 1
        for kk in range(unroll):
            carry = block(t0 + kk, (1 + kk) % nbuf, carry)
        return carry

    carry = lax.fori_loop(0, n_iter, blocks, carry)
    for t in range(n_iter * unroll + 1, n_iter * unroll + 1 + n_rest):
        carry = block(t, t % nbuf, carry)
    carry = block(total - 1, (total - 1) % nbuf, carry, do_scores=False)
    block(total, total % nbuf, carry, do_scores=False, do_softmax=False)


def _mla_fast_kernel(qt_ref, k_ref, vt_ref, m_ref, o_ref, p0, p1, *, n):
    tq, tk, g = MLA_TQ, MLA_TK, MLA_GROUP
    ng = n // (tk * g)
    total = (n // tq) * ng
    p_bufs = (p0, p1)

    def split(u):
        return lax.div(u, ng), lax.rem(u, ng)

    def probs(u, slot, l):
        qi, kg = split(u)
        qoff = pl.multiple_of(qi * tq, tq)
        qt = qt_ref[:, pl.ds(qoff, tq)]
        m = m_ref[:, pl.ds(qoff, tq)]
        l = jnp.where(kg == 0, 0.0, l)
        for c in range(g):
            off = pl.multiple_of((kg * g + c) * tk, tk)
            s = _dot(k_ref[pl.ds(off, tk), :], qt)
            p = jnp.exp2(s - m)
            p_bufs[slot][c * tk:(c + 1) * tk] = p.astype(BF16)
            l = l + jnp.sum(p, axis=0, keepdims=True)
        return l

    def values(u, slot, acc, l):
        qi, kg = split(u)
        acc = jnp.where(kg == 0, 0.0, acc)
        for c in range(g):
            off = pl.multiple_of((kg * g + c) * tk, tk)
            acc = acc + _dot(vt_ref[0:MLA_V, pl.ds(off, tk)], p_bufs[slot][c * tk:(c + 1) * tk])
        o_ref[:, pl.ds(pl.multiple_of(qi * tq, tq), tq)] = (acc * (1.0 / l)).astype(BF16)
        return acc

    def block(t, slot, carry):
        acc, l = carry
        l_next = probs(t + 1, 1 - slot, l)
        return values(t, slot, acc, l), l_next

    unroll = MLA_BLOCKS_PER_ITER
    assert unroll % 2 == 0 and total >= 2
    carry = (jnp.zeros((MLA_V, tq), F32), probs(0, 0, jnp.zeros((1, tq), F32)))
    carry = block(0, 0, carry)
    n_iter, n_rest = divmod(total - 2, unroll)

    def blocks(i, carry):
        t0 = unroll * i + 1
        for kk in range(unroll):
            carry = block(t0 + kk, (1 + kk) % 2, carry)
        return carry

    carry = lax.fori_loop(0, n_iter, blocks, carry)
    for t in range(n_iter * unroll + 1, total - 1):
        carry = block(t, t % 2, carry)
    values(total - 1, (total - 1) % 2, *carry)


def mla_attention(qt, k, vt, qn2, lb, kn2):
    b, h, _, n = qt.shape
    rows = MLA_GROUP * MLA_TK
    head = lambda r, c: pl.BlockSpec((None, None, r, c), lambda bi, hi: (bi, hi, 0, 0))
    params = pltpu.CompilerParams(dimension_semantics=("arbitrary",) * 2, vmem_limit_bytes=VMEM_LIMIT)
    out_shape = jax.ShapeDtypeStruct((b, h, MLA_V, n), BF16)

    k2max = jnp.max(kn2, axis=(1, 3, 4))
    m = jnp.sqrt(qn2 * k2max[:, :, None, None]) * MLA_BOUND_INFLATE
    gap = m - lb

    def fast(qt, k, vt, m):
        return pl.pallas_call(
            functools.partial(_mla_fast_kernel, n=n),
            out_shape=out_shape,
            grid=(b, h),
            in_specs=[head(QK_PAD, n), head(n, QK_PAD), head(MLA_VA, n), head(1, n)],
            out_specs=head(MLA_V, n),
            scratch_shapes=[pltpu.VMEM((rows, MLA_TQ), BF16)] * 2,
            compiler_params=params,
            name="mla_attention_fast",
        )(qt, k, vt, m)

    def exact(qt, k, vt, m):
        return pl.pallas_call(
            functools.partial(_mla_kernel, n=n),
            out_shape=out_shape,
            grid=(b, h),
            in_specs=[head(QK_PAD, n), head(n, QK_PAD), head(MLA_VA, n)],
            out_specs=head(MLA_V, n),
            scratch_shapes=[pltpu.VMEM((rows, MLA_TQ), F32)] * MLA_NBUF + [pltpu.VMEM((rows, MLA_TQ), BF16)] * MLA_NBUF,
            compiler_params=params,
            name="mla_attention",
        )(qt, k, vt)

    return lax.cond(jnp.max(gap) < MLA_MAX_GAP, fast, exact, qt, k, vt, m)


def _mixer_out_kernel(x_ref, na_ref, at_ref, ga_ref, gb_ref, wna_ref, wmla_ref, wout_ref, g_ref, b_ref,
                      o_ref, mix_ref):
    na = na_ref[...]
    at = at_ref[...].T
    d = o_ref.shape[1]
    for c in range(d // MIX_CHUNK):
        cs = slice(c * MIX_CHUNK, (c + 1) * MIX_CHUNK)
        ya = _dot(na, wna_ref[:, cs])
        yb = _dot(at, wmla_ref[:, cs])
        mix_ref[:, cs] = (ga_ref[:, cs].astype(F32) * ya + gb_ref[:, cs].astype(F32) * yb).astype(BF16)
    y = _dot(mix_ref[...], wout_ref[...])
    o_ref[...] = _layer_norm(ALPHA * x_ref[...] + y, g_ref[...], b_ref[...])


def mixer_out(x, na, at, ga, gb, w_na_o, w_mla_o, w_out, g, bb):
    b, n, d = x.shape
    tm = TOKEN_TILE
    tok = lambda c: pl.BlockSpec((None, tm, c), lambda bi, i: (bi, i, 0))
    return pl.pallas_call(
        _mixer_out_kernel,
        out_shape=jax.ShapeDtypeStruct((b, n, d), F32),
        grid=(b, n // tm),
        in_specs=[tok(d), tok(NA_WIDTH),
                  pl.BlockSpec((None, at.shape[1], tm), lambda bi, i: (bi, 0, i)),
                  tok(d), tok(d),
                  _const_spec(w_na_o.shape), _const_spec(w_mla_o.shape), _const_spec(w_out.shape),
                  _const_spec(g.shape), _const_spec(bb.shape)],
        out_specs=tok(d),
        scratch_shapes=[pltpu.VMEM((tm, d), BF16)],
        compiler_params=pltpu.CompilerParams(dimension_semantics=("arbitrary", "arbitrary"),
                                             vmem_limit_bytes=VMEM_LIMIT),
        name="mixer_out",
    )(x, na, at, ga, gb, w_na_o, w_mla_o, w_out, g, bb)


def _prep_mixer_weights(w_in, b_gate, q_norm_g, kv_norm_g, w_uq, w_ukv):
    d = w_in.shape[0]
    half = MLA_ROPE // 2
    swap = lambda w: jnp.concatenate([-w[..., half:], w[..., :half]], axis=-1)
    pad_rope = lambda w: jnp.pad(w, ((0, 0), (MLA_NOPE, QK_PAD - MLA_QK)))
    c_kr = 3 * NA_WIDTH + MLA_Q_RANK + MLA_KV_RANK
    w_kr = w_in[:, c_kr:c_kr + MLA_ROPE]
    w_packed = jnp.concatenate([w_in[:, :c_kr], pad_rope(w_kr), pad_rope(swap(w_kr)),
                                w_in[:, c_kr + MLA_ROPE:]], axis=1)
    ukv = w_ukv.reshape(MLA_KV_RANK, MLA_HEADS, MLA_NOPE + MLA_V)
    wk = jnp.pad(ukv[..., :MLA_NOPE], ((0, 0), (0, 0), (0, QK_PAD - MLA_NOPE)))
    wk = wk.reshape(MLA_KV_RANK, MLA_HEADS * QK_PAD)
    wvt = ukv[..., MLA_NOPE:].reshape(MLA_KV_RANK, MLA_HEADS * MLA_V).T
    uq = w_uq.reshape(MLA_Q_RANK, MLA_HEADS, MLA_QK)
    wqt = jnp.pad(uq, ((0, 0), (0, 0), (0, QK_PAD - MLA_QK))).reshape(MLA_Q_RANK, MLA_HEADS * QK_PAD).T
    wqs = swap(uq[..., MLA_NOPE:]).reshape(MLA_Q_RANK, MLA_HEADS * MLA_ROPE).T
    return {
        "w_in": w_packed.astype(BF16), "b_gate": b_gate.reshape(1, 2 * d),
        "q_norm_g": q_norm_g.reshape(1, -1), "kv_norm_g": kv_norm_g.reshape(1, -1),
        "wk": wk.astype(BF16), "wvt": wvt.astype(BF16), "wqt": wqt.astype(BF16), "wqs": wqs.astype(BF16),
    }


def _rope_tables(n):
    inv = 1.0 / (ROPE_BASE ** (jnp.arange(0, MLA_ROPE, 2, dtype=F32) / MLA_ROPE))
    ang = jnp.arange(n, dtype=F32)[:, None] * inv[None, :]
    cos2 = jnp.tile(jnp.cos(ang), (1, 2))
    sin2 = jnp.tile(jnp.sin(ang), (1, 2))
    pad = ((0, 0), (MLA_NOPE, QK_PAD - MLA_QK))
    return {"ck": jnp.pad(cos2, pad), "sk": jnp.pad(sin2, pad), "cq": cos2.T, "sq": sin2.T}


def _na_bias_table(rpb):
    qc = np.arange(GRID_W)[:, None]
    kc = np.arange(GRID_W)[None, :]
    dc = np.clip(kc - qc + NA_KW - 1, 0, 2 * NA_KW - 2)
    onehot = (dc[None] == np.arange(2 * NA_KW - 1)[:, None, None]).astype(np.float32)
    win = np.clip(qc - NA_KW // 2, 0, GRID_W - NA_KW)
    in_win = (kc >= win) & (kc < win + NA_KW)
    t = jnp.einsum("hrc,cqk->hrqk", rpb, jnp.asarray(onehot), precision=lax.Precision.HIGHEST)
    t = jnp.where(in_win, t, NEG_BIG)
    bias = jnp.stack([t[:, NA_KH - 1 - dl:2 * NA_KH - 1 - dl] for dl in range(NA_KH)])
    bias = bias.transpose(0, 1, 3, 2, 4)
    return bias.reshape(NA_KH, NA_HEADS // 2, 2 * GRID_W, NA_KH * GRID_W)


def _encoder_layer(x, p, tabs):
    b, n, d = x.shape
    x1 = ffn_ln(x.reshape(b * n, d), p["ffn1_w_in"], p["ffn1_w_out"], p["ln1_g"], p["ln1_b"])
    x1 = x1.reshape(b, n, d)
    naq, nak, nav, qt, k, vt, ga, gb, qn2, lb, kn2 = mixer_in(x1, p["mixer"], tabs)
    na = na_attention(naq, nak, nav, p["na_bias"])
    at = mla_attention(qt, k, vt, qn2, lb, kn2).reshape(b, MLA_HEADS * MLA_V, n)
    x2 = mixer_out(x1, na, at, ga, gb, p["w_na_o"], p["w_mla_o"], p["w_out"], p["ln2_g"], p["ln2_b"])
    y = ffn_ln(x2.reshape(b * n, d), p["ffn2_w_in"], p["ffn2_w_out"], p["ln3_g"], p["ln3_b"])
    return y.reshape(b, n, d)


def kernel(x_prompt, x_sample, ffn1_w_in, ffn1_w_out, ln1_g, ln1_b, w_in, b_gate, na_rpb, q_norm_g, kv_norm_g, w_uq, w_ukv, w_na_o, w_mla_o, w_out, ln2_g, ln2_b, ffn2_w_in, ffn2_w_out, ln3_g, ln3_b):
    l = 0
    row = lambda a: a[l].reshape(1, -1)
    p = {
        "ffn1_w_in": ffn1_w_in[l].astype(BF16), "ffn1_w_out": ffn1_w_out[l].astype(BF16),
        "ln1_g": row(ln1_g), "ln1_b": row(ln1_b),
        "mixer": _prep_mixer_weights(w_in[l], b_gate[l], q_norm_g[l], kv_norm_g[l], w_uq[l], w_ukv[l]),
        "na_bias": _na_bias_table(na_rpb[l]),
        "w_na_o": w_na_o[l].astype(BF16), "w_mla_o": w_mla_o[l].astype(BF16), "w_out": w_out[l].astype(BF16),
        "ln2_g": row(ln2_g), "ln2_b": row(ln2_b),
        "ffn2_w_in": ffn2_w_in[l].astype(BF16), "ffn2_w_out": ffn2_w_out[l].astype(BF16),
        "ln3_g": row(ln3_g), "ln3_b": row(ln3_b),
    }
    outs = []
    for x in (x_prompt, x_sample):
        outs.append(_encoder_layer(x, p, _rope_tables(x.shape[1])))
    return tuple(outs)
```

```python
import functools
import math

import numpy as np
import jax
import jax.numpy as jnp
from jax import lax
from jax.experimental import pallas as pl
from jax.experimental.pallas import tpu as pltpu

F32 = jnp.float32
BF16 = jnp.bfloat16

DEPTH = 1
GRID_W = 64
NA_HEADS = 8
NA_HEAD_DIM = 64
NA_WIDTH = NA_HEADS * NA_HEAD_DIM
NA_KH = 8
NA_KW = 16
MLA_HEADS = 8
MLA_NOPE = 64
MLA_ROPE = 32
MLA_QK = MLA_NOPE + MLA_ROPE
MLA_V = 64
MLA_VA = MLA_V + 16
MLA_Q_RANK = 384
MLA_KV_RANK = 256
ROPE_BASE = 10000.0
LN_EPS = 1e-5
RMS_EPS = 1e-6
ALPHA = (2.0 * DEPTH) ** 0.25
LOG2E = math.log2(math.e)

LANES = 128
QK_PAD = 128
VMEM_LIMIT = 56 * 1024 * 1024

TOKEN_TILE = 512
FFN_TILE = 1024
FFN_CHUNK = 256
MIX_CHUNK = 256
NA_ROWS_PER_STEP = 32
NA_BLOCKS_PER_ITER = 6
MLA_TQ = 512
MLA_TK = 256
MLA_GROUP = 4
MLA_NBUF = 2
MLA_BLOCKS_PER_ITER = 10
MLA_PROBE_KEYS = 16
MLA_BOUND_INFLATE = 1.02
MLA_MAX_GAP = 64.0
NEG_BIG = -1e30


def _const_spec(shape):
    nd = len(shape)
    return pl.BlockSpec(shape, lambda *_: (0,) * nd, pipeline_mode=pl.Buffered(1))


def _layer_norm(y, g, b):
    mu = jnp.mean(y, axis=-1, keepdims=True)
    d = y - mu
    var = jnp.mean(d * d, axis=-1, keepdims=True)
    return d * lax.rsqrt(var + LN_EPS) * g + b


def _rms_norm(y, g):
    return y * lax.rsqrt(jnp.mean(y * y, axis=-1, keepdims=True) + RMS_EPS) * g


def _dot(a, b):
    return jnp.dot(a, b, preferred_element_type=F32)


def _swiglu_ln(x, w_in_ref, w_out_ref, g_ref, b_ref, h_ref, d_ff):
    xb = x.astype(BF16)
    for c in range(d_ff // FFN_CHUNK):
        lo, hi = c * FFN_CHUNK, (c + 1) * FFN_CHUNK
        a = _dot(xb, w_in_ref[:, lo:hi])
        u = _dot(xb, w_in_ref[:, d_ff + lo:d_ff + hi])
        h_ref[:, lo:hi] = (a * jax.nn.sigmoid(a) * u).astype(BF16)
    y = _dot(h_ref[...], w_out_ref[...])
    return _layer_norm(ALPHA * x + 0.5 * y, g_ref[...], b_ref[...])


def _ffn_ln_kernel(x_ref, w_in_ref, w_out_ref, g_ref, b_ref, o_ref, h_ref, *, d_ff):
    o_ref[...] = _swiglu_ln(x_ref[...], w_in_ref, w_out_ref, g_ref, b_ref, h_ref, d_ff)


def ffn_ln(x, w_in, w_out, g, b):
    t, d = x.shape
    d_ff = w_out.shape[0]
    tm = FFN_TILE
    return pl.pallas_call(
        functools.partial(_ffn_ln_kernel, d_ff=d_ff),
        out_shape=jax.ShapeDtypeStruct((t, d), F32),
        grid=(t // tm,),
        in_specs=[pl.BlockSpec((tm, d), lambda i: (i, 0)),
                  _const_spec(w_in.shape), _const_spec(w_out.shape),
                  _const_spec(g.shape), _const_spec(b.shape)],
        out_specs=pl.BlockSpec((tm, d), lambda i: (i, 0)),
        scratch_shapes=[pltpu.VMEM((tm, d_ff), BF16)],
        compiler_params=pltpu.CompilerParams(dimension_semantics=("arbitrary",),
                                             vmem_limit_bytes=VMEM_LIMIT),
        name="ffn_ln",
    )(x, w_in, w_out, g, b)


_C_NAQ, _C_NAK, _C_NAV = 0, NA_WIDTH, 2 * NA_WIDTH
_C_CQ = 3 * NA_WIDTH
_C_CKV = _C_CQ + MLA_Q_RANK
_C_KR = _C_CKV + MLA_KV_RANK
_C_GATE = _C_KR + 2 * QK_PAD


def _mixer_in_kernel(x_ref, w_ref, bg_ref, qg_ref, kvg_ref, wk_ref, wvt_ref, wqt_ref, wqs_ref,
                     ck_ref, sk_ref, cq_ref, sq_ref,
                     naq_ref, nak_ref, nav_ref, qt_ref, k_ref, vt_ref, ga_ref, gb_ref,
                     qn2_ref, lb_ref, kn2_ref, *, d_model):
    xb = x_ref[...].astype(BF16)
    naq_ref[...] = (_dot(xb, w_ref[:, _C_NAQ:_C_NAK]) * (NA_HEAD_DIM ** -0.5)).astype(BF16)
    nak_ref[...] = _dot(xb, w_ref[:, _C_NAK:_C_NAV]).astype(BF16)
    nav_ref[...] = _dot(xb, w_ref[:, _C_NAV:_C_CQ]).astype(BF16)

    cqn = _rms_norm(_dot(xb, w_ref[:, _C_CQ:_C_CKV]), qg_ref[...])
    ckvn = _rms_norm(_dot(xb, w_ref[:, _C_CKV:_C_KR]), kvg_ref[...])

    kr = _dot(xb, w_ref[:, _C_KR:_C_GATE])
    kr_blk = kr[:, :QK_PAD] * ck_ref[...] + kr[:, QK_PAD:] * sk_ref[...]
    kall = _dot(ckvn.astype(BF16), wk_ref[...])
    k_probe = []
    one_hot = (lax.broadcasted_iota(jnp.int32, kr_blk.shape, 1) == MLA_QK).astype(F32)
    for h in range(MLA_HEADS):
        kf = kall[:, h * QK_PAD:(h + 1) * QK_PAD] + kr_blk
        kb = (kf + one_hot).astype(BF16)
        k_ref[h] = kb
        k2 = jnp.max(jnp.sum(kf * kf, axis=1, keepdims=True), axis=0, keepdims=True)
        kn2_ref[h] = jnp.broadcast_to(k2, (1, LANES))
        k_probe.append(kb[0:MLA_PROBE_KEYS])

    ckvn_t = ckvn.T.astype(BF16)
    vt = _dot(wvt_ref[...], ckvn_t)
    ones = jnp.ones((MLA_VA - MLA_V, vt.shape[1]), BF16)
    for h in range(MLA_HEADS):
        vt_ref[h, 0:MLA_V] = vt[h * MLA_V:(h + 1) * MLA_V].astype(BF16)
        vt_ref[h, MLA_V:MLA_VA] = ones

    cqn_t = cqn.T.astype(BF16)
    q_scale = (MLA_QK ** -0.5) * LOG2E
    qt = _dot(wqt_ref[...], cqn_t) * q_scale
    qs = _dot(wqs_ref[...], cqn_t) * q_scale
    cq, sq = cq_ref[...], sq_ref[...]
    zeros = jnp.zeros((QK_PAD - MLA_QK, qt.shape[1]), BF16)
    for h in range(MLA_HEADS):
        base = h * QK_PAD
        nope = qt[base:base + MLA_NOPE]
        rope = qt[base + MLA_NOPE:base + MLA_QK] * cq + qs[h * MLA_ROPE:(h + 1) * MLA_ROPE] * sq
        qh = jnp.concatenate([nope.astype(BF16), rope.astype(BF16), zeros], axis=0)
        qt_ref[h] = qh
        qn2_ref[h] = (jnp.sum(nope * nope, axis=0, keepdims=True)
                      + jnp.sum(rope * rope, axis=0, keepdims=True))
        lb_ref[h] = jnp.max(_dot(k_probe[h], qh), axis=0, keepdims=True)

    for half, out_ref in enumerate((ga_ref, gb_ref)):
        lo = _C_GATE + half * d_model
        logits = _dot(xb, w_ref[:, lo:lo + d_model]) + bg_ref[:, half * d_model:(half + 1) * d_model]
        out_ref[...] = jax.nn.sigmoid(logits).astype(BF16)


def mixer_in(x, wts, tabs):
    b, n, d = x.shape
    tm = TOKEN_TILE
    h = MLA_HEADS
    tok = lambda c: pl.BlockSpec((None, tm, c), lambda bi, i: (bi, i, 0))
    out_shape = (
        jax.ShapeDtypeStruct((b, n, NA_WIDTH), BF16), jax.ShapeDtypeStruct((b, n, NA_WIDTH), BF16),
        jax.ShapeDtypeStruct((b, n, NA_WIDTH), BF16),
        jax.ShapeDtypeStruct((b, h, QK_PAD, n), BF16),
        jax.ShapeDtypeStruct((b, h, n, QK_PAD), BF16),
        jax.ShapeDtypeStruct((b, h, MLA_VA, n), BF16),
        jax.ShapeDtypeStruct((b, n, d), BF16), jax.ShapeDtypeStruct((b, n, d), BF16),
        jax.ShapeDtypeStruct((b, h, 1, n), F32), jax.ShapeDtypeStruct((b, h, 1, n), F32),
        jax.ShapeDtypeStruct((b, n // tm, h, 1, LANES), F32),
    )
    stat = pl.BlockSpec((None, h, 1, tm), lambda bi, i: (bi, 0, 0, i))
    out_specs = (
        tok(NA_WIDTH), tok(NA_WIDTH), tok(NA_WIDTH),
        pl.BlockSpec((None, h, QK_PAD, tm), lambda bi, i: (bi, 0, 0, i)),
        pl.BlockSpec((None, h, tm, QK_PAD), lambda bi, i: (bi, 0, i, 0)),
        pl.BlockSpec((None, h, MLA_VA, tm), lambda bi, i: (bi, 0, 0, i)),
        tok(d), tok(d),
        stat, stat,
        pl.BlockSpec((None, None, h, 1, LANES), lambda bi, i: (bi, i, 0, 0, 0)),
    )
    consts = [wts["w_in"], wts["b_gate"], wts["q_norm_g"], wts["kv_norm_g"],
              wts["wk"], wts["wvt"], wts["wqt"], wts["wqs"]]
    in_specs = ([tok(d)] + [_const_spec(c.shape) for c in consts] + [
        pl.BlockSpec((tm, QK_PAD), lambda bi, i: (i, 0)),
        pl.BlockSpec((tm, QK_PAD), lambda bi, i: (i, 0)),
        pl.BlockSpec((MLA_ROPE, tm), lambda bi, i: (0, i)),
        pl.BlockSpec((MLA_ROPE, tm), lambda bi, i: (0, i)),
    ])
    return pl.pallas_call(
        functools.partial(_mixer_in_kernel, d_model=d),
        out_shape=out_shape,
        grid=(b, n // tm),
        in_specs=in_specs,
        out_specs=out_specs,
        compiler_params=pltpu.CompilerParams(dimension_semantics=("arbitrary", "arbitrary"),
                                             vmem_limit_bytes=VMEM_LIMIT),
        name="mixer_in",
    )(x, *consts, tabs["ck"], tabs["sk"], tabs["cq"], tabs["sq"])


def _na_kernel(q_ref, kp_ref, km_ref, kn_ref, vp_ref, vm_ref, vn_ref, bias_ref, o_ref, kbuf, vbuf,
               s0, s1, p0, p1, r0, r1, *, rows):
    g = NA_ROWS_PER_STEP
    halo = NA_KH * GRID_W
    main = g * GRID_W
    window = NA_KH * GRID_W
    npair = NA_HEADS // 2
    i = pl.program_id(1)
    kbuf[0:halo] = kp_ref[...]
    kbuf[halo:halo + main] = km_ref[...]
    kbuf[halo + main:] = kn_ref[...]
    vbuf[0:halo] = vp_ref[...]
    vbuf[halo:halo + main] = vm_ref[...]
    vbuf[halo + main:] = vn_ref[...]
    lane = lax.broadcasted_iota(jnp.int32, (GRID_W, LANES), 1)
    first_head = lane < NA_HEAD_DIM
    s_bufs, p_bufs, r_bufs = (s0, s1), (p0, p1), (r0, r1)
    lanes_of = lambda hp: slice(hp * LANES, (hp + 1) * LANES)

    def geometry(rho):
        r = i * g + rho
        rs = jnp.clip(r - NA_KH // 2, 0, rows - NA_KH)
        off = pl.multiple_of((rs - (i * g - NA_KH)) * GRID_W, GRID_W)
        return r - rs, off, pl.multiple_of(rho * GRID_W, GRID_W)

    def scores(rho, slot):
        delta, off, qoff = geometry(rho)
        for hp in range(npair):
            q2 = q_ref[pl.ds(qoff, GRID_W), lanes_of(hp)]
            zero = jnp.zeros_like(q2)
            qs = jnp.concatenate([jnp.where(first_head, q2, zero),
                                  jnp.where(first_head, zero, q2)], axis=0)
            k2 = kbuf[pl.ds(off, window), lanes_of(hp)]
            s = lax.dot_general(qs, k2, (((1,), (1,)), ((), ())), preferred_element_type=F32)
            s_bufs[slot][hp] = s + bias_ref[delta, hp]

    def softmax(slot):
        for hp in range(npair):
            for rows_ in (slice(0, GRID_W), slice(GRID_W, 2 * GRID_W)):
                s = s_bufs[slot][hp, rows_]
                p = jnp.exp(s - jnp.max(s, axis=1, keepdims=True))
                p_bufs[slot][hp, rows_] = p.astype(BF16)
                r_bufs[slot][hp, rows_] = jnp.broadcast_to(1.0 / jnp.sum(p, axis=1, keepdims=True),
                                                           (GRID_W, LANES))

    def values(rho, slot):
        _, off, qoff = geometry(rho)
        for hp in range(npair):
            o = _dot(p_bufs[slot][hp], vbuf[pl.ds(off, window), lanes_of(hp)])
            o = o * r_bufs[slot][hp]
            out2 = jnp.where(first_head, o[:GRID_W], o[GRID_W:])
            o_ref[pl.ds(qoff, GRID_W), lanes_of(hp)] = out2.astype(BF16)

    def block(t, slot, do_scores=True, do_softmax=True, do_values=True):
        if do_scores:
            scores(t + 1, 1 - slot)
        if do_softmax:
            softmax(slot)
        if do_values:
            values(t - 1, 1 - slot)

    assert g % 2 == 0 and g >= 4
    block(-1, 1, do_softmax=False, do_values=False)
    block(0, 0, do_values=False)

    unroll = NA_BLOCKS_PER_ITER
    assert unroll % 2 == 0 and (g - 2) % unroll == 0

    def blocks(j, carry):
        for kk in range(unroll):
            block(unroll * j + 1 + kk, (1 + kk) % 2)
        return carry

    lax.fori_loop(0, (g - 2) // unroll, blocks, 0)
    block(g - 1, 1, do_scores=False)
    block(g, 0, do_scores=False, do_softmax=False)


def na_attention(q, k, v, bias):
    b, n, c = q.shape
    rows = n // GRID_W
    g = NA_ROWS_PER_STEP
    main = g * GRID_W
    halo = NA_KH * GRID_W
    per = main // halo
    last = n // halo - 1
    spec_main = pl.BlockSpec((None, main, c), lambda bi, i: (bi, i, 0))
    spec_prev = pl.BlockSpec((None, halo, c), lambda bi, i: (bi, jnp.maximum(i * per - 1, 0), 0))
    spec_next = pl.BlockSpec((None, halo, c), lambda bi, i: (bi, jnp.minimum((i + 1) * per, last), 0))
    return pl.pallas_call(
        functools.partial(_na_kernel, rows=rows),
        out_shape=jax.ShapeDtypeStruct((b, n, c), BF16),
        grid=(b, rows // g),
        in_specs=[spec_main, spec_prev, spec_main, spec_next, spec_prev, spec_main, spec_next,
                  _const_spec(bias.shape)],
        out_specs=spec_main,
        scratch_shapes=[pltpu.VMEM((main + 2 * halo, c), BF16), pltpu.VMEM((main + 2 * halo, c), BF16)]
        + [pltpu.VMEM(bias.shape[1:], F32)] * 2 + [pltpu.VMEM(bias.shape[1:], BF16)] * 2
        + [pltpu.VMEM(bias.shape[1:3] + (LANES,), F32)] * 2,
        compiler_params=pltpu.CompilerParams(dimension_semantics=("arbitrary", "arbitrary"),
                                             vmem_limit_bytes=VMEM_LIMIT),
        name="na_attention",
    )(q, k, k, k, v, v, v, bias)


def _mla_kernel(qt_ref, k_ref, vt_ref, o_ref, *bufs, n):
    tq, tk, g = MLA_TQ, MLA_TK, MLA_GROUP
    ng = n // (tk * g)
    total = (n // tq) * ng
    nbuf = MLA_NBUF
    assert total >= 4
    s_bufs, p_bufs = bufs[:nbuf], bufs[nbuf:]

    def split(u):
        return lax.div(u, ng), lax.rem(u, ng)

    def scores_chunk(u, slot, c):
        qi, kg = split(u)
        qt = qt_ref[:, pl.ds(pl.multiple_of(qi * tq, tq), tq)]
        off = pl.multiple_of((kg * g + c) * tk, tk)
        s = _dot(k_ref[pl.ds(off, tk), :], qt)
        s_bufs[slot][c * tk:(c + 1) * tk] = s
        return jnp.max(s, axis=0, keepdims=True)

    def first_group_reset(u, m):
        _, kg = split(u)
        return jnp.where(kg == 0, -jnp.inf, m)

    def softmax_chunk(slot, c, m, mx):
        m_new = jnp.maximum(m, mx)
        p_bufs[slot][c * tk:(c + 1) * tk] = jnp.exp2(s_bufs[slot][c * tk:(c + 1) * tk] - m_new).astype(BF16)
        return m_new, jnp.exp2(m - m_new)

    def values_chunk(u, slot, c, r, acc):
        _, kg = split(u)
        off = pl.multiple_of((kg * g + c) * tk, tk)
        return r * acc + _dot(vt_ref[:, pl.ds(off, tk)], p_bufs[slot][c * tk:(c + 1) * tk])

    def write_out(u, acc):
        qi, _ = split(u)
        out = acc[:MLA_V] * (1.0 / acc[MLA_V:MLA_V + 1])
        o_ref[:, pl.ds(pl.multiple_of(qi * tq, tq), tq)] = out.astype(BF16)

    def block(t, slot, carry, do_scores=True, do_softmax=True, do_values=True):
        m, mxs, rs_prev, acc = carry
        if do_softmax:
            m = first_group_reset(t, m)
        mxs_next, rs = list(mxs), list(rs_prev)
        if do_scores:
            for c in range(g):
                mxs_next[c] = scores_chunk(t + 1, (slot + 1) % nbuf, c)
        if do_softmax:
            for c in range(g):
                m, rs[c] = softmax_chunk(slot, c, m, mxs[c])
        if do_values:
            for c in range(g):
                acc = values_chunk(t - 1, (slot - 1) % nbuf, c, rs_prev[c], acc)
            write_out(t - 1, acc)
        return m, tuple(mxs_next), tuple(rs), acc

    row = jnp.zeros((1, tq), F32)
    carry = (row, (row,) * g, (row,) * g, jnp.zeros((MLA_VA, tq), F32))
    carry = block(-1, -1 % nbuf, carry, do_softmax=False, do_values=False)
    carry = block(0, 0, carry, do_values=False)

    unroll = MLA_BLOCKS_PER_ITER
    assert unroll % nbuf == 0
    n_iter, n_rest = divmod(total - 2, unroll)

    def blocks(i, carry):
        t0 = unroll * i + 1
        for kk in range(unroll):
            carry = block(t0 + kk, (1 + kk) % nbuf, carry)
        return carry

    carry = lax.fori_loop(0, n_iter, blocks, carry)
    for t in range(n_iter * unroll + 1, n_iter * unroll + 1 + n_rest):
        carry = block(t, t % nbuf, carry)
    carry = block(total - 1, (total - 1) % nbuf, carry, do_scores=False)
    block(total, total % nbuf, carry, do_scores=False, do_softmax=False)


def _mla_fast_kernel(qt_ref, k_ref, vt_ref, m_ref, o_ref, p0, p1, *, n):
    tq, tk, g = MLA_TQ, MLA_TK, MLA_GROUP
    ng = n // (tk * g)
    total = (n // tq) * ng
    p_bufs = (p0, p1)

    def split(u):
        return lax.div(u, ng), lax.rem(u, ng)

    def probs(u, slot, l):
        qi, kg = split(u)
        qoff = pl.multiple_of(qi * tq, tq)
        qt = qt_ref[:, pl.ds(qoff, tq)]
        m = m_ref[:, pl.ds(qoff, tq)]
        feat = lax.broadcasted_iota(jnp.int32, (QK_PAD, tq), 0)
        qt = jnp.where(feat == MLA_QK, -m, qt.astype(F32)).astype(BF16)
        l = jnp.where(kg == 0, 0.0, l)
        for c in range(g):
            off = pl.multiple_of((kg * g + c) * tk, tk)
            p = jnp.exp2(_dot(k_ref[pl.ds(off, tk), :], qt))
            p_bufs[slot][c * tk:(c + 1) * tk] = p.astype(BF16)
            l = l + jnp.sum(p, axis=0, keepdims=True)
        return l

    def values(u, slot, acc, l):
        qi, kg = split(u)
        acc = jnp.where(kg == 0, 0.0, acc)
        for c in range(g):
            off = pl.multiple_of((kg * g + c) * tk, tk)
            acc = acc + _dot(vt_ref[0:MLA_V, pl.ds(off, tk)], p_bufs[slot][c * tk:(c + 1) * tk])
        o_ref[:, pl.ds(pl.multiple_of(qi * tq, tq), tq)] = (acc * (1.0 / l)).astype(BF16)
        return acc

    def block(t, slot, carry):
        acc, l = carry
        l_next = probs(t + 1, 1 - slot, l)
        return values(t, slot, acc, l), l_next

    unroll = MLA_BLOCKS_PER_ITER
    assert unroll % 2 == 0 and total >= 2
    carry = (jnp.zeros((MLA_V, tq), F32), probs(0, 0, jnp.zeros((1, tq), F32)))
    carry = block(0, 0, carry)
    n_iter, n_rest = divmod(total - 2, unroll)

    def blocks(i, carry):
        t0 = unroll * i + 1
        for kk in range(unroll):
            carry = block(t0 + kk, (1 + kk) % 2, carry)
        return carry

    carry = lax.fori_loop(0, n_iter, blocks, carry)
    for t in range(n_iter * unroll + 1, total - 1):
        carry = block(t, t % 2, carry)
    values(total - 1, (total - 1) % 2, *carry)


def mla_attention(qt, k, vt, qn2, lb, kn2):
    b, h, _, n = qt.shape
    rows = MLA_GROUP * MLA_TK
    head = lambda r, c: pl.BlockSpec((None, None, r, c), lambda bi, hi: (bi, hi, 0, 0))
    params = pltpu.CompilerParams(dimension_semantics=("arbitrary",) * 2, vmem_limit_bytes=VMEM_LIMIT)
    out_shape = jax.ShapeDtypeStruct((b, h, MLA_V, n), BF16)

    k2max = jnp.max(kn2, axis=(1, 3, 4))
    m = (jnp.sqrt(qn2 * k2max[:, :, None, None]) * MLA_BOUND_INFLATE).astype(BF16).astype(F32)
    gap = m - lb

    def fast(qt, k, vt, m):
        return pl.pallas_call(
            functools.partial(_mla_fast_kernel, n=n),
            out_shape=out_shape,
            grid=(b, h),
            in_specs=[head(QK_PAD, n), head(n, QK_PAD), head(MLA_VA, n), head(1, n)],
            out_specs=head(MLA_V, n),
            scratch_shapes=[pltpu.VMEM((rows, MLA_TQ), BF16)] * 2,
            compiler_params=params,
            name="mla_attention_fast",
        )(qt, k, vt, m)

    def exact(qt, k, vt, m):
        return pl.pallas_call(
            functools.partial(_mla_kernel, n=n),
            out_shape=out_shape,
            grid=(b, h),
            in_specs=[head(QK_PAD, n), head(n, QK_PAD), head(MLA_VA, n)],
            out_specs=head(MLA_V, n),
            scratch_shapes=[pltpu.VMEM((rows, MLA_TQ), F32)] * MLA_NBUF + [pltpu.VMEM((rows, MLA_TQ), BF16)] * MLA_NBUF,
            compiler_params=params,
            name="mla_attention",
        )(qt, k, vt)

    return lax.cond(jnp.max(gap) < MLA_MAX_GAP, fast, exact, qt, k, vt, m)


def _mixer_out_kernel(x_ref, na_ref, at_ref, ga_ref, gb_ref, wna_ref, wmla_ref, wout_ref, g_ref, b_ref,
                      o_ref, mix_ref):
    na = na_ref[...]
    at = at_ref[...].T
    d = o_ref.shape[1]
    for c in range(d // MIX_CHUNK):
        cs = slice(c * MIX_CHUNK, (c + 1) * MIX_CHUNK)
        ya = _dot(na, wna_ref[:, cs])
        yb = _dot(at, wmla_ref[:, cs])
        mix_ref[:, cs] = (ga_ref[:, cs].astype(F32) * ya + gb_ref[:, cs].astype(F32) * yb).astype(BF16)
    y = _dot(mix_ref[...], wout_ref[...])
    o_ref[...] = _layer_norm(ALPHA * x_ref[...] + y, g_ref[...], b_ref[...])


def mixer_out(x, na, at, ga, gb, w_na_o, w_mla_o, w_out, g, bb):
    b, n, d = x.shape
    tm = FFN_TILE
    tok = lambda c: pl.BlockSpec((None, tm, c), lambda bi, i: (bi, i, 0))
    return pl.pallas_call(
        _mixer_out_kernel,
        out_shape=jax.ShapeDtypeStruct((b, n, d), F32),
        grid=(b, n // tm),
        in_specs=[tok(d), tok(NA_WIDTH),
                  pl.BlockSpec((None, at.shape[1], tm), lambda bi, i: (bi, 0, i)),
                  tok(d), tok(d),
                  _const_spec(w_na_o.shape), _const_spec(w_mla_o.shape), _const_spec(w_out.shape),
                  _const_spec(g.shape), _const_spec(bb.shape)],
        out_specs=tok(d),
        scratch_shapes=[pltpu.VMEM((tm, d), BF16)],
        compiler_params=pltpu.CompilerParams(dimension_semantics=("arbitrary", "arbitrary"),
                                             vmem_limit_bytes=VMEM_LIMIT),
        name="mixer_out",
    )(x, na, at, ga, gb, w_na_o, w_mla_o, w_out, g, bb)


def _prep_mixer_weights(w_in, b_gate, q_norm_g, kv_norm_g, w_uq, w_ukv):
    d = w_in.shape[0]
    half = MLA_ROPE // 2
    swap = lambda w: jnp.concatenate([-w[..., half:], w[..., :half]], axis=-1)
    pad_rope = lambda w: jnp.pad(w, ((0, 0), (MLA_NOPE, QK_PAD - MLA_QK)))
    c_kr = 3 * NA_WIDTH + MLA_Q_RANK + MLA_KV_RANK
    w_kr = w_in[:, c_kr:c_kr + MLA_ROPE]
    w_packed = jnp.concatenate([w_in[:, :c_kr], pad_rope(w_kr), pad_rope(swap(w_kr)),
                                w_in[:, c_kr + MLA_ROPE:]], axis=1)
    ukv = w_ukv.reshape(MLA_KV_RANK, MLA_HEADS, MLA_NOPE + MLA_V)
    wk = jnp.pad(ukv[..., :MLA_NOPE], ((0, 0), (0, 0), (0, QK_PAD - MLA_NOPE)))
    wk = wk.reshape(MLA_KV_RANK, MLA_HEADS * QK_PAD)
    wvt = ukv[..., MLA_NOPE:].reshape(MLA_KV_RANK, MLA_HEADS * MLA_V).T
    uq = w_uq.reshape(MLA_Q_RANK, MLA_HEADS, MLA_QK)
    wqt = jnp.pad(uq, ((0, 0), (0, 0), (0, QK_PAD - MLA_QK))).reshape(MLA_Q_RANK, MLA_HEADS * QK_PAD).T
    wqs = swap(uq[..., MLA_NOPE:]).reshape(MLA_Q_RANK, MLA_HEADS * MLA_ROPE).T
    return {
        "w_in": w_packed.astype(BF16), "b_gate": b_gate.reshape(1, 2 * d),
        "q_norm_g": q_norm_g.reshape(1, -1), "kv_norm_g": kv_norm_g.reshape(1, -1),
        "wk": wk.astype(BF16), "wvt": wvt.astype(BF16), "wqt": wqt.astype(BF16), "wqs": wqs.astype(BF16),
    }


def _rope_tables(n):
    inv = 1.0 / (ROPE_BASE ** (jnp.arange(0, MLA_ROPE, 2, dtype=F32) / MLA_ROPE))
    ang = jnp.arange(n, dtype=F32)[:, None] * inv[None, :]
    cos2 = jnp.tile(jnp.cos(ang), (1, 2))
    sin2 = jnp.tile(jnp.sin(ang), (1, 2))
    pad = ((0, 0), (MLA_NOPE, QK_PAD - MLA_QK))
    return {"ck": jnp.pad(cos2, pad), "sk": jnp.pad(sin2, pad), "cq": cos2.T, "sq": sin2.T}


def _na_bias_table(rpb):
    qc = np.arange(GRID_W)[:, None]
    kc = np.arange(GRID_W)[None, :]
    dc = np.clip(kc - qc + NA_KW - 1, 0, 2 * NA_KW - 2)
    onehot = (dc[None] == np.arange(2 * NA_KW - 1)[:, None, None]).astype(np.float32)
    win = np.clip(qc - NA_KW // 2, 0, GRID_W - NA_KW)
    in_win = (kc >= win) & (kc < win + NA_KW)
    t = jnp.einsum("hrc,cqk->hrqk", rpb, jnp.asarray(onehot), precision=lax.Precision.HIGHEST)
    t = jnp.where(in_win, t, NEG_BIG)
    bias = jnp.stack([t[:, NA_KH - 1 - dl:2 * NA_KH - 1 - dl] for dl in range(NA_KH)])
    bias = bias.transpose(0, 1, 3, 2, 4)
    return bias.reshape(NA_KH, NA_HEADS // 2, 2 * GRID_W, NA_KH * GRID_W)


def _encoder_layer(x, p, tabs):
    b, n, d = x.shape
    x1 = ffn_ln(x.reshape(b * n, d), p["ffn1_w_in"], p["ffn1_w_out"], p["ln1_g"], p["ln1_b"])
    x1 = x1.reshape(b, n, d)
    naq, nak, nav, qt, k, vt, ga, gb, qn2, lb, kn2 = mixer_in(x1, p["mixer"], tabs)
    na = na_attention(naq, nak, nav, p["na_bias"])
    at = mla_attention(qt, k, vt, qn2, lb, kn2).reshape(b, MLA_HEADS * MLA_V, n)
    x2 = mixer_out(x1, na, at, ga, gb, p["w_na_o"], p["w_mla_o"], p["w_out"], p["ln2_g"], p["ln2_b"])
    y = ffn_ln(x2.reshape(b * n, d), p["ffn2_w_in"], p["ffn2_w_out"], p["ln3_g"], p["ln3_b"])
    return y.reshape(b, n, d)


def kernel(x_prompt, x_sample, ffn1_w_in, ffn1_w_out, ln1_g, ln1_b, w_in, b_gate, na_rpb, q_norm_g, kv_norm_g, w_uq, w_ukv, w_na_o, w_mla_o, w_out, ln2_g, ln2_b, ffn2_w_in, ffn2_w_out, ln3_g, ln3_b):
    l = 0
    row = lambda a: a[l].reshape(1, -1)
    p = {
        "ffn1_w_in": ffn1_w_in[l].astype(BF16), "ffn1_w_out": ffn1_w_out[l].astype(BF16),
        "ln1_g": row(ln1_g), "ln1_b": row(ln1_b),
        "mixer": _prep_mixer_weights(w_in[l], b_gate[l], q_norm_g[l], kv_norm_g[l], w_uq[l], w_ukv[l]),
        "na_bias": _na_bias_table(na_rpb[l]),
        "w_na_o": w_na_o[l].astype(BF16), "w_mla_o": w_mla_o[l].astype(BF16), "w_out": w_out[l].astype(BF16),
        "ln2_g": row(ln2_g), "ln2_b": row(ln2_b),
        "ffn2_w_in": ffn2_w_in[l].astype(BF16), "ffn2_w_out": ffn2_w_out[l].astype(BF16),
        "ln3_g": row(ln3_g), "ln3_b": row(ln3_b),
    }
    outs = []
    for x in (x_prompt, x_sample):
        outs.append(_encoder_layer(x, p, _rope_tables(x.shape[1])))
    return tuple(outs)
```

```python
import functools
import math

import numpy as np
import jax
import jax.numpy as jnp
from jax import lax
from jax.experimental import pallas as pl
from jax.experimental.pallas import tpu as pltpu

F32 = jnp.float32
BF16 = jnp.bfloat16

DEPTH = 1
GRID_W = 64
NA_HEADS = 8
NA_HEAD_DIM = 64
NA_WIDTH = NA_HEADS * NA_HEAD_DIM
NA_KH = 8
NA_KW = 16
MLA_HEADS = 8
MLA_NOPE = 64
MLA_ROPE = 32
MLA_QK = MLA_NOPE + MLA_ROPE
MLA_V = 64
MLA_VA = MLA_V + 16
MLA_Q_RANK = 384
MLA_KV_RANK = 256
ROPE_BASE = 10000.0
LN_EPS = 1e-5
RMS_EPS = 1e-6
ALPHA = (2.0 * DEPTH) ** 0.25
LOG2E = math.log2(math.e)

LANES = 128
QK_PAD = 128
VMEM_LIMIT = 56 * 1024 * 1024

TOKEN_TILE = 512
FFN_TILE = 1024
FFN_CHUNK = 256
MIX_CHUNK = 256
NA_ROWS_PER_STEP = 32
NA_BLOCKS_PER_ITER = 6
MLA_TQ = 512
MLA_TK = 256
MLA_GROUP = 4
MLA_NBUF = 2
MLA_BLOCKS_PER_ITER = 10
MLA_PROBE_KEYS = 16
MLA_BOUND_INFLATE = 1.02
MLA_MAX_GAP = 64.0
NEG_BIG = -1e30


def _const_spec(shape):
    nd = len(shape)
    return pl.BlockSpec(shape, lambda *_: (0,) * nd, pipeline_mode=pl.Buffered(1))


def _layer_norm(y, g, b):
    mu = jnp.mean(y, axis=-1, keepdims=True)
    d = y - mu
    var = jnp.mean(d * d, axis=-1, keepdims=True)
    return d * lax.rsqrt(var + LN_EPS) * g + b


def _rms_norm(y, g):
    return y * lax.rsqrt(jnp.mean(y * y, axis=-1, keepdims=True) + RMS_EPS) * g


def _dot(a, b):
    return jnp.dot(a, b, preferred_element_type=F32)


def _swiglu_ln(x, w_in_ref, w_out_ref, g_ref, b_ref, h_ref, d_ff):
    xb = x.astype(BF16)
    for c in range(d_ff // FFN_CHUNK):
        lo, hi = c * FFN_CHUNK, (c + 1) * FFN_CHUNK
        a = _dot(xb, w_in_ref[:, lo:hi])
        u = _dot(xb, w_in_ref[:, d_ff + lo:d_ff + hi])
        h_ref[:, lo:hi] = (a * jax.nn.sigmoid(a) * u).astype(BF16)
    y = _dot(h_ref[...], w_out_ref[...])
    return _layer_norm(ALPHA * x + 0.5 * y, g_ref[...], b_ref[...])


def _ffn_ln_kernel(x_ref, w_in_ref, w_out_ref, g_ref, b_ref, o_ref, h_ref, *, d_ff):
    o_ref[...] = _swiglu_ln(x_ref[...], w_in_ref, w_out_ref, g_ref, b_ref, h_ref, d_ff)


def ffn_ln(x, w_in, w_out, g, b):
    t, d = x.shape
    d_ff = w_out.shape[0]
    tm = FFN_TILE
    return pl.pallas_call(
        functools.partial(_ffn_ln_kernel, d_ff=d_ff),
        out_shape=jax.ShapeDtypeStruct((t, d), F32),
        grid=(t // tm,),
        in_specs=[pl.BlockSpec((tm, d), lambda i: (i, 0)),
                  _const_spec(w_in.shape), _const_spec(w_out.shape),
                  _const_spec(g.shape), _const_spec(b.shape)],
        out_specs=pl.BlockSpec((tm, d), lambda i: (i, 0)),
        scratch_shapes=[pltpu.VMEM((tm, d_ff), BF16)],
        compiler_params=pltpu.CompilerParams(dimension_semantics=("arbitrary",),
                                             vmem_limit_bytes=VMEM_LIMIT),
        name="ffn_ln",
    )(x, w_in, w_out, g, b)


_C_NAQ, _C_NAK, _C_NAV = 0, NA_WIDTH, 2 * NA_WIDTH
_C_CQ = 3 * NA_WIDTH
_C_CKV = _C_CQ + MLA_Q_RANK
_C_KR = _C_CKV + MLA_KV_RANK
_C_GATE = _C_KR + 2 * QK_PAD


def _mixer_in_kernel(x_ref, w_ref, bg_ref, qg_ref, kvg_ref, wk_ref, wvt_ref, wqt_ref, wqs_ref,
                     ck_ref, sk_ref, cq_ref, sq_ref,
                     naq_ref, nak_ref, nav_ref, qt_ref, k_ref, vt_ref, ga_ref, gb_ref,
                     qn2_ref, lb_ref, kn2_ref, *, d_model):
    xb = x_ref[...].astype(BF16)
    naq_ref[...] = (_dot(xb, w_ref[:, _C_NAQ:_C_NAK]) * (NA_HEAD_DIM ** -0.5 * LOG2E)).astype(BF16)
    nak_ref[...] = _dot(xb, w_ref[:, _C_NAK:_C_NAV]).astype(BF16)
    nav_ref[...] = _dot(xb, w_ref[:, _C_NAV:_C_CQ]).astype(BF16)

    cqn = _rms_norm(_dot(xb, w_ref[:, _C_CQ:_C_CKV]), qg_ref[...])
    ckvn = _rms_norm(_dot(xb, w_ref[:, _C_CKV:_C_KR]), kvg_ref[...])

    kr = _dot(xb, w_ref[:, _C_KR:_C_GATE])
    kr_blk = kr[:, :QK_PAD] * ck_ref[...] + kr[:, QK_PAD:] * sk_ref[...]
    kall = _dot(ckvn.astype(BF16), wk_ref[...])
    k_probe = []
    one_hot = (lax.broadcasted_iota(jnp.int32, kr_blk.shape, 1) == MLA_QK).astype(F32)
    for h in range(MLA_HEADS):
        kf = kall[:, h * QK_PAD:(h + 1) * QK_PAD] + kr_blk
        kb = (kf + one_hot).astype(BF16)
        k_ref[h] = kb
        k2 = jnp.max(jnp.sum(kf * kf, axis=1, keepdims=True), axis=0, keepdims=True)
        kn2_ref[h] = jnp.broadcast_to(k2, (1, LANES))
        k_probe.append(kb[0:MLA_PROBE_KEYS])

    ckvn_t = ckvn.T.astype(BF16)
    vt = _dot(wvt_ref[...], ckvn_t)
    ones = jnp.ones((MLA_VA - MLA_V, vt.shape[1]), BF16)
    for h in range(MLA_HEADS):
        vt_ref[h, 0:MLA_V] = vt[h * MLA_V:(h + 1) * MLA_V].astype(BF16)
        vt_ref[h, MLA_V:MLA_VA] = ones

    cqn_t = cqn.T.astype(BF16)
    q_scale = (MLA_QK ** -0.5) * LOG2E
    qt = _dot(wqt_ref[...], cqn_t) * q_scale
    qs = _dot(wqs_ref[...], cqn_t) * q_scale
    cq, sq = cq_ref[...], sq_ref[...]
    zeros = jnp.zeros((QK_PAD - MLA_QK, qt.shape[1]), BF16)
    for h in range(MLA_HEADS):
        base = h * QK_PAD
        nope = qt[base:base + MLA_NOPE]
        rope = qt[base + MLA_NOPE:base + MLA_QK] * cq + qs[h * MLA_ROPE:(h + 1) * MLA_ROPE] * sq
        qh = jnp.concatenate([nope.astype(BF16), rope.astype(BF16), zeros], axis=0)
        qt_ref[h] = qh
        qn2_ref[h] = (jnp.sum(nope * nope, axis=0, keepdims=True)
                      + jnp.sum(rope * rope, axis=0, keepdims=True))
        lb_ref[h] = jnp.max(_dot(k_probe[h], qh), axis=0, keepdims=True)

    for half, out_ref in enumerate((ga_ref, gb_ref)):
        lo = _C_GATE + half * d_model
        logits = _dot(xb, w_ref[:, lo:lo + d_model]) + bg_ref[:, half * d_model:(half + 1) * d_model]
        out_ref[...] = jax.nn.sigmoid(logits).astype(BF16)


def mixer_in(x, wts, tabs):
    b, n, d = x.shape
    tm = TOKEN_TILE
    h = MLA_HEADS
    tok = lambda c: pl.BlockSpec((None, tm, c), lambda bi, i: (bi, i, 0))
    out_shape = (
        jax.ShapeDtypeStruct((b, n, NA_WIDTH), BF16), jax.ShapeDtypeStruct((b, n, NA_WIDTH), BF16),
        jax.ShapeDtypeStruct((b, n, NA_WIDTH), BF16),
        jax.ShapeDtypeStruct((b, h, QK_PAD, n), BF16),
        jax.ShapeDtypeStruct((b, h, n, QK_PAD), BF16),
        jax.ShapeDtypeStruct((b, h, MLA_VA, n), BF16),
        jax.ShapeDtypeStruct((b, n, d), BF16), jax.ShapeDtypeStruct((b, n, d), BF16),
        jax.ShapeDtypeStruct((b, h, 1, n), F32), jax.ShapeDtypeStruct((b, h, 1, n), F32),
        jax.ShapeDtypeStruct((b, n // tm, h, 1, LANES), F32),
    )
    stat = pl.BlockSpec((None, h, 1, tm), lambda bi, i: (bi, 0, 0, i))
    out_specs = (
        tok(NA_WIDTH), tok(NA_WIDTH), tok(NA_WIDTH),
        pl.BlockSpec((None, h, QK_PAD, tm), lambda bi, i: (bi, 0, 0, i)),
        pl.BlockSpec((None, h, tm, QK_PAD), lambda bi, i: (bi, 0, i, 0)),
        pl.BlockSpec((None, h, MLA_VA, tm), lambda bi, i: (bi, 0, 0, i)),
        tok(d), tok(d),
        stat, stat,
        pl.BlockSpec((None, None, h, 1, LANES), lambda bi, i: (bi, i, 0, 0, 0)),
    )
    consts = [wts["w_in"], wts["b_gate"], wts["q_norm_g"], wts["kv_norm_g"],
              wts["wk"], wts["wvt"], wts["wqt"], wts["wqs"]]
    in_specs = ([tok(d)] + [_const_spec(c.shape) for c in consts] + [
        pl.BlockSpec((tm, QK_PAD), lambda bi, i: (i, 0)),
        pl.BlockSpec((tm, QK_PAD), lambda bi, i: (i, 0)),
        pl.BlockSpec((MLA_ROPE, tm), lambda bi, i: (0, i)),
        pl.BlockSpec((MLA_ROPE, tm), lambda bi, i: (0, i)),
    ])
    return pl.pallas_call(
        functools.partial(_mixer_in_kernel, d_model=d),
        out_shape=out_shape,
        grid=(b, n // tm),
        in_specs=in_specs,
        out_specs=out_specs,
        compiler_params=pltpu.CompilerParams(dimension_semantics=("arbitrary", "arbitrary"),
                                             vmem_limit_bytes=VMEM_LIMIT),
        name="mixer_in",
    )(x, *consts, tabs["ck"], tabs["sk"], tabs["cq"], tabs["sq"])


def _na_first_key_row(i, rows):
    g = NA_ROWS_PER_STEP
    return jnp.clip(i * g - NA_KH, 0, rows - (g + 2 * NA_KH))


def _na_kernel(q_ref, k_ref, v_ref, bias_ref, o_ref, s0, s1, p0, p1, r0, r1, *, rows):
    g = NA_ROWS_PER_STEP
    window = NA_KH * GRID_W
    npair = NA_HEADS // 2
    i = pl.program_id(1)
    base_row = _na_first_key_row(i, rows)
    kbuf, vbuf = k_ref.at[0], v_ref.at[0]
    lane = lax.broadcasted_iota(jnp.int32, (GRID_W, LANES), 1)
    first_head = lane < NA_HEAD_DIM
    s_bufs, p_bufs, r_bufs = (s0, s1), (p0, p1), (r0, r1)
    lanes_of = lambda hp: slice(hp * LANES, (hp + 1) * LANES)

    def geometry(rho):
        r = i * g + rho
        rs = jnp.clip(r - NA_KH // 2, 0, rows - NA_KH)
        off = pl.multiple_of((rs - base_row) * GRID_W, GRID_W)
        return r - rs, off, pl.multiple_of(rho * GRID_W, GRID_W)

    def scores(rho, slot):
        delta, off, qoff = geometry(rho)
        for hp in range(npair):
            q2 = q_ref[pl.ds(qoff, GRID_W), lanes_of(hp)]
            zero = jnp.zeros_like(q2)
            qs = jnp.concatenate([jnp.where(first_head, q2, zero),
                                  jnp.where(first_head, zero, q2)], axis=0)
            k2 = kbuf[pl.ds(off, window), lanes_of(hp)]
            s = lax.dot_general(qs, k2, (((1,), (1,)), ((), ())), preferred_element_type=F32)
            s_bufs[slot][hp] = s + bias_ref[delta, hp]

    def softmax(slot):
        for hp in range(npair):
            for rows_ in (slice(0, GRID_W), slice(GRID_W, 2 * GRID_W)):
                s = s_bufs[slot][hp, rows_]
                p = jnp.exp2(s - jnp.max(s, axis=1, keepdims=True))
                p_bufs[slot][hp, rows_] = p.astype(BF16)
                r_bufs[slot][hp, rows_] = jnp.broadcast_to(1.0 / jnp.sum(p, axis=1, keepdims=True),
                                                           (GRID_W, LANES))

    def values(rho, slot):
        _, off, qoff = geometry(rho)
        for hp in range(npair):
            o = _dot(p_bufs[slot][hp], vbuf[pl.ds(off, window), lanes_of(hp)])
            o = o * r_bufs[slot][hp]
            out2 = jnp.where(first_head, o[:GRID_W], o[GRID_W:])
            o_ref[pl.ds(qoff, GRID_W), lanes_of(hp)] = out2.astype(BF16)

    def block(t, slot, do_scores=True, do_softmax=True, do_values=True):
        if do_scores:
            scores(t + 1, 1 - slot)
        if do_softmax:
            softmax(slot)
        if do_values:
            values(t - 1, 1 - slot)

    assert g % 2 == 0 and g >= 4
    block(-1, 1, do_softmax=False, do_values=False)
    block(0, 0, do_values=False)

    unroll = NA_BLOCKS_PER_ITER
    assert unroll % 2 == 0 and (g - 2) % unroll == 0

    def blocks(j, carry):
        for kk in range(unroll):
            block(unroll * j + 1 + kk, (1 + kk) % 2)
        return carry

    lax.fori_loop(0, (g - 2) // unroll, blocks, 0)
    block(g - 1, 1, do_scores=False)
    block(g, 0, do_scores=False, do_softmax=False)


def na_attention(q, k, v, bias):
    b, n, c = q.shape
    rows = n // GRID_W
    g = NA_ROWS_PER_STEP
    main = g * GRID_W
    halo = NA_KH * GRID_W
    assert rows >= g + 2 * NA_KH
    spec_main = pl.BlockSpec((None, main, c), lambda bi, i: (bi, i, 0))
    spec_kv = pl.BlockSpec((pl.Element(1), pl.Element(main + 2 * halo), pl.Element(c)),
                           lambda bi, i: (bi, _na_first_key_row(i, rows) * GRID_W, 0))
    return pl.pallas_call(
        functools.partial(_na_kernel, rows=rows),
        out_shape=jax.ShapeDtypeStruct((b, n, c), BF16),
        grid=(b, rows // g),
        in_specs=[spec_main, spec_kv, spec_kv, _const_spec(bias.shape)],
        out_specs=spec_main,
        scratch_shapes=[pltpu.VMEM(bias.shape[1:], F32)] * 2 + [pltpu.VMEM(bias.shape[1:], BF16)] * 2
        + [pltpu.VMEM(bias.shape[1:3] + (LANES,), F32)] * 2,
        compiler_params=pltpu.CompilerParams(dimension_semantics=("arbitrary", "arbitrary"),
                                             vmem_limit_bytes=VMEM_LIMIT),
        name="na_attention",
    )(q, k, v, bias)


def _mla_kernel(qt_ref, k_ref, vt_ref, o_ref, *bufs, n):
    tq, tk, g = MLA_TQ, MLA_TK, MLA_GROUP
    ng = n // (tk * g)
    total = (n // tq) * ng
    nbuf = MLA_NBUF
    assert total >= 4
    s_bufs, p_bufs = bufs[:nbuf], bufs[nbuf:]

    def split(u):
        return lax.div(u, ng), lax.rem(u, ng)

    def scores_chunk(u, slot, c):
        qi, kg = split(u)
        qt = qt_ref[:, pl.ds(pl.multiple_of(qi * tq, tq), tq)]
        off = pl.multiple_of((kg * g + c) * tk, tk)
        s = _dot(k_ref[pl.ds(off, tk), :], qt)
        s_bufs[slot][c * tk:(c + 1) * tk] = s
        return jnp.max(s, axis=0, keepdims=True)

    def first_group_reset(u, m):
        _, kg = split(u)
        return jnp.where(kg == 0, -jnp.inf, m)

    def softmax_chunk(slot, c, m, mx):
        m_new = jnp.maximum(m, mx)
        p_bufs[slot][c * tk:(c + 1) * tk] = jnp.exp2(s_bufs[slot][c * tk:(c + 1) * tk] - m_new).astype(BF16)
        return m_new, jnp.exp2(m - m_new)

    def values_chunk(u, slot, c, r, acc):
        _, kg = split(u)
        off = pl.multiple_of((kg * g + c) * tk, tk)
        return r * acc + _dot(vt_ref[:, pl.ds(off, tk)], p_bufs[slot][c * tk:(c + 1) * tk])

    def write_out(u, acc):
        qi, _ = split(u)
        out = acc[:MLA_V] * (1.0 / acc[MLA_V:MLA_V + 1])
        o_ref[:, pl.ds(pl.multiple_of(qi * tq, tq), tq)] = out.astype(BF16)

    def block(t, slot, carry, do_scores=True, do_softmax=True, do_values=True):
        m, mxs, rs_prev, acc = carry
        if do_softmax:
            m = first_group_reset(t, m)
        mxs_next, rs = list(mxs), list(rs_prev)
        if do_scores:
            for c in range(g):
                mxs_next[c] = scores_chunk(t + 1, (slot + 1) % nbuf, c)
        if do_softmax:
            for c in range(g):
                m, rs[c] = softmax_chunk(slot, c, m, mxs[c])
        if do_values:
            for c in range(g):
                acc = values_chunk(t - 1, (slot - 1) % nbuf, c, rs_prev[c], acc)
            write_out(t - 1, acc)
        return m, tuple(mxs_next), tuple(rs), acc

    row = jnp.zeros((1, tq), F32)
    carry = (row, (row,) * g, (row,) * g, jnp.zeros((MLA_VA, tq), F32))
    carry = block(-1, -1 % nbuf, carry, do_softmax=False, do_values=False)
    carry = block(0, 0, carry, do_values=False)

    unroll = MLA_BLOCKS_PER_ITER
    assert unroll % nbuf == 0
    n_iter, n_rest = divmod(total - 2, unroll)

    def blocks(i, carry):
        t0 = unroll * i + 1
        for kk in range(unroll):
            carry = block(t0 + kk, (1 + kk) % nbuf, carry)
        return carry

    carry = lax.fori_loop(0, n_iter, blocks, carry)
    for t in range(n_iter * unroll + 1, n_iter * unroll + 1 + n_rest):
        carry = block(t, t % nbuf, carry)
    carry = block(total - 1, (total - 1) % nbuf, carry, do_scores=False)
    block(total, total % nbuf, carry, do_scores=False, do_softmax=False)


def _mla_fast_kernel(qt_ref, k_ref, vt_ref, m_ref, o_ref, p0, p1, *, n):
    tq, tk, g = MLA_TQ, MLA_TK, MLA_GROUP
    ng = n // (tk * g)
    total = (n // tq) * ng
    p_bufs = (p0, p1)

    def split(u):
        return lax.div(u, ng), lax.rem(u, ng)

    def probs(u, slot):
        qi, kg = split(u)
        qoff = pl.multiple_of(qi * tq, tq)
        qt = qt_ref[:, pl.ds(qoff, tq)]
        m = m_ref[:, pl.ds(qoff, tq)]
        feat = lax.broadcasted_iota(jnp.int32, (QK_PAD, tq), 0)
        qt = jnp.where(feat == MLA_QK, -m, qt.astype(F32)).astype(BF16)
        for c in range(g):
            off = pl.multiple_of((kg * g + c) * tk, tk)
            s = _dot(k_ref[pl.ds(off, tk), :], qt)
            p_bufs[slot][c * tk:(c + 1) * tk] = jnp.exp2(s).astype(BF16)

    def values(u, slot, acc):
        qi, kg = split(u)
        acc = jnp.where(kg == 0, 0.0, acc)
        for c in range(g):
            off = pl.multiple_of((kg * g + c) * tk, tk)
            acc = acc + _dot(vt_ref[:, pl.ds(off, tk)], p_bufs[slot][c * tk:(c + 1) * tk])
        out = acc[:MLA_V] * (1.0 / acc[MLA_V:MLA_V + 1])
        o_ref[:, pl.ds(pl.multiple_of(qi * tq, tq), tq)] = out.astype(BF16)
        return acc

    def block(t, slot, acc):
        probs(t + 1, 1 - slot)
        return values(t, slot, acc)

    unroll = MLA_BLOCKS_PER_ITER
    assert unroll % 2 == 0 and total >= 2
    probs(0, 0)
    acc = block(0, 0, jnp.zeros((MLA_VA, tq), F32))
    n_iter, n_rest = divmod(total - 2, unroll)

    def blocks(i, acc):
        t0 = unroll * i + 1
        for kk in range(unroll):
            acc = block(t0 + kk, (1 + kk) % 2, acc)
        return acc

    acc = lax.fori_loop(0, n_iter, blocks, acc)
    for t in range(n_iter * unroll + 1, total - 1):
        acc = block(t, t % 2, acc)
    values(total - 1, (total - 1) % 2, acc)


def mla_attention(qt, k, vt, qn2, lb, kn2):
    b, h, _, n = qt.shape
    rows = MLA_GROUP * MLA_TK
    head = lambda r, c: pl.BlockSpec((None, None, r, c), lambda bi, hi: (bi, hi, 0, 0))
    params = pltpu.CompilerParams(dimension_semantics=("arbitrary",) * 2, vmem_limit_bytes=VMEM_LIMIT)
    out_shape = jax.ShapeDtypeStruct((b, h, MLA_V, n), BF16)

    k2max = jnp.max(kn2, axis=(1, 3, 4))
    m = (jnp.sqrt(qn2 * k2max[:, :, None, None]) * MLA_BOUND_INFLATE).astype(BF16).astype(F32)
    gap = m - lb

    def fast(qt, k, vt, m):
        return pl.pallas_call(
            functools.partial(_mla_fast_kernel, n=n),
            out_shape=out_shape,
            grid=(b, h),
            in_specs=[head(QK_PAD, n), head(n, QK_PAD), head(MLA_VA, n), head(1, n)],
            out_specs=head(MLA_V, n),
            scratch_shapes=[pltpu.VMEM((rows, MLA_TQ), BF16)] * 2,
            compiler_params=params,
            name="mla_attention_fast",
        )(qt, k, vt, m)

    def exact(qt, k, vt, m):
        return pl.pallas_call(
            functools.partial(_mla_kernel, n=n),
            out_shape=out_shape,
            grid=(b, h),
            in_specs=[head(QK_PAD, n), head(n, QK_PAD), head(MLA_VA, n)],
            out_specs=head(MLA_V, n),
            scratch_shapes=[pltpu.VMEM((rows, MLA_TQ), F32)] * MLA_NBUF + [pltpu.VMEM((rows, MLA_TQ), BF16)] * MLA_NBUF,
            compiler_params=params,
            name="mla_attention",
        )(qt, k, vt)

    return lax.cond(jnp.max(gap) < MLA_MAX_GAP, fast, exact, qt, k, vt, m)


def _mixer_out_kernel(x_ref, na_ref, at_ref, ga_ref, gb_ref, wna_ref, wmla_ref, wout_ref, g_ref, b_ref,
                      o_ref, mix_ref):
    na = na_ref[...]
    at = at_ref[...].T
    d = o_ref.shape[1]
    for c in range(d // MIX_CHUNK):
        cs = slice(c * MIX_CHUNK, (c + 1) * MIX_CHUNK)
        ya = _dot(na, wna_ref[:, cs])
        yb = _dot(at, wmla_ref[:, cs])
        mix_ref[:, cs] = (ga_ref[:, cs].astype(F32) * ya + gb_ref[:, cs].astype(F32) * yb).astype(BF16)
    y = _dot(mix_ref[...], wout_ref[...])
    o_ref[...] = _layer_norm(ALPHA * x_ref[...] + y, g_ref[...], b_ref[...])


def mixer_out(x, na, at, ga, gb, w_na_o, w_mla_o, w_out, g, bb):
    b, n, d = x.shape
    tm = FFN_TILE
    tok = lambda c: pl.BlockSpec((None, tm, c), lambda bi, i: (bi, i, 0))
    return pl.pallas_call(
        _mixer_out_kernel,
        out_shape=jax.ShapeDtypeStruct((b, n, d), F32),
        grid=(b, n // tm),
        in_specs=[tok(d), tok(NA_WIDTH),
                  pl.BlockSpec((None, at.shape[1], tm), lambda bi, i: (bi, 0, i)),
                  tok(d), tok(d),
                  _const_spec(w_na_o.shape), _const_spec(w_mla_o.shape), _const_spec(w_out.shape),
                  _const_spec(g.shape), _const_spec(bb.shape)],
        out_specs=tok(d),
        scratch_shapes=[pltpu.VMEM((tm, d), BF16)],
        compiler_params=pltpu.CompilerParams(dimension_semantics=("arbitrary", "arbitrary"),
                                             vmem_limit_bytes=VMEM_LIMIT),
        name="mixer_out",
    )(x, na, at, ga, gb, w_na_o, w_mla_o, w_out, g, bb)


def _prep_mixer_weights(w_in, b_gate, q_norm_g, kv_norm_g, w_uq, w_ukv):
    d = w_in.shape[0]
    half = MLA_ROPE // 2
    swap = lambda w: jnp.concatenate([-w[..., half:], w[..., :half]], axis=-1)
    pad_rope = lambda w: jnp.pad(w, ((0, 0), (MLA_NOPE, QK_PAD - MLA_QK)))
    c_kr = 3 * NA_WIDTH + MLA_Q_RANK + MLA_KV_RANK
    w_kr = w_in[:, c_kr:c_kr + MLA_ROPE]
    w_packed = jnp.concatenate([w_in[:, :c_kr], pad_rope(w_kr), pad_rope(swap(w_kr)),
                                w_in[:, c_kr + MLA_ROPE:]], axis=1)
    ukv = w_ukv.reshape(MLA_KV_RANK, MLA_HEADS, MLA_NOPE + MLA_V)
    wk = jnp.pad(ukv[..., :MLA_NOPE], ((0, 0), (0, 0), (0, QK_PAD - MLA_NOPE)))
    wk = wk.reshape(MLA_KV_RANK, MLA_HEADS * QK_PAD)
    wvt = ukv[..., MLA_NOPE:].reshape(MLA_KV_RANK, MLA_HEADS * MLA_V).T
    uq = w_uq.reshape(MLA_Q_RANK, MLA_HEADS, MLA_QK)
    wqt = jnp.pad(uq, ((0, 0), (0, 0), (0, QK_PAD - MLA_QK))).reshape(MLA_Q_RANK, MLA_HEADS * QK_PAD).T
    wqs = swap(uq[..., MLA_NOPE:]).reshape(MLA_Q_RANK, MLA_HEADS * MLA_ROPE).T
    return {
        "w_in": w_packed.astype(BF16), "b_gate": b_gate.reshape(1, 2 * d),
        "q_norm_g": q_norm_g.reshape(1, -1), "kv_norm_g": kv_norm_g.reshape(1, -1),
        "wk": wk.astype(BF16), "wvt": wvt.astype(BF16), "wqt": wqt.astype(BF16), "wqs": wqs.astype(BF16),
    }


def _rope_tables(n):
    inv = 1.0 / (ROPE_BASE ** (jnp.arange(0, MLA_ROPE, 2, dtype=F32) / MLA_ROPE))
    ang = jnp.arange(n, dtype=F32)[:, None] * inv[None, :]
    cos2 = jnp.tile(jnp.cos(ang), (1, 2))
    sin2 = jnp.tile(jnp.sin(ang), (1, 2))
    pad = ((0, 0), (MLA_NOPE, QK_PAD - MLA_QK))
    return {"ck": jnp.pad(cos2, pad), "sk": jnp.pad(sin2, pad), "cq": cos2.T, "sq": sin2.T}


def _na_bias_table(rpb):
    qc = np.arange(GRID_W)[:, None]
    kc = np.arange(GRID_W)[None, :]
    dc = np.clip(kc - qc + NA_KW - 1, 0, 2 * NA_KW - 2)
    onehot = (dc[None] == np.arange(2 * NA_KW - 1)[:, None, None]).astype(np.float32)
    win = np.clip(qc - NA_KW // 2, 0, GRID_W - NA_KW)
    in_win = (kc >= win) & (kc < win + NA_KW)
    t = jnp.einsum("hrc,cqk->hqrk", rpb * LOG2E, jnp.asarray(onehot), precision=lax.Precision.HIGHEST)
    t = jnp.where(in_win[:, None, :], t, NEG_BIG)
    bias = jnp.stack([t[:, :, NA_KH - 1 - dl:2 * NA_KH - 1 - dl] for dl in range(NA_KH)])
    return bias.reshape(NA_KH, NA_HEADS // 2, 2 * GRID_W, NA_KH * GRID_W)


def _encoder_layer(x, p, tabs):
    b, n, d = x.shape
    x1 = ffn_ln(x.reshape(b * n, d), p["ffn1_w_in"], p["ffn1_w_out"], p["ln1_g"], p["ln1_b"])
    x1 = x1.reshape(b, n, d)
    naq, nak, nav, qt, k, vt, ga, gb, qn2, lb, kn2 = mixer_in(x1, p["mixer"], tabs)
    na = na_attention(naq, nak, nav, p["na_bias"])
    at = mla_attention(qt, k, vt, qn2, lb, kn2).reshape(b, MLA_HEADS * MLA_V, n)
    x2 = mixer_out(x1, na, at, ga, gb, p["w_na_o"], p["w_mla_o"], p["w_out"], p["ln2_g"], p["ln2_b"])
    y = ffn_ln(x2.reshape(b * n, d), p["ffn2_w_in"], p["ffn2_w_out"], p["ln3_g"], p["ln3_b"])
    return y.reshape(b, n, d)


def kernel(x_prompt, x_sample, ffn1_w_in, ffn1_w_out, ln1_g, ln1_b, w_in, b_gate, na_rpb, q_norm_g, kv_norm_g, w_uq, w_ukv, w_na_o, w_mla_o, w_out, ln2_g, ln2_b, ffn2_w_in, ffn2_w_out, ln3_g, ln3_b):
    l = 0
    row = lambda a: a[l].reshape(1, -1)
    p = {
        "ffn1_w_in": ffn1_w_in[l].astype(BF16), "ffn1_w_out": ffn1_w_out[l].astype(BF16),
        "ln1_g": row(ln1_g), "ln1_b": row(ln1_b),
        "mixer": _prep_mixer_weights(w_in[l], b_gate[l], q_norm_g[l], kv_norm_g[l], w_uq[l], w_ukv[l]),
        "na_bias": _na_bias_table(na_rpb[l]),
        "w_na_o": w_na_o[l].astype(BF16), "w_mla_o": w_mla_o[l].astype(BF16), "w_out": w_out[l].astype(BF16),
        "ln2_g": row(ln2_g), "ln2_b": row(ln2_b),
        "ffn2_w_in": ffn2_w_in[l].astype(BF16), "ffn2_w_out": ffn2_w_out[l].astype(BF16),
        "ln3_g": row(ln3_g), "ln3_b": row(ln3_b),
    }
    outs = []
    for x in (x_prompt, x_sample):
        outs.append(_encoder_layer(x, p, _rope_tables(x.shape[1])))
    return tuple(outs)
```

```python
import functools
import math

import numpy as np
import jax
import jax.numpy as jnp
from jax import lax
from jax.experimental import pallas as pl
from jax.experimental.pallas import tpu as pltpu

F32 = jnp.float32
BF16 = jnp.bfloat16

DEPTH = 1
GRID_W = 64
NA_HEADS = 8
NA_HEAD_DIM = 64
NA_WIDTH = NA_HEADS * NA_HEAD_DIM
NA_KH = 8
NA_KW = 16
MLA_HEADS = 8
MLA_NOPE = 64
MLA_ROPE = 32
MLA_QK = MLA_NOPE + MLA_ROPE
MLA_V = 64
MLA_VA = MLA_V + 16
MLA_Q_RANK = 384
MLA_KV_RANK = 256
ROPE_BASE = 10000.0
LN_EPS = 1e-5
RMS_EPS = 1e-6
ALPHA = (2.0 * DEPTH) ** 0.25
LOG2E = math.log2(math.e)

LANES = 128
QK_PAD = 128
VMEM_LIMIT = 56 * 1024 * 1024

TOKEN_TILE = 512
FFN_TILE = 1024
FFN_CHUNK = 256
MIX_CHUNK = 256
NA_ROWS_PER_STEP = 32
NA_BLOCKS_PER_ITER = 6
MLA_TQ = 512
MLA_TK = 256
MLA_GROUP = 4
MLA_NBUF = 2
MLA_BLOCKS_PER_ITER = 10
MLA_PROBE_KEYS = 16
MLA_BOUND_INFLATE = 1.02
MLA_MAX_GAP = 64.0
NEG_BIG = -1e30


def _const_spec(shape):
    nd = len(shape)
    return pl.BlockSpec(shape, lambda *_: (0,) * nd, pipeline_mode=pl.Buffered(1))


def _layer_norm(y, g, b):
    mu = jnp.mean(y, axis=-1, keepdims=True)
    d = y - mu
    var = jnp.mean(d * d, axis=-1, keepdims=True)
    return d * lax.rsqrt(var + LN_EPS) * g + b


def _rms_norm(y, g):
    return y * lax.rsqrt(jnp.mean(y * y, axis=-1, keepdims=True) + RMS_EPS) * g


def _dot(a, b):
    return jnp.dot(a, b, preferred_element_type=F32)


def _swiglu_ln(x, w_in_ref, w_out_ref, g_ref, b_ref, h_ref, d_ff):
    xb = x.astype(BF16)
    for c in range(d_ff // FFN_CHUNK):
        lo, hi = c * FFN_CHUNK, (c + 1) * FFN_CHUNK
        a = _dot(xb, w_in_ref[:, lo:hi])
        u = _dot(xb, w_in_ref[:, d_ff + lo:d_ff + hi])
        h_ref[:, lo:hi] = (a * jax.nn.sigmoid(a) * u).astype(BF16)
    y = _dot(h_ref[...], w_out_ref[...])
    return _layer_norm(ALPHA * x + 0.5 * y, g_ref[...], b_ref[...])


def _ffn_ln_kernel(x_ref, w_in_ref, w_out_ref, g_ref, b_ref, o_ref, h_ref, *, d_ff):
    o_ref[...] = _swiglu_ln(x_ref[...], w_in_ref, w_out_ref, g_ref, b_ref, h_ref, d_ff)


def ffn_ln(x, w_in, w_out, g, b):
    t, d = x.shape
    d_ff = w_out.shape[0]
    tm = FFN_TILE
    return pl.pallas_call(
        functools.partial(_ffn_ln_kernel, d_ff=d_ff),
        out_shape=jax.ShapeDtypeStruct((t, d), F32),
        grid=(t // tm,),
        in_specs=[pl.BlockSpec((tm, d), lambda i: (i, 0)),
                  _const_spec(w_in.shape), _const_spec(w_out.shape),
                  _const_spec(g.shape), _const_spec(b.shape)],
        out_specs=pl.BlockSpec((tm, d), lambda i: (i, 0)),
        scratch_shapes=[pltpu.VMEM((tm, d_ff), BF16)],
        compiler_params=pltpu.CompilerParams(dimension_semantics=("arbitrary",),
                                             vmem_limit_bytes=VMEM_LIMIT),
        name="ffn_ln",
    )(x, w_in, w_out, g, b)


_C_NAQ, _C_NAK, _C_NAV = 0, NA_WIDTH, 2 * NA_WIDTH
_C_CQ = 3 * NA_WIDTH
_C_CKV = _C_CQ + MLA_Q_RANK
_C_KR = _C_CKV + MLA_KV_RANK
_C_GATE = _C_KR + 2 * QK_PAD


def _mixer_in_kernel(x_ref, w_ref, bg_ref, qg_ref, kvg_ref, wk_ref, wvt_ref, wqt_ref, wqs_ref,
                     ck_ref, sk_ref, cq_ref, sq_ref,
                     naq_ref, nak_ref, nav_ref, qt_ref, k_ref, vt_ref, ga_ref, gb_ref,
                     qn2_ref, lb_ref, kn2_ref, *, d_model):
    xb = x_ref[...].astype(BF16)
    naq_ref[...] = (_dot(xb, w_ref[:, _C_NAQ:_C_NAK]) * (NA_HEAD_DIM ** -0.5 * LOG2E)).astype(BF16)
    nak_ref[...] = _dot(xb, w_ref[:, _C_NAK:_C_NAV]).astype(BF16)
    nav_ref[...] = _dot(xb, w_ref[:, _C_NAV:_C_CQ]).astype(BF16)

    cqn = _rms_norm(_dot(xb, w_ref[:, _C_CQ:_C_CKV]), qg_ref[...])
    ckvn = _rms_norm(_dot(xb, w_ref[:, _C_CKV:_C_KR]), kvg_ref[...])

    kr = _dot(xb, w_ref[:, _C_KR:_C_GATE])
    kr_blk = kr[:, :QK_PAD] * ck_ref[...] + kr[:, QK_PAD:] * sk_ref[...]
    kall = _dot(ckvn.astype(BF16), wk_ref[...])
    k_probe = []
    one_hot = (lax.broadcasted_iota(jnp.int32, kr_blk.shape, 1) == MLA_QK).astype(F32)
    for h in range(MLA_HEADS):
        kf = kall[:, h * QK_PAD:(h + 1) * QK_PAD] + kr_blk
        kb = (kf + one_hot).astype(BF16)
        k_ref[h] = kb
        k2 = jnp.max(jnp.sum(kf * kf, axis=1, keepdims=True), axis=0, keepdims=True)
        kn2_ref[h] = jnp.broadcast_to(k2, (1, LANES))
        k_probe.append(kb[0:MLA_PROBE_KEYS])

    ckvn_t = ckvn.T.astype(BF16)
    vt = _dot(wvt_ref[...], ckvn_t)
    ones = jnp.ones((MLA_VA - MLA_V, vt.shape[1]), BF16)
    for h in range(MLA_HEADS):
        vt_ref[h, 0:MLA_V] = vt[h * MLA_V:(h + 1) * MLA_V].astype(BF16)
        vt_ref[h, MLA_V:MLA_VA] = ones

    cqn_t = cqn.T.astype(BF16)
    q_scale = (MLA_QK ** -0.5) * LOG2E
    qt = _dot(wqt_ref[...], cqn_t) * q_scale
    qs = _dot(wqs_ref[...], cqn_t) * q_scale
    cq, sq = cq_ref[...], sq_ref[...]
    zeros = jnp.zeros((QK_PAD - MLA_QK, qt.shape[1]), BF16)
    for h in range(MLA_HEADS):
        base = h * QK_PAD
        nope = qt[base:base + MLA_NOPE]
        rope = qt[base + MLA_NOPE:base + MLA_QK] * cq + qs[h * MLA_ROPE:(h + 1) * MLA_ROPE] * sq
        qh = jnp.concatenate([nope.astype(BF16), rope.astype(BF16), zeros], axis=0)
        qt_ref[h] = qh
        qn2_ref[h] = (jnp.sum(nope * nope, axis=0, keepdims=True)
                      + jnp.sum(rope * rope, axis=0, keepdims=True))
        lb_ref[h] = jnp.max(_dot(k_probe[h], qh), axis=0, keepdims=True)

    for half, out_ref in enumerate((ga_ref, gb_ref)):
        lo = _C_GATE + half * d_model
        logits = _dot(xb, w_ref[:, lo:lo + d_model]) + bg_ref[:, half * d_model:(half + 1) * d_model]
        out_ref[...] = jax.nn.sigmoid(logits).astype(BF16)


def mixer_in(x, wts, tabs):
    b, n, d = x.shape
    tm = TOKEN_TILE
    h = MLA_HEADS
    tok = lambda c: pl.BlockSpec((None, tm, c), lambda bi, i: (bi, i, 0))
    out_shape = (
        jax.ShapeDtypeStruct((b, n, NA_WIDTH), BF16), jax.ShapeDtypeStruct((b, n, NA_WIDTH), BF16),
        jax.ShapeDtypeStruct((b, n, NA_WIDTH), BF16),
        jax.ShapeDtypeStruct((b, h, QK_PAD, n), BF16),
        jax.ShapeDtypeStruct((b, h, n, QK_PAD), BF16),
        jax.ShapeDtypeStruct((b, h, MLA_VA, n), BF16),
        jax.ShapeDtypeStruct((b, n, d), BF16), jax.ShapeDtypeStruct((b, n, d), BF16),
        jax.ShapeDtypeStruct((b, h, 1, n), F32), jax.ShapeDtypeStruct((b, h, 1, n), F32),
        jax.ShapeDtypeStruct((b, n // tm, h, 1, LANES), F32),
    )
    stat = pl.BlockSpec((None, h, 1, tm), lambda bi, i: (bi, 0, 0, i))
    out_specs = (
        tok(NA_WIDTH), tok(NA_WIDTH), tok(NA_WIDTH),
        pl.BlockSpec((None, h, QK_PAD, tm), lambda bi, i: (bi, 0, 0, i)),
        pl.BlockSpec((None, h, tm, QK_PAD), lambda bi, i: (bi, 0, i, 0)),
        pl.BlockSpec((None, h, MLA_VA, tm), lambda bi, i: (bi, 0, 0, i)),
        tok(d), tok(d),
        stat, stat,
        pl.BlockSpec((None, None, h, 1, LANES), lambda bi, i: (bi, i, 0, 0, 0)),
    )
    consts = [wts["w_in"], wts["b_gate"], wts["q_norm_g"], wts["kv_norm_g"],
              wts["wk"], wts["wvt"], wts["wqt"], wts["wqs"]]
    in_specs = ([tok(d)] + [_const_spec(c.shape) for c in consts] + [
        pl.BlockSpec((tm, QK_PAD), lambda bi, i: (i, 0)),
        pl.BlockSpec((tm, QK_PAD), lambda bi, i: (i, 0)),
        pl.BlockSpec((MLA_ROPE, tm), lambda bi, i: (0, i)),
        pl.BlockSpec((MLA_ROPE, tm), lambda bi, i: (0, i)),
    ])
    return pl.pallas_call(
        functools.partial(_mixer_in_kernel, d_model=d),
        out_shape=out_shape,
        grid=(b, n // tm),
        in_specs=in_specs,
        out_specs=out_specs,
        compiler_params=pltpu.CompilerParams(dimension_semantics=("arbitrary", "arbitrary"),
                                             vmem_limit_bytes=VMEM_LIMIT),
        name="mixer_in",
    )(x, *consts, tabs["ck"], tabs["sk"], tabs["cq"], tabs["sq"])


def _na_first_key_row(i, rows):
    g = NA_ROWS_PER_STEP
    return jnp.clip(i * g - NA_KH, 0, rows - (g + 2 * NA_KH))


def _na_kernel(q_ref, k_ref, v_ref, bias_ref, o_ref, s0, s1, p0, p1, r0, r1, *, rows):
    g = NA_ROWS_PER_STEP
    window = NA_KH * GRID_W
    npair = NA_HEADS // 2
    i = pl.program_id(1)
    base_row = _na_first_key_row(i, rows)
    kbuf, vbuf = k_ref.at[0], v_ref.at[0]
    lane = lax.broadcasted_iota(jnp.int32, (GRID_W, LANES), 1)
    first_head = lane < NA_HEAD_DIM
    s_bufs, p_bufs, r_bufs = (s0, s1), (p0, p1), (r0, r1)
    lanes_of = lambda hp: slice(hp * LANES, (hp + 1) * LANES)

    def geometry(rho):
        r = i * g + rho
        rs = jnp.clip(r - NA_KH // 2, 0, rows - NA_KH)
        off = pl.multiple_of((rs - base_row) * GRID_W, GRID_W)
        return r - rs, off, pl.multiple_of(rho * GRID_W, GRID_W)

    def scores(rho, slot):
        delta, off, qoff = geometry(rho)
        for hp in range(npair):
            q2 = q_ref[pl.ds(qoff, GRID_W), lanes_of(hp)]
            zero = jnp.zeros_like(q2)
            qs = jnp.concatenate([jnp.where(first_head, q2, zero),
                                  jnp.where(first_head, zero, q2)], axis=0)
            k2 = kbuf[pl.ds(off, window), lanes_of(hp)]
            s = lax.dot_general(qs, k2, (((1,), (1,)), ((), ())), preferred_element_type=F32)
            s_bufs[slot][hp] = s + bias_ref[delta, hp]

    def softmax(slot):
        for hp in range(npair):
            for rows_ in (slice(0, GRID_W), slice(GRID_W, 2 * GRID_W)):
                s = s_bufs[slot][hp, rows_]
                p = jnp.exp2(s - jnp.max(s, axis=1, keepdims=True))
                p_bufs[slot][hp, rows_] = p.astype(BF16)
                r_bufs[slot][hp, rows_] = jnp.broadcast_to(1.0 / jnp.sum(p, axis=1, keepdims=True),
                                                           (GRID_W, LANES))

    def values(rho, slot):
        _, off, qoff = geometry(rho)
        for hp in range(npair):
            o = _dot(p_bufs[slot][hp], vbuf[pl.ds(off, window), lanes_of(hp)])
            o = o * r_bufs[slot][hp]
            out2 = jnp.where(first_head, o[:GRID_W], o[GRID_W:])
            o_ref[pl.ds(qoff, GRID_W), lanes_of(hp)] = out2.astype(BF16)

    def block(t, slot, do_scores=True, do_softmax=True, do_values=True):
        if do_scores:
            scores(t + 1, 1 - slot)
        if do_softmax:
            softmax(slot)
        if do_values:
            values(t - 1, 1 - slot)

    assert g % 2 == 0 and g >= 4
    block(-1, 1, do_softmax=False, do_values=False)
    block(0, 0, do_values=False)

    unroll = NA_BLOCKS_PER_ITER
    assert unroll % 2 == 0 and (g - 2) % unroll == 0

    def blocks(j, carry):
        for kk in range(unroll):
            block(unroll * j + 1 + kk, (1 + kk) % 2)
        return carry

    lax.fori_loop(0, (g - 2) // unroll, blocks, 0)
    block(g - 1, 1, do_scores=False)
    block(g, 0, do_scores=False, do_softmax=False)


def na_attention(q, k, v, bias):
    b, n, c = q.shape
    rows = n // GRID_W
    g = NA_ROWS_PER_STEP
    main = g * GRID_W
    halo = NA_KH * GRID_W
    assert rows >= g + 2 * NA_KH
    spec_main = pl.BlockSpec((None, main, c), lambda bi, i: (bi, i, 0))
    spec_kv = pl.BlockSpec((pl.Element(1), pl.Element(main + 2 * halo), pl.Element(c)),
                           lambda bi, i: (bi, _na_first_key_row(i, rows) * GRID_W, 0))
    return pl.pallas_call(
        functools.partial(_na_kernel, rows=rows),
        out_shape=jax.ShapeDtypeStruct((b, n, c), BF16),
        grid=(b, rows // g),
        in_specs=[spec_main, spec_kv, spec_kv, _const_spec(bias.shape)],
        out_specs=spec_main,
        scratch_shapes=[pltpu.VMEM(bias.shape[1:], F32)] * 2 + [pltpu.VMEM(bias.shape[1:], BF16)] * 2
        + [pltpu.VMEM(bias.shape[1:3] + (LANES,), F32)] * 2,
        compiler_params=pltpu.CompilerParams(dimension_semantics=("arbitrary", "arbitrary"),
                                             vmem_limit_bytes=VMEM_LIMIT),
        name="na_attention",
    )(q, k, v, bias)


def _mla_kernel(qt_ref, k_ref, vt_ref, o_ref, *bufs, n):
    tq, tk, g = MLA_TQ, MLA_TK, MLA_GROUP
    ng = n // (tk * g)
    total = (n // tq) * ng
    nbuf = MLA_NBUF
    assert total >= 4
    s_bufs, p_bufs = bufs[:nbuf], bufs[nbuf:]

    def split(u):
        return lax.div(u, ng), lax.rem(u, ng)

    def scores_chunk(u, slot, c):
        qi, kg = split(u)
        qt = qt_ref[:, pl.ds(pl.multiple_of(qi * tq, tq), tq)]
        off = pl.multiple_of((kg * g + c) * tk, tk)
        s = _dot(k_ref[pl.ds(off, tk), :], qt)
        s_bufs[slot][c * tk:(c + 1) * tk] = s
        return jnp.max(s, axis=0, keepdims=True)

    def first_group_reset(u, m):
        _, kg = split(u)
        return jnp.where(kg == 0, -jnp.inf, m)

    def softmax_chunk(slot, c, m, mx):
        m_new = jnp.maximum(m, mx)
        p_bufs[slot][c * tk:(c + 1) * tk] = jnp.exp2(s_bufs[slot][c * tk:(c + 1) * tk] - m_new).astype(BF16)
        return m_new, jnp.exp2(m - m_new)

    def values_chunk(u, slot, c, r, acc):
        _, kg = split(u)
        off = pl.multiple_of((kg * g + c) * tk, tk)
        return r * acc + _dot(vt_ref[:, pl.ds(off, tk)], p_bufs[slot][c * tk:(c + 1) * tk])

    def write_out(u, acc):
        qi, _ = split(u)
        out = acc[:MLA_V] * (1.0 / acc[MLA_V:MLA_V + 1])
        o_ref[:, pl.ds(pl.multiple_of(qi * tq, tq), tq)] = out.astype(BF16)

    def block(t, slot, carry, do_scores=True, do_softmax=True, do_values=True):
        m, mxs, rs_prev, acc = carry
        if do_softmax:
            m = first_group_reset(t, m)
        mxs_next, rs = list(mxs), list(rs_prev)
        if do_scores:
            for c in range(g):
                mxs_next[c] = scores_chunk(t + 1, (slot + 1) % nbuf, c)
        if do_softmax:
            for c in range(g):
                m, rs[c] = softmax_chunk(slot, c, m, mxs[c])
        if do_values:
            for c in range(g):
                acc = values_chunk(t - 1, (slot - 1) % nbuf, c, rs_prev[c], acc)
            write_out(t - 1, acc)
        return m, tuple(mxs_next), tuple(rs), acc

    row = jnp.zeros((1, tq), F32)
    carry = (row, (row,) * g, (row,) * g, jnp.zeros((MLA_VA, tq), F32))
    carry = block(-1, -1 % nbuf, carry, do_softmax=False, do_values=False)
    carry = block(0, 0, carry, do_values=False)

    unroll = MLA_BLOCKS_PER_ITER
    assert unroll % nbuf == 0
    n_iter, n_rest = divmod(total - 2, unroll)

    def blocks(i, carry):
        t0 = unroll * i + 1
        for kk in range(unroll):
            carry = block(t0 + kk, (1 + kk) % nbuf, carry)
        return carry

    carry = lax.fori_loop(0, n_iter, blocks, carry)
    for t in range(n_iter * unroll + 1, n_iter * unroll + 1 + n_rest):
        carry = block(t, t % nbuf, carry)
    carry = block(total - 1, (total - 1) % nbuf, carry, do_scores=False)
    block(total, total % nbuf, carry, do_scores=False, do_softmax=False)


def _mla_fast_kernel(qt_ref, k_ref, vt_ref, m_ref, o_ref, p0, p1, *, n):
    tq, tk, g = MLA_TQ, MLA_TK, MLA_GROUP
    ng = n // (tk * g)
    total = (n // tq) * ng
    p_bufs = (p0, p1)

    def split(u):
        return lax.div(u, ng), lax.rem(u, ng)

    def probs(u, slot, l):
        qi, kg = split(u)
        qoff = pl.multiple_of(qi * tq, tq)
        qt = qt_ref[:, pl.ds(qoff, tq)]
        m = m_ref[:, pl.ds(qoff, tq)]
        feat = lax.broadcasted_iota(jnp.int32, (QK_PAD, tq), 0)
        qt = jnp.where(feat == MLA_QK, -m, qt.astype(F32)).astype(BF16)
        l = jnp.where(kg == 0, 0.0, l)
        for c in range(g):
            off = pl.multiple_of((kg * g + c) * tk, tk)
            p = jnp.exp2(_dot(k_ref[pl.ds(off, tk), :], qt))
            p_bufs[slot][c * tk:(c + 1) * tk] = p.astype(BF16)
            l = l + jnp.sum(p, axis=0, keepdims=True)
        return l

    def values(u, slot, acc, l):
        qi, kg = split(u)
        acc = jnp.where(kg == 0, 0.0, acc)
        for c in range(g):
            off = pl.multiple_of((kg * g + c) * tk, tk)
            acc = acc + _dot(vt_ref[0:MLA_V, pl.ds(off, tk)], p_bufs[slot][c * tk:(c + 1) * tk])
        o_ref[:, pl.ds(pl.multiple_of(qi * tq, tq), tq)] = (acc * (1.0 / l)).astype(BF16)
        return acc

    def block(t, slot, carry):
        acc, l = carry
        l_next = probs(t + 1, 1 - slot, l)
        return values(t, slot, acc, l), l_next

    unroll = MLA_BLOCKS_PER_ITER
    assert unroll % 2 == 0 and total >= 2
    carry = (jnp.zeros((MLA_V, tq), F32), probs(0, 0, jnp.zeros((1, tq), F32)))
    carry = block(0, 0, carry)
    n_iter, n_rest = divmod(total - 2, unroll)

    def blocks(i, carry):
        t0 = unroll * i + 1
        for kk in range(unroll):
            carry = block(t0 + kk, (1 + kk) % 2, carry)
        return carry

    carry = lax.fori_loop(0, n_iter, blocks, carry)
    for t in range(n_iter * unroll + 1, total - 1):
        carry = block(t, t % 2, carry)
    values(total - 1, (total - 1) % 2, *carry)


def mla_attention(qt, k, vt, qn2, lb, kn2):
    b, h, _, n = qt.shape
    rows = MLA_GROUP * MLA_TK
    head = lambda r, c: pl.BlockSpec((None, None, r, c), lambda bi, hi: (bi, hi, 0, 0))
    params = pltpu.CompilerParams(dimension_semantics=("arbitrary",) * 2, vmem_limit_bytes=VMEM_LIMIT)
    out_shape = jax.ShapeDtypeStruct((b, h, MLA_V, n), BF16)

    k2max = jnp.max(kn2, axis=(1, 3, 4))
    m = (jnp.sqrt(qn2 * k2max[:, :, None, None]) * MLA_BOUND_INFLATE).astype(BF16).astype(F32)
    gap = m - lb

    def fast(qt, k, vt, m):
        return pl.pallas_call(
            functools.partial(_mla_fast_kernel, n=n),
            out_shape=out_shape,
            grid=(b, h),
            in_specs=[head(QK_PAD, n), head(n, QK_PAD), head(MLA_VA, n), head(1, n)],
            out_specs=head(MLA_V, n),
            scratch_shapes=[pltpu.VMEM((rows, MLA_TQ), BF16)] * 2,
            compiler_params=params,
            name="mla_attention_fast",
        )(qt, k, vt, m)

    def exact(qt, k, vt, m):
        return pl.pallas_call(
            functools.partial(_mla_kernel, n=n),
            out_shape=out_shape,
            grid=(b, h),
            in_specs=[head(QK_PAD, n), head(n, QK_PAD), head(MLA_VA, n)],
            out_specs=head(MLA_V, n),
            scratch_shapes=[pltpu.VMEM((rows, MLA_TQ), F32)] * MLA_NBUF + [pltpu.VMEM((rows, MLA_TQ), BF16)] * MLA_NBUF,
            compiler_params=params,
            name="mla_attention",
        )(qt, k, vt)

    return lax.cond(jnp.max(gap) < MLA_MAX_GAP, fast, exact, qt, k, vt, m)


def _mixer_out_kernel(x_ref, na_ref, at_ref, ga_ref, gb_ref, wna_ref, wmla_ref, wout_ref, g_ref, b_ref,
                      o_ref, mix_ref):
    na = na_ref[...]
    at = at_ref[...].T
    d = o_ref.shape[1]
    for c in range(d // MIX_CHUNK):
        cs = slice(c * MIX_CHUNK, (c + 1) * MIX_CHUNK)
        ya = _dot(na, wna_ref[:, cs])
        yb = _dot(at, wmla_ref[:, cs])
        mix_ref[:, cs] = (ga_ref[:, cs].astype(F32) * ya + gb_ref[:, cs].astype(F32) * yb).astype(BF16)
    y = _dot(mix_ref[...], wout_ref[...])
    o_ref[...] = _layer_norm(ALPHA * x_ref[...] + y, g_ref[...], b_ref[...])


def mixer_out(x, na, at, ga, gb, w_na_o, w_mla_o, w_out, g, bb):
    b, n, d = x.shape
    tm = FFN_TILE
    tok = lambda c: pl.BlockSpec((None, tm, c), lambda bi, i: (bi, i, 0))
    return pl.pallas_call(
        _mixer_out_kernel,
        out_shape=jax.ShapeDtypeStruct((b, n, d), F32),
        grid=(b, n // tm),
        in_specs=[tok(d), tok(NA_WIDTH),
                  pl.BlockSpec((None, at.shape[1], tm), lambda bi, i: (bi, 0, i)),
                  tok(d), tok(d),
                  _const_spec(w_na_o.shape), _const_spec(w_mla_o.shape), _const_spec(w_out.shape),
                  _const_spec(g.shape), _const_spec(bb.shape)],
        out_specs=tok(d),
        scratch_shapes=[pltpu.VMEM((tm, d), BF16)],
        compiler_params=pltpu.CompilerParams(dimension_semantics=("arbitrary", "arbitrary"),
                                             vmem_limit_bytes=VMEM_LIMIT),
        name="mixer_out",
    )(x, na, at, ga, gb, w_na_o, w_mla_o, w_out, g, bb)


def _prep_mixer_weights(w_in, b_gate, q_norm_g, kv_norm_g, w_uq, w_ukv):
    d = w_in.shape[0]
    half = MLA_ROPE // 2
    swap = lambda w: jnp.concatenate([-w[..., half:], w[..., :half]], axis=-1)
    pad_rope = lambda w: jnp.pad(w, ((0, 0), (MLA_NOPE, QK_PAD - MLA_QK)))
    c_kr = 3 * NA_WIDTH + MLA_Q_RANK + MLA_KV_RANK
    w_kr = w_in[:, c_kr:c_kr + MLA_ROPE]
    w_packed = jnp.concatenate([w_in[:, :c_kr], pad_rope(w_kr), pad_rope(swap(w_kr)),
                                w_in[:, c_kr + MLA_ROPE:]], axis=1)
    ukv = w_ukv.reshape(MLA_KV_RANK, MLA_HEADS, MLA_NOPE + MLA_V)
    wk = jnp.pad(ukv[..., :MLA_NOPE], ((0, 0), (0, 0), (0, QK_PAD - MLA_NOPE)))
    wk = wk.reshape(MLA_KV_RANK, MLA_HEADS * QK_PAD)
    wvt = ukv[..., MLA_NOPE:].reshape(MLA_KV_RANK, MLA_HEADS * MLA_V).T
    uq = w_uq.reshape(MLA_Q_RANK, MLA_HEADS, MLA_QK)
    wqt = jnp.pad(uq, ((0, 0), (0, 0), (0, QK_PAD - MLA_QK))).reshape(MLA_Q_RANK, MLA_HEADS * QK_PAD).T
    wqs = swap(uq[..., MLA_NOPE:]).reshape(MLA_Q_RANK, MLA_HEADS * MLA_ROPE).T
    return {
        "w_in": w_packed.astype(BF16), "b_gate": b_gate.reshape(1, 2 * d),
        "q_norm_g": q_norm_g.reshape(1, -1), "kv_norm_g": kv_norm_g.reshape(1, -1),
        "wk": wk.astype(BF16), "wvt": wvt.astype(BF16), "wqt": wqt.astype(BF16), "wqs": wqs.astype(BF16),
    }


def _rope_tables(n):
    inv = 1.0 / (ROPE_BASE ** (jnp.arange(0, MLA_ROPE, 2, dtype=F32) / MLA_ROPE))
    ang = jnp.arange(n, dtype=F32)[:, None] * inv[None, :]
    cos2 = jnp.tile(jnp.cos(ang), (1, 2))
    sin2 = jnp.tile(jnp.sin(ang), (1, 2))
    pad = ((0, 0), (MLA_NOPE, QK_PAD - MLA_QK))
    return {"ck": jnp.pad(cos2, pad), "sk": jnp.pad(sin2, pad), "cq": cos2.T, "sq": sin2.T}


def _na_bias_table(rpb):
    qc = np.arange(GRID_W)[:, None]
    kc = np.arange(GRID_W)[None, :]
    dc = np.clip(kc - qc + NA_KW - 1, 0, 2 * NA_KW - 2)
    onehot = (dc[None] == np.arange(2 * NA_KW - 1)[:, None, None]).astype(np.float32)
    win = np.clip(qc - NA_KW // 2, 0, GRID_W - NA_KW)
    in_win = (kc >= win) & (kc < win + NA_KW)
    t = jnp.einsum("hrc,cqk->hqrk", rpb * LOG2E, jnp.asarray(onehot), precision=lax.Precision.HIGHEST)
    t = jnp.where(in_win[:, None, :], t, NEG_BIG)
    bias = jnp.stack([t[:, :, NA_KH - 1 - dl:2 * NA_KH - 1 - dl] for dl in range(NA_KH)])
    return bias.reshape(NA_KH, NA_HEADS // 2, 2 * GRID_W, NA_KH * GRID_W)


def _encoder_layer(x, p, tabs):
    b, n, d = x.shape
    x1 = ffn_ln(x.reshape(b * n, d), p["ffn1_w_in"], p["ffn1_w_out"], p["ln1_g"], p["ln1_b"])
    x1 = x1.reshape(b, n, d)
    naq, nak, nav, qt, k, vt, ga, gb, qn2, lb, kn2 = mixer_in(x1, p["mixer"], tabs)
    na = na_attention(naq, nak, nav, p["na_bias"])
    at = mla_attention(qt, k, vt, qn2, lb, kn2).reshape(b, MLA_HEADS * MLA_V, n)
    x2 = mixer_out(x1, na, at, ga, gb, p["w_na_o"], p["w_mla_o"], p["w_out"], p["ln2_g"], p["ln2_b"])
    y = ffn_ln(x2.reshape(b * n, d), p["ffn2_w_in"], p["ffn2_w_out"], p["ln3_g"], p["ln3_b"])
    return y.reshape(b, n, d)


def kernel(x_prompt, x_sample, ffn1_w_in, ffn1_w_out, ln1_g, ln1_b, w_in, b_gate, na_rpb, q_norm_g, kv_norm_g, w_uq, w_ukv, w_na_o, w_mla_o, w_out, ln2_g, ln2_b, ffn2_w_in, ffn2_w_out, ln3_g, ln3_b):
    l = 0
    row = lambda a: a[l].reshape(1, -1)
    p = {
        "ffn1_w_in": ffn1_w_in[l].astype(BF16), "ffn1_w_out": ffn1_w_out[l].astype(BF16),
        "ln1_g": row(ln1_g), "ln1_b": row(ln1_b),
        "mixer": _prep_mixer_weights(w_in[l], b_gate[l], q_norm_g[l], kv_norm_g[l], w_uq[l], w_ukv[l]),
        "na_bias": _na_bias_table(na_rpb[l]),
        "w_na_o": w_na_o[l].astype(BF16), "w_mla_o": w_mla_o[l].astype(BF16), "w_out": w_out[l].astype(BF16),
        "ln2_g": row(ln2_g), "ln2_b": row(ln2_b),
        "ffn2_w_in": ffn2_w_in[l].astype(BF16), "ffn2_w_out": ffn2_w_out[l].astype(BF16),
        "ln3_g": row(ln3_g), "ln3_b": row(ln3_b),
    }
    outs = []
    for x in (x_prompt, x_sample):
        outs.append(_encoder_layer(x, p, _rope_tables(x.shape[1])))
    return tuple(outs)
```

```python
import functools
import math

import numpy as np
import jax
import jax.numpy as jnp
from jax import lax
from jax.experimental import pallas as pl
from jax.experimental.pallas import tpu as pltpu

F32 = jnp.float32
BF16 = jnp.bfloat16

DEPTH = 1
GRID_W = 64
NA_HEADS = 8
NA_HEAD_DIM = 64
NA_WIDTH = NA_HEADS * NA_HEAD_DIM
NA_KH = 8
NA_KW = 16
MLA_HEADS = 8
MLA_NOPE = 64
MLA_ROPE = 32
MLA_QK = MLA_NOPE + MLA_ROPE
MLA_V = 64
MLA_VA = MLA_V + 16
MLA_Q_RANK = 384
MLA_KV_RANK = 256
ROPE_BASE = 10000.0
LN_EPS = 1e-5
RMS_EPS = 1e-6
ALPHA = (2.0 * DEPTH) ** 0.25
LOG2E = math.log2(math.e)

LANES = 128
QK_PAD = 128
VMEM_LIMIT = 56 * 1024 * 1024

TOKEN_TILE = 512
FFN_TILE = 1024
FFN_CHUNK = 256
MIX_CHUNK = 256
NA_ROWS_PER_STEP = 32
NA_BLOCKS_PER_ITER = 6
MLA_TQ = 512
MLA_TK = 512
MLA_GROUP = 2
MLA_NBUF = 2
MLA_BLOCKS_PER_ITER = 10
MLA_PROBE_KEYS = 16
MLA_BOUND_INFLATE = 1.02
MLA_MAX_GAP = 64.0
NEG_BIG = -1e30


def _const_spec(shape):
    nd = len(shape)
    return pl.BlockSpec(shape, lambda *_: (0,) * nd, pipeline_mode=pl.Buffered(1))


def _layer_norm(y, g, b):
    mu = jnp.mean(y, axis=-1, keepdims=True)
    d = y - mu
    var = jnp.mean(d * d, axis=-1, keepdims=True)
    return d * lax.rsqrt(var + LN_EPS) * g + b


def _rms_norm(y, g):
    return y * lax.rsqrt(jnp.mean(y * y, axis=-1, keepdims=True) + RMS_EPS) * g


def _dot(a, b):
    return jnp.dot(a, b, preferred_element_type=F32)


def _swiglu_ln(x, w_in_ref, w_out_ref, g_ref, b_ref, h_ref, d_ff):
    xb = x.astype(BF16)
    for c in range(d_ff // FFN_CHUNK):
        lo, hi = c * FFN_CHUNK, (c + 1) * FFN_CHUNK
        a = _dot(xb, w_in_ref[:, lo:hi])
        u = _dot(xb, w_in_ref[:, d_ff + lo:d_ff + hi])
        h_ref[:, lo:hi] = (a * jax.nn.sigmoid(a) * u).astype(BF16)
    y = _dot(h_ref[...], w_out_ref[...])
    return _layer_norm(ALPHA * x + 0.5 * y, g_ref[...], b_ref[...])


def _ffn_ln_kernel(x_ref, w_in_ref, w_out_ref, g_ref, b_ref, o_ref, h_ref, *, d_ff):
    o_ref[...] = _swiglu_ln(x_ref[...], w_in_ref, w_out_ref, g_ref, b_ref, h_ref, d_ff)


def ffn_ln(x, w_in, w_out, g, b):
    t, d = x.shape
    d_ff = w_out.shape[0]
    tm = FFN_TILE
    return pl.pallas_call(
        functools.partial(_ffn_ln_kernel, d_ff=d_ff),
        out_shape=jax.ShapeDtypeStruct((t, d), F32),
        grid=(t // tm,),
        in_specs=[pl.BlockSpec((tm, d), lambda i: (i, 0)),
                  _const_spec(w_in.shape), _const_spec(w_out.shape),
                  _const_spec(g.shape), _const_spec(b.shape)],
        out_specs=pl.BlockSpec((tm, d), lambda i: (i, 0)),
        scratch_shapes=[pltpu.VMEM((tm, d_ff), BF16)],
        compiler_params=pltpu.CompilerParams(dimension_semantics=("arbitrary",),
                                             vmem_limit_bytes=VMEM_LIMIT),
        name="ffn_ln",
    )(x, w_in, w_out, g, b)


_C_NAQ, _C_NAK, _C_NAV = 0, NA_WIDTH, 2 * NA_WIDTH
_C_CQ = 3 * NA_WIDTH
_C_CKV = _C_CQ + MLA_Q_RANK
_C_KR = _C_CKV + MLA_KV_RANK
_C_GATE = _C_KR + 2 * QK_PAD


def _mixer_in_kernel(x_ref, w_ref, bg_ref, qg_ref, kvg_ref, wk_ref, wvt_ref, wqt_ref, wqs_ref,
                     ck_ref, sk_ref, cq_ref, sq_ref,
                     naq_ref, nak_ref, nav_ref, qt_ref, k_ref, vt_ref, ga_ref, gb_ref,
                     qn2_ref, lb_ref, kn2_ref, *, d_model):
    xb = x_ref[...].astype(BF16)
    naq_ref[...] = (_dot(xb, w_ref[:, _C_NAQ:_C_NAK]) * (NA_HEAD_DIM ** -0.5 * LOG2E)).astype(BF16)
    nak_ref[...] = _dot(xb, w_ref[:, _C_NAK:_C_NAV]).astype(BF16)
    nav_ref[...] = _dot(xb, w_ref[:, _C_NAV:_C_CQ]).astype(BF16)

    cqn = _rms_norm(_dot(xb, w_ref[:, _C_CQ:_C_CKV]), qg_ref[...])
    ckvn = _rms_norm(_dot(xb, w_ref[:, _C_CKV:_C_KR]), kvg_ref[...])

    kr = _dot(xb, w_ref[:, _C_KR:_C_GATE])
    kr_blk = kr[:, :QK_PAD] * ck_ref[...] + kr[:, QK_PAD:] * sk_ref[...]
    kall = _dot(ckvn.astype(BF16), wk_ref[...])
    k_probe = []
    one_hot = (lax.broadcasted_iota(jnp.int32, kr_blk.shape, 1) == MLA_QK).astype(F32)
    for h in range(MLA_HEADS):
        kf = kall[:, h * QK_PAD:(h + 1) * QK_PAD] + kr_blk
        kb = (kf + one_hot).astype(BF16)
        k_ref[h] = kb
        k2 = jnp.max(jnp.sum(kf * kf, axis=1, keepdims=True), axis=0, keepdims=True)
        kn2_ref[h] = jnp.broadcast_to(k2, (1, LANES))
        k_probe.append(kb[0:MLA_PROBE_KEYS])

    ckvn_t = ckvn.T.astype(BF16)
    vt = _dot(wvt_ref[...], ckvn_t)
    ones = jnp.ones((MLA_VA - MLA_V, vt.shape[1]), BF16)
    for h in range(MLA_HEADS):
        vt_ref[h, 0:MLA_V] = vt[h * MLA_V:(h + 1) * MLA_V].astype(BF16)
        vt_ref[h, MLA_V:MLA_VA] = ones

    cqn_t = cqn.T.astype(BF16)
    q_scale = (MLA_QK ** -0.5) * LOG2E
    qt = _dot(wqt_ref[...], cqn_t) * q_scale
    qs = _dot(wqs_ref[...], cqn_t) * q_scale
    cq, sq = cq_ref[...], sq_ref[...]
    zeros = jnp.zeros((QK_PAD - MLA_QK, qt.shape[1]), BF16)
    for h in range(MLA_HEADS):
        base = h * QK_PAD
        nope = qt[base:base + MLA_NOPE]
        rope = qt[base + MLA_NOPE:base + MLA_QK] * cq + qs[h * MLA_ROPE:(h + 1) * MLA_ROPE] * sq
        qh = jnp.concatenate([nope.astype(BF16), rope.astype(BF16), zeros], axis=0)
        qt_ref[h] = qh
        qn2_ref[h] = (jnp.sum(nope * nope, axis=0, keepdims=True)
                      + jnp.sum(rope * rope, axis=0, keepdims=True))
        lb_ref[h] = jnp.max(_dot(k_probe[h], qh), axis=0, keepdims=True)

    for half, out_ref in enumerate((ga_ref, gb_ref)):
        lo = _C_GATE + half * d_model
        logits = _dot(xb, w_ref[:, lo:lo + d_model]) + bg_ref[:, half * d_model:(half + 1) * d_model]
        out_ref[...] = jax.nn.sigmoid(logits).astype(BF16)


def mixer_in(x, wts, tabs):
    b, n, d = x.shape
    tm = TOKEN_TILE
    h = MLA_HEADS
    tok = lambda c: pl.BlockSpec((None, tm, c), lambda bi, i: (bi, i, 0))
    out_shape = (
        jax.ShapeDtypeStruct((b, n, NA_WIDTH), BF16), jax.ShapeDtypeStruct((b, n, NA_WIDTH), BF16),
        jax.ShapeDtypeStruct((b, n, NA_WIDTH), BF16),
        jax.ShapeDtypeStruct((b, h, QK_PAD, n), BF16),
        jax.ShapeDtypeStruct((b, h, n, QK_PAD), BF16),
        jax.ShapeDtypeStruct((b, h, MLA_VA, n), BF16),
        jax.ShapeDtypeStruct((b, n, d), BF16), jax.ShapeDtypeStruct((b, n, d), BF16),
        jax.ShapeDtypeStruct((b, h, 1, n), F32), jax.ShapeDtypeStruct((b, h, 1, n), F32),
        jax.ShapeDtypeStruct((b, n // tm, h, 1, LANES), F32),
    )
    stat = pl.BlockSpec((None, h, 1, tm), lambda bi, i: (bi, 0, 0, i))
    out_specs = (
        tok(NA_WIDTH), tok(NA_WIDTH), tok(NA_WIDTH),
        pl.BlockSpec((None, h, QK_PAD, tm), lambda bi, i: (bi, 0, 0, i)),
        pl.BlockSpec((None, h, tm, QK_PAD), lambda bi, i: (bi, 0, i, 0)),
        pl.BlockSpec((None, h, MLA_VA, tm), lambda bi, i: (bi, 0, 0, i)),
        tok(d), tok(d),
        stat, stat,
        pl.BlockSpec((None, None, h, 1, LANES), lambda bi, i: (bi, i, 0, 0, 0)),
    )
    consts = [wts["w_in"], wts["b_gate"], wts["q_norm_g"], wts["kv_norm_g"],
              wts["wk"], wts["wvt"], wts["wqt"], wts["wqs"]]
    in_specs = ([tok(d)] + [_const_spec(c.shape) for c in consts] + [
        pl.BlockSpec((tm, QK_PAD), lambda bi, i: (i, 0)),
        pl.BlockSpec((tm, QK_PAD), lambda bi, i: (i, 0)),
        pl.BlockSpec((MLA_ROPE, tm), lambda bi, i: (0, i)),
        pl.BlockSpec((MLA_ROPE, tm), lambda bi, i: (0, i)),
    ])
    return pl.pallas_call(
        functools.partial(_mixer_in_kernel, d_model=d),
        out_shape=out_shape,
        grid=(b, n // tm),
        in_specs=in_specs,
        out_specs=out_specs,
        compiler_params=pltpu.CompilerParams(dimension_semantics=("arbitrary", "arbitrary"),
                                             vmem_limit_bytes=VMEM_LIMIT),
        name="mixer_in",
    )(x, *consts, tabs["ck"], tabs["sk"], tabs["cq"], tabs["sq"])


def _na_first_key_row(i, rows):
    g = NA_ROWS_PER_STEP
    return jnp.clip(i * g - NA_KH, 0, rows - (g + 2 * NA_KH))


def _na_kernel(q_ref, k_ref, v_ref, bias_ref, o_ref, s0, s1, p0, p1, r0, r1, *, rows):
    g = NA_ROWS_PER_STEP
    window = NA_KH * GRID_W
    npair = NA_HEADS // 2
    i = pl.program_id(1)
    base_row = _na_first_key_row(i, rows)
    kbuf, vbuf = k_ref.at[0], v_ref.at[0]
    lane = lax.broadcasted_iota(jnp.int32, (GRID_W, LANES), 1)
    first_head = lane < NA_HEAD_DIM
    s_bufs, p_bufs, r_bufs = (s0, s1), (p0, p1), (r0, r1)
    lanes_of = lambda hp: slice(hp * LANES, (hp + 1) * LANES)

    def geometry(rho):
        r = i * g + rho
        rs = jnp.clip(r - NA_KH // 2, 0, rows - NA_KH)
        off = pl.multiple_of((rs - base_row) * GRID_W, GRID_W)
        return r - rs, off, pl.multiple_of(rho * GRID_W, GRID_W)

    def scores(rho, slot):
        delta, off, qoff = geometry(rho)
        for hp in range(npair):
            q2 = q_ref[pl.ds(qoff, GRID_W), lanes_of(hp)]
            zero = jnp.zeros_like(q2)
            qs = jnp.concatenate([jnp.where(first_head, q2, zero),
                                  jnp.where(first_head, zero, q2)], axis=0)
            k2 = kbuf[pl.ds(off, window), lanes_of(hp)]
            s = lax.dot_general(qs, k2, (((1,), (1,)), ((), ())), preferred_element_type=F32)
            s_bufs[slot][hp] = s + bias_ref[delta, hp]

    def softmax(slot):
        for hp in range(npair):
            for rows_ in (slice(0, GRID_W), slice(GRID_W, 2 * GRID_W)):
                s = s_bufs[slot][hp, rows_]
                p = jnp.exp2(s - jnp.max(s, axis=1, keepdims=True))
                p_bufs[slot][hp, rows_] = p.astype(BF16)
                r_bufs[slot][hp, rows_] = jnp.broadcast_to(1.0 / jnp.sum(p, axis=1, keepdims=True),
                                                           (GRID_W, LANES))

    def values(rho, slot):
        _, off, qoff = geometry(rho)
        for hp in range(npair):
            o = _dot(p_bufs[slot][hp], vbuf[pl.ds(off, window), lanes_of(hp)])
            o = o * r_bufs[slot][hp]
            out2 = jnp.where(first_head, o[:GRID_W], o[GRID_W:])
            o_ref[pl.ds(qoff, GRID_W), lanes_of(hp)] = out2.astype(BF16)

    def block(t, slot, do_scores=True, do_softmax=True, do_values=True):
        if do_scores:
            scores(t + 1, 1 - slot)
        if do_softmax:
            softmax(slot)
        if do_values:
            values(t - 1, 1 - slot)

    assert g % 2 == 0 and g >= 4
    block(-1, 1, do_softmax=False, do_values=False)
    block(0, 0, do_values=False)

    unroll = NA_BLOCKS_PER_ITER
    assert unroll % 2 == 0 and (g - 2) % unroll == 0

    def blocks(j, carry):
        for kk in range(unroll):
            block(unroll * j + 1 + kk, (1 + kk) % 2)
        return carry

    lax.fori_loop(0, (g - 2) // unroll, blocks, 0)
    block(g - 1, 1, do_scores=False)
    block(g, 0, do_scores=False, do_softmax=False)


def na_attention(q, k, v, bias):
    b, n, c = q.shape
    rows = n // GRID_W
    g = NA_ROWS_PER_STEP
    main = g * GRID_W
    halo = NA_KH * GRID_W
    assert rows >= g + 2 * NA_KH
    spec_main = pl.BlockSpec((None, main, c), lambda bi, i: (bi, i, 0))
    spec_kv = pl.BlockSpec((pl.Element(1), pl.Element(main + 2 * halo), pl.Element(c)),
                           lambda bi, i: (bi, _na_first_key_row(i, rows) * GRID_W, 0))
    return pl.pallas_call(
        functools.partial(_na_kernel, rows=rows),
        out_shape=jax.ShapeDtypeStruct((b, n, c), BF16),
        grid=(b, rows // g),
        in_specs=[spec_main, spec_kv, spec_kv, _const_spec(bias.shape)],
        out_specs=spec_main,
        scratch_shapes=[pltpu.VMEM(bias.shape[1:], F32)] * 2 + [pltpu.VMEM(bias.shape[1:], BF16)] * 2
        + [pltpu.VMEM(bias.shape[1:3] + (LANES,), F32)] * 2,
        compiler_params=pltpu.CompilerParams(dimension_semantics=("arbitrary", "arbitrary"),
                                             vmem_limit_bytes=VMEM_LIMIT),
        name="na_attention",
    )(q, k, v, bias)


def _mla_kernel(qt_ref, k_ref, vt_ref, o_ref, *bufs, n):
    tq, tk, g = MLA_TQ, MLA_TK, MLA_GROUP
    ng = n // (tk * g)
    total = (n // tq) * ng
    nbuf = MLA_NBUF
    assert total >= 4
    s_bufs, p_bufs = bufs[:nbuf], bufs[nbuf:]

    def split(u):
        return lax.div(u, ng), lax.rem(u, ng)

    def scores_chunk(u, slot, c):
        qi, kg = split(u)
        qt = qt_ref[:, pl.ds(pl.multiple_of(qi * tq, tq), tq)]
        off = pl.multiple_of((kg * g + c) * tk, tk)
        s = _dot(k_ref[pl.ds(off, tk), :], qt)
        s_bufs[slot][c * tk:(c + 1) * tk] = s
        return jnp.max(s, axis=0, keepdims=True)

    def first_group_reset(u, m):
        _, kg = split(u)
        return jnp.where(kg == 0, -jnp.inf, m)

    def softmax_chunk(slot, c, m, mx):
        m_new = jnp.maximum(m, mx)
        p_bufs[slot][c * tk:(c + 1) * tk] = jnp.exp2(s_bufs[slot][c * tk:(c + 1) * tk] - m_new).astype(BF16)
        return m_new, jnp.exp2(m - m_new)

    def values_chunk(u, slot, c, r, acc):
        _, kg = split(u)
        off = pl.multiple_of((kg * g + c) * tk, tk)
        return r * acc + _dot(vt_ref[:, pl.ds(off, tk)], p_bufs[slot][c * tk:(c + 1) * tk])

    def write_out(u, acc):
        qi, _ = split(u)
        out = acc[:MLA_V] * (1.0 / acc[MLA_V:MLA_V + 1])
        o_ref[:, pl.ds(pl.multiple_of(qi * tq, tq), tq)] = out.astype(BF16)

    def block(t, slot, carry, do_scores=True, do_softmax=True, do_values=True):
        m, mxs, rs_prev, acc = carry
        if do_softmax:
            m = first_group_reset(t, m)
        mxs_next, rs = list(mxs), list(rs_prev)
        if do_scores:
            for c in range(g):
                mxs_next[c] = scores_chunk(t + 1, (slot + 1) % nbuf, c)
        if do_softmax:
            for c in range(g):
                m, rs[c] = softmax_chunk(slot, c, m, mxs[c])
        if do_values:
            for c in range(g):
                acc = values_chunk(t - 1, (slot - 1) % nbuf, c, rs_prev[c], acc)
            write_out(t - 1, acc)
        return m, tuple(mxs_next), tuple(rs), acc

    row = jnp.zeros((1, tq), F32)
    carry = (row, (row,) * g, (row,) * g, jnp.zeros((MLA_VA, tq), F32))
    carry = block(-1, -1 % nbuf, carry, do_softmax=False, do_values=False)
    carry = block(0, 0, carry, do_values=False)

    unroll = MLA_BLOCKS_PER_ITER
    assert unroll % nbuf == 0
    n_iter, n_rest = divmod(total - 2, unroll)

    def blocks(i, carry):
        t0 = unroll * i + 1
        for kk in range(unroll):
            carry = block(t0 + kk, (1 + kk) % nbuf, carry)
        return carry

    carry = lax.fori_loop(0, n_iter, blocks, carry)
    for t in range(n_iter * unroll + 1, n_iter * unroll + 1 + n_rest):
        carry = block(t, t % nbuf, carry)
    carry = block(total - 1, (total - 1) % nbuf, carry, do_scores=False)
    block(total, total % nbuf, carry, do_scores=False, do_softmax=False)


def _mla_fast_kernel(qt_ref, k_ref, vt_ref, m_ref, o_ref, p0, p1, *, n):
    tq, tk, g = MLA_TQ, MLA_TK, MLA_GROUP
    ng = n // (tk * g)
    total = (n // tq) * ng
    p_bufs = (p0, p1)

    def split(u):
        return lax.div(u, ng), lax.rem(u, ng)

    def probs(u, slot, l):
        qi, kg = split(u)
        qoff = pl.multiple_of(qi * tq, tq)
        qt = qt_ref[:, pl.ds(qoff, tq)]
        m = m_ref[:, pl.ds(qoff, tq)]
        feat = lax.broadcasted_iota(jnp.int32, (QK_PAD, tq), 0)
        qt = jnp.where(feat == MLA_QK, -m, qt.astype(F32)).astype(BF16)
        l = jnp.where(kg == 0, 0.0, l)
        for c in range(g):
            off = pl.multiple_of((kg * g + c) * tk, tk)
            p = jnp.exp2(_dot(k_ref[pl.ds(off, tk), :], qt))
            p_bufs[slot][c * tk:(c + 1) * tk] = p.astype(BF16)
            l = l + jnp.sum(p, axis=0, keepdims=True)
        return l

    def values(u, slot, acc, l):
        qi, kg = split(u)
        acc = jnp.where(kg == 0, 0.0, acc)
        for c in range(g):
            off = pl.multiple_of((kg * g + c) * tk, tk)
            acc = acc + _dot(vt_ref[0:MLA_V, pl.ds(off, tk)], p_bufs[slot][c * tk:(c + 1) * tk])
        o_ref[:, pl.ds(pl.multiple_of(qi * tq, tq), tq)] = (acc * (1.0 / l)).astype(BF16)
        return acc

    def block(t, slot, carry):
        acc, l = carry
        l_next = probs(t + 1, 1 - slot, l)
        return values(t, slot, acc, l), l_next

    unroll = MLA_BLOCKS_PER_ITER
    assert unroll % 2 == 0 and total >= 2
    carry = (jnp.zeros((MLA_V, tq), F32), probs(0, 0, jnp.zeros((1, tq), F32)))
    carry = block(0, 0, carry)
    n_iter, n_rest = divmod(total - 2, unroll)

    def blocks(i, carry):
        t0 = unroll * i + 1
        for kk in range(unroll):
            carry = block(t0 + kk, (1 + kk) % 2, carry)
        return carry

    carry = lax.fori_loop(0, n_iter, blocks, carry)
    for t in range(n_iter * unroll + 1, total - 1):
        carry = block(t, t % 2, carry)
    values(total - 1, (total - 1) % 2, *carry)


def mla_attention(qt, k, vt, qn2, lb, kn2):
    b, h, _, n = qt.shape
    rows = MLA_GROUP * MLA_TK
    head = lambda r, c: pl.BlockSpec((None, None, r, c), lambda bi, hi: (bi, hi, 0, 0))
    params = pltpu.CompilerParams(dimension_semantics=("arbitrary",) * 2, vmem_limit_bytes=VMEM_LIMIT)
    out_shape = jax.ShapeDtypeStruct((b, h, MLA_V, n), BF16)

    k2max = jnp.max(kn2, axis=(1, 3, 4))
    m = (jnp.sqrt(qn2 * k2max[:, :, None, None]) * MLA_BOUND_INFLATE).astype(BF16).astype(F32)
    gap = m - lb

    def fast(qt, k, vt, m):
        return pl.pallas_call(
            functools.partial(_mla_fast_kernel, n=n),
            out_shape=out_shape,
            grid=(b, h),
            in_specs=[head(QK_PAD, n), head(n, QK_PAD), head(MLA_VA, n), head(1, n)],
            out_specs=head(MLA_V, n),
            scratch_shapes=[pltpu.VMEM((rows, MLA_TQ), BF16)] * 2,
            compiler_params=params,
            name="mla_attention_fast",
        )(qt, k, vt, m)

    def exact(qt, k, vt, m):
        return pl.pallas_call(
            functools.partial(_mla_kernel, n=n),
            out_shape=out_shape,
            grid=(b, h),
            in_specs=[head(QK_PAD, n), head(n, QK_PAD), head(MLA_VA, n)],
            out_specs=head(MLA_V, n),
            scratch_shapes=[pltpu.VMEM((rows, MLA_TQ), F32)] * MLA_NBUF + [pltpu.VMEM((rows, MLA_TQ), BF16)] * MLA_NBUF,
            compiler_params=params,
            name="mla_attention",
        )(qt, k, vt)

    return lax.cond(jnp.max(gap) < MLA_MAX_GAP, fast, exact, qt, k, vt, m)


def _mixer_out_kernel(x_ref, na_ref, at_ref, ga_ref, gb_ref, wna_ref, wmla_ref, wout_ref, g_ref, b_ref,
                      o_ref, mix_ref):
    na = na_ref[...]
    at = at_ref[...].T
    d = o_ref.shape[1]
    for c in range(d // MIX_CHUNK):
        cs = slice(c * MIX_CHUNK, (c + 1) * MIX_CHUNK)
        ya = _dot(na, wna_ref[:, cs])
        yb = _dot(at, wmla_ref[:, cs])
        mix_ref[:, cs] = (ga_ref[:, cs].astype(F32) * ya + gb_ref[:, cs].astype(F32) * yb).astype(BF16)
    y = _dot(mix_ref[...], wout_ref[...])
    o_ref[...] = _layer_norm(ALPHA * x_ref[...] + y, g_ref[...], b_ref[...])


def mixer_out(x, na, at, ga, gb, w_na_o, w_mla_o, w_out, g, bb):
    b, n, d = x.shape
    tm = FFN_TILE
    tok = lambda c: pl.BlockSpec((None, tm, c), lambda bi, i: (bi, i, 0))
    return pl.pallas_call(
        _mixer_out_kernel,
        out_shape=jax.ShapeDtypeStruct((b, n, d), F32),
        grid=(b, n // tm),
        in_specs=[tok(d), tok(NA_WIDTH),
                  pl.BlockSpec((None, at.shape[1], tm), lambda bi, i: (bi, 0, i)),
                  tok(d), tok(d),
                  _const_spec(w_na_o.shape), _const_spec(w_mla_o.shape), _const_spec(w_out.shape),
                  _const_spec(g.shape), _const_spec(bb.shape)],
        out_specs=tok(d),
        scratch_shapes=[pltpu.VMEM((tm, d), BF16)],
        compiler_params=pltpu.CompilerParams(dimension_semantics=("arbitrary", "arbitrary"),
                                             vmem_limit_bytes=VMEM_LIMIT),
        name="mixer_out",
    )(x, na, at, ga, gb, w_na_o, w_mla_o, w_out, g, bb)


def _prep_mixer_weights(w_in, b_gate, q_norm_g, kv_norm_g, w_uq, w_ukv):
    d = w_in.shape[0]
    half = MLA_ROPE // 2
    swap = lambda w: jnp.concatenate([-w[..., half:], w[..., :half]], axis=-1)
    pad_rope = lambda w: jnp.pad(w, ((0, 0), (MLA_NOPE, QK_PAD - MLA_QK)))
    c_kr = 3 * NA_WIDTH + MLA_Q_RANK + MLA_KV_RANK
    w_kr = w_in[:, c_kr:c_kr + MLA_ROPE]
    w_packed = jnp.concatenate([w_in[:, :c_kr], pad_rope(w_kr), pad_rope(swap(w_kr)),
                                w_in[:, c_kr + MLA_ROPE:]], axis=1)
    ukv = w_ukv.reshape(MLA_KV_RANK, MLA_HEADS, MLA_NOPE + MLA_V)
    wk = jnp.pad(ukv[..., :MLA_NOPE], ((0, 0), (0, 0), (0, QK_PAD - MLA_NOPE)))
    wk = wk.reshape(MLA_KV_RANK, MLA_HEADS * QK_PAD)
    wvt = ukv[..., MLA_NOPE:].reshape(MLA_KV_RANK, MLA_HEADS * MLA_V).T
    uq = w_uq.reshape(MLA_Q_RANK, MLA_HEADS, MLA_QK)
    wqt = jnp.pad(uq, ((0, 0), (0, 0), (0, QK_PAD - MLA_QK))).reshape(MLA_Q_RANK, MLA_HEADS * QK_PAD).T
    wqs = swap(uq[..., MLA_NOPE:]).reshape(MLA_Q_RANK, MLA_HEADS * MLA_ROPE).T
    return {
        "w_in": w_packed.astype(BF16), "b_gate": b_gate.reshape(1, 2 * d),
        "q_norm_g": q_norm_g.reshape(1, -1), "kv_norm_g": kv_norm_g.reshape(1, -1),
        "wk": wk.astype(BF16), "wvt": wvt.astype(BF16), "wqt": wqt.astype(BF16), "wqs": wqs.astype(BF16),
    }


def _rope_tables(n):
    inv = 1.0 / (ROPE_BASE ** (jnp.arange(0, MLA_ROPE, 2, dtype=F32) / MLA_ROPE))
    ang = jnp.arange(n, dtype=F32)[:, None] * inv[None, :]
    cos2 = jnp.tile(jnp.cos(ang), (1, 2))
    sin2 = jnp.tile(jnp.sin(ang), (1, 2))
    pad = ((0, 0), (MLA_NOPE, QK_PAD - MLA_QK))
    return {"ck": jnp.pad(cos2, pad), "sk": jnp.pad(sin2, pad), "cq": cos2.T, "sq": sin2.T}


def _na_bias_table(rpb):
    qc = np.arange(GRID_W)[:, None]
    kc = np.arange(GRID_W)[None, :]
    dc = np.clip(kc - qc + NA_KW - 1, 0, 2 * NA_KW - 2)
    onehot = (dc[None] == np.arange(2 * NA_KW - 1)[:, None, None]).astype(np.float32)
    win = np.clip(qc - NA_KW // 2, 0, GRID_W - NA_KW)
    in_win = (kc >= win) & (kc < win + NA_KW)
    t = jnp.einsum("hrc,cqk->hqrk", rpb * LOG2E, jnp.asarray(onehot), precision=lax.Precision.HIGHEST)
    t = jnp.where(in_win[:, None, :], t, NEG_BIG)
    bias = jnp.stack([t[:, :, NA_KH - 1 - dl:2 * NA_KH - 1 - dl] for dl in range(NA_KH)])
    return bias.reshape(NA_KH, NA_HEADS // 2, 2 * GRID_W, NA_KH * GRID_W)


def _encoder_layer(x, p, tabs):
    b, n, d = x.shape
    x1 = ffn_ln(x.reshape(b * n, d), p["ffn1_w_in"], p["ffn1_w_out"], p["ln1_g"], p["ln1_b"])
    x1 = x1.reshape(b, n, d)
    naq, nak, nav, qt, k, vt, ga, gb, qn2, lb, kn2 = mixer_in(x1, p["mixer"], tabs)
    na = na_attention(naq, nak, nav, p["na_bias"])
    at = mla_attention(qt, k, vt, qn2, lb, kn2).reshape(b, MLA_HEADS * MLA_V, n)
    x2 = mixer_out(x1, na, at, ga, gb, p["w_na_o"], p["w_mla_o"], p["w_out"], p["ln2_g"], p["ln2_b"])
    y = ffn_ln(x2.reshape(b * n, d), p["ffn2_w_in"], p["ffn2_w_out"], p["ln3_g"], p["ln3_b"])
    return y.reshape(b, n, d)


def kernel(x_prompt, x_sample, ffn1_w_in, ffn1_w_out, ln1_g, ln1_b, w_in, b_gate, na_rpb, q_norm_g, kv_norm_g, w_uq, w_ukv, w_na_o, w_mla_o, w_out, ln2_g, ln2_b, ffn2_w_in, ffn2_w_out, ln3_g, ln3_b):
    assert ffn1_w_in.shape[0] == DEPTH

    def layer_params(l):
        row = lambda a: a[l].reshape(1, -1)
        return {
            "ffn1_w_in": ffn1_w_in[l].astype(BF16), "ffn1_w_out": ffn1_w_out[l].astype(BF16),
            "ln1_g": row(ln1_g), "ln1_b": row(ln1_b),
            "mixer": _prep_mixer_weights(w_in[l], b_gate[l], q_norm_g[l], kv_norm_g[l], w_uq[l], w_ukv[l]),
            "na_bias": _na_bias_table(na_rpb[l]),
            "w_na_o": w_na_o[l].astype(BF16), "w_mla_o": w_mla_o[l].astype(BF16), "w_out": w_out[l].astype(BF16),
            "ln2_g": row(ln2_g), "ln2_b": row(ln2_b),
            "ffn2_w_in": ffn2_w_in[l].astype(BF16), "ffn2_w_out": ffn2_w_out[l].astype(BF16),
            "ln3_g": row(ln3_g), "ln3_b": row(ln3_b),
        }

    layers = [layer_params(l) for l in range(DEPTH)]
    outs = []
    for x in (x_prompt, x_sample):
        tabs = _rope_tables(x.shape[1])
        for p in layers:
            x = _encoder_layer(x, p, tabs)
        outs.append(x)
    return tuple(outs)
```

```python
import functools
import math

import numpy as np
import jax
import jax.numpy as jnp
from jax import lax
from jax.experimental import pallas as pl
from jax.experimental.pallas import tpu as pltpu

F32 = jnp.float32
BF16 = jnp.bfloat16

DEPTH = 1
GRID_W = 64
NA_HEADS = 8
NA_HEAD_DIM = 64
NA_WIDTH = NA_HEADS * NA_HEAD_DIM
NA_KH = 8
NA_KW = 16
MLA_HEADS = 8
MLA_NOPE = 64
MLA_ROPE = 32
MLA_QK = MLA_NOPE + MLA_ROPE
MLA_V = 64
MLA_VA = MLA_V + 16
MLA_Q_RANK = 384
MLA_KV_RANK = 256
ROPE_BASE = 10000.0
LN_EPS = 1e-5
RMS_EPS = 1e-6
ALPHA = (2.0 * DEPTH) ** 0.25
LOG2E = math.log2(math.e)

LANES = 128
QK_PAD = 128
VMEM_LIMIT = 56 * 1024 * 1024

TOKEN_TILE = 512
FFN_TILE = 1024
FFN_CHUNK = 256
MIX_CHUNK = 256
NA_ROWS_PER_STEP = 32
NA_BLOCKS_PER_ITER = 6
MLA_TQ = 512
MLA_TK = 512
MLA_GROUP = 2
MLA_NBUF = 2
MLA_BLOCKS_PER_ITER = 10
MLA_FAST_BLOCKS_PER_ITER = (30, 10, 2)
MLA_PROBE_KEYS = 16
MLA_BOUND_INFLATE = 1.02
MLA_MAX_GAP = 64.0
NEG_BIG = -1e30


def _const_spec(shape):
    nd = len(shape)
    return pl.BlockSpec(shape, lambda *_: (0,) * nd, pipeline_mode=pl.Buffered(1))


def _layer_norm(y, g, b):
    mu = jnp.mean(y, axis=-1, keepdims=True)
    d = y - mu
    var = jnp.mean(d * d, axis=-1, keepdims=True)
    return d * lax.rsqrt(var + LN_EPS) * g + b


def _rms_norm(y, g):
    return y * lax.rsqrt(jnp.mean(y * y, axis=-1, keepdims=True) + RMS_EPS) * g


def _dot(a, b):
    return jnp.dot(a, b, preferred_element_type=F32)


def _swiglu_ln(x, w_in_ref, w_out_ref, g_ref, b_ref, h_ref, d_ff):
    xb = x.astype(BF16)
    for c in range(d_ff // FFN_CHUNK):
        lo, hi = c * FFN_CHUNK, (c + 1) * FFN_CHUNK
        a = _dot(xb, w_in_ref[:, lo:hi])
        u = _dot(xb, w_in_ref[:, d_ff + lo:d_ff + hi])
        h_ref[:, lo:hi] = (a * jax.nn.sigmoid(a) * u).astype(BF16)
    y = _dot(h_ref[...], w_out_ref[...])
    return _layer_norm(ALPHA * x + 0.5 * y, g_ref[...], b_ref[...])


def _ffn_ln_kernel(x_ref, w_in_ref, w_out_ref, g_ref, b_ref, o_ref, h_ref, *, d_ff):
    o_ref[...] = _swiglu_ln(x_ref[...], w_in_ref, w_out_ref, g_ref, b_ref, h_ref, d_ff)


def ffn_ln(x, w_in, w_out, g, b):
    t, d = x.shape
    d_ff = w_out.shape[0]
    tm = FFN_TILE
    return pl.pallas_call(
        functools.partial(_ffn_ln_kernel, d_ff=d_ff),
        out_shape=jax.ShapeDtypeStruct((t, d), F32),
        grid=(t // tm,),
        in_specs=[pl.BlockSpec((tm, d), lambda i: (i, 0)),
                  _const_spec(w_in.shape), _const_spec(w_out.shape),
                  _const_spec(g.shape), _const_spec(b.shape)],
        out_specs=pl.BlockSpec((tm, d), lambda i: (i, 0)),
        scratch_shapes=[pltpu.VMEM((tm, d_ff), BF16)],
        compiler_params=pltpu.CompilerParams(dimension_semantics=("arbitrary",),
                                             vmem_limit_bytes=VMEM_LIMIT),
        name="ffn_ln",
    )(x, w_in, w_out, g, b)


_C_NAQ, _C_NAK, _C_NAV = 0, NA_WIDTH, 2 * NA_WIDTH
_C_CQ = 3 * NA_WIDTH
_C_CKV = _C_CQ + MLA_Q_RANK
_C_KR = _C_CKV + MLA_KV_RANK
_C_GATE = _C_KR + 2 * QK_PAD


def _mixer_in_kernel(x_ref, w_ref, bg_ref, qg_ref, kvg_ref, wk_ref, wvt_ref, wqt_ref, wqs_ref,
                     ck_ref, sk_ref, cq_ref, sq_ref,
                     naq_ref, nak_ref, nav_ref, qt_ref, k_ref, vt_ref, ga_ref, gb_ref,
                     qn2_ref, lb_ref, kn2_ref, *, d_model):
    xb = x_ref[...].astype(BF16)
    naq_ref[...] = (_dot(xb, w_ref[:, _C_NAQ:_C_NAK]) * (NA_HEAD_DIM ** -0.5 * LOG2E)).astype(BF16)
    nak_ref[...] = _dot(xb, w_ref[:, _C_NAK:_C_NAV]).astype(BF16)
    nav_ref[...] = _dot(xb, w_ref[:, _C_NAV:_C_CQ]).astype(BF16)

    cqn = _rms_norm(_dot(xb, w_ref[:, _C_CQ:_C_CKV]), qg_ref[...])
    ckvn = _rms_norm(_dot(xb, w_ref[:, _C_CKV:_C_KR]), kvg_ref[...])

    kr = _dot(xb, w_ref[:, _C_KR:_C_GATE])
    kr_blk = kr[:, :QK_PAD] * ck_ref[...] + kr[:, QK_PAD:] * sk_ref[...]
    kall = _dot(ckvn.astype(BF16), wk_ref[...])
    k_probe = []
    one_hot = (lax.broadcasted_iota(jnp.int32, kr_blk.shape, 1) == MLA_QK).astype(F32)
    for h in range(MLA_HEADS):
        kf = kall[:, h * QK_PAD:(h + 1) * QK_PAD] + kr_blk
        kb = (kf + one_hot).astype(BF16)
        k_ref[h] = kb
        k2 = jnp.max(jnp.sum(kf * kf, axis=1, keepdims=True), axis=0, keepdims=True)
        kn2_ref[h] = jnp.broadcast_to(k2, (1, LANES))
        k_probe.append(kb[0:MLA_PROBE_KEYS])

    ckvn_t = ckvn.T.astype(BF16)
    vt = _dot(wvt_ref[...], ckvn_t)
    ones = jnp.ones((MLA_VA - MLA_V, vt.shape[1]), BF16)
    for h in range(MLA_HEADS):
        vt_ref[h, 0:MLA_V] = vt[h * MLA_V:(h + 1) * MLA_V].astype(BF16)
        vt_ref[h, MLA_V:MLA_VA] = ones

    cqn_t = cqn.T.astype(BF16)
    q_scale = (MLA_QK ** -0.5) * LOG2E
    qt = _dot(wqt_ref[...], cqn_t) * q_scale
    qs = _dot(wqs_ref[...], cqn_t) * q_scale
    cq, sq = cq_ref[...], sq_ref[...]
    zeros = jnp.zeros((QK_PAD - MLA_QK, qt.shape[1]), BF16)
    for h in range(MLA_HEADS):
        base = h * QK_PAD
        nope = qt[base:base + MLA_NOPE]
        rope = qt[base + MLA_NOPE:base + MLA_QK] * cq + qs[h * MLA_ROPE:(h + 1) * MLA_ROPE] * sq
        qh = jnp.concatenate([nope.astype(BF16), rope.astype(BF16), zeros], axis=0)
        qt_ref[h] = qh
        qn2_ref[h] = (jnp.sum(nope * nope, axis=0, keepdims=True)
                      + jnp.sum(rope * rope, axis=0, keepdims=True))
        lb_ref[h] = jnp.max(_dot(k_probe[h], qh), axis=0, keepdims=True)

    for half, out_ref in enumerate((ga_ref, gb_ref)):
        lo = _C_GATE + half * d_model
        logits = _dot(xb, w_ref[:, lo:lo + d_model]) + bg_ref[:, half * d_model:(half + 1) * d_model]
        out_ref[...] = jax.nn.sigmoid(logits).astype(BF16)


def mixer_in(x, wts, tabs):
    b, n, d = x.shape
    tm = TOKEN_TILE
    h = MLA_HEADS
    tok = lambda c: pl.BlockSpec((None, tm, c), lambda bi, i: (bi, i, 0))
    out_shape = (
        jax.ShapeDtypeStruct((b, n, NA_WIDTH), BF16), jax.ShapeDtypeStruct((b, n, NA_WIDTH), BF16),
        jax.ShapeDtypeStruct((b, n, NA_WIDTH), BF16),
        jax.ShapeDtypeStruct((b, h, QK_PAD, n), BF16),
        jax.ShapeDtypeStruct((b, h, n, QK_PAD), BF16),
        jax.ShapeDtypeStruct((b, h, MLA_VA, n), BF16),
        jax.ShapeDtypeStruct((b, n, d), BF16), jax.ShapeDtypeStruct((b, n, d), BF16),
        jax.ShapeDtypeStruct((b, h, 1, n), F32), jax.ShapeDtypeStruct((b, h, 1, n), F32),
        jax.ShapeDtypeStruct((b, n // tm, h, 1, LANES), F32),
    )
    stat = pl.BlockSpec((None, h, 1, tm), lambda bi, i: (bi, 0, 0, i))
    out_specs = (
        tok(NA_WIDTH), tok(NA_WIDTH), tok(NA_WIDTH),
        pl.BlockSpec((None, h, QK_PAD, tm), lambda bi, i: (bi, 0, 0, i)),
        pl.BlockSpec((None, h, tm, QK_PAD), lambda bi, i: (bi, 0, i, 0)),
        pl.BlockSpec((None, h, MLA_VA, tm), lambda bi, i: (bi, 0, 0, i)),
        tok(d), tok(d),
        stat, stat,
        pl.BlockSpec((None, None, h, 1, LANES), lambda bi, i: (bi, i, 0, 0, 0)),
    )
    consts = [wts["w_in"], wts["b_gate"], wts["q_norm_g"], wts["kv_norm_g"],
              wts["wk"], wts["wvt"], wts["wqt"], wts["wqs"]]
    in_specs = ([tok(d)] + [_const_spec(c.shape) for c in consts] + [
        pl.BlockSpec((tm, QK_PAD), lambda bi, i: (i, 0)),
        pl.BlockSpec((tm, QK_PAD), lambda bi, i: (i, 0)),
        pl.BlockSpec((MLA_ROPE, tm), lambda bi, i: (0, i)),
        pl.BlockSpec((MLA_ROPE, tm), lambda bi, i: (0, i)),
    ])
    return pl.pallas_call(
        functools.partial(_mixer_in_kernel, d_model=d),
        out_shape=out_shape,
        grid=(b, n // tm),
        in_specs=in_specs,
        out_specs=out_specs,
        compiler_params=pltpu.CompilerParams(dimension_semantics=("arbitrary", "arbitrary"),
                                             vmem_limit_bytes=VMEM_LIMIT),
        name="mixer_in",
    )(x, *consts, tabs["ck"], tabs["sk"], tabs["cq"], tabs["sq"])


def _na_first_key_row(i, rows):
    g = NA_ROWS_PER_STEP
    return jnp.clip(i * g - NA_KH, 0, rows - (g + 2 * NA_KH))


def _na_kernel(q_ref, k_ref, v_ref, bias_ref, o_ref, s0, s1, p0, p1, r0, r1, *, rows):
    g = NA_ROWS_PER_STEP
    window = NA_KH * GRID_W
    npair = NA_HEADS // 2
    i = pl.program_id(1)
    base_row = _na_first_key_row(i, rows)
    kbuf, vbuf = k_ref.at[0], v_ref.at[0]
    lane = lax.broadcasted_iota(jnp.int32, (GRID_W, LANES), 1)
    first_head = lane < NA_HEAD_DIM
    s_bufs, p_bufs, r_bufs = (s0, s1), (p0, p1), (r0, r1)
    lanes_of = lambda hp: slice(hp * LANES, (hp + 1) * LANES)

    def geometry(rho):
        r = i * g + rho
        rs = jnp.clip(r - NA_KH // 2, 0, rows - NA_KH)
        off = pl.multiple_of((rs - base_row) * GRID_W, GRID_W)
        return r - rs, off, pl.multiple_of(rho * GRID_W, GRID_W)

    def scores(rho, slot):
        delta, off, qoff = geometry(rho)
        for hp in range(npair):
            q2 = q_ref[pl.ds(qoff, GRID_W), lanes_of(hp)]
            zero = jnp.zeros_like(q2)
            qs = jnp.concatenate([jnp.where(first_head, q2, zero),
                                  jnp.where(first_head, zero, q2)], axis=0)
            k2 = kbuf[pl.ds(off, window), lanes_of(hp)]
            s = lax.dot_general(qs, k2, (((1,), (1,)), ((), ())), preferred_element_type=F32)
            s_bufs[slot][hp] = s + bias_ref[delta, hp]

    def softmax(slot):
        for hp in range(npair):
            for rows_ in (slice(0, GRID_W), slice(GRID_W, 2 * GRID_W)):
                s = s_bufs[slot][hp, rows_]
                p = jnp.exp2(s - jnp.max(s, axis=1, keepdims=True))
                p_bufs[slot][hp, rows_] = p.astype(BF16)
                r_bufs[slot][hp, rows_] = jnp.broadcast_to(1.0 / jnp.sum(p, axis=1, keepdims=True),
                                                           (GRID_W, LANES))

    def values(rho, slot):
        _, off, qoff = geometry(rho)
        for hp in range(npair):
            o = _dot(p_bufs[slot][hp], vbuf[pl.ds(off, window), lanes_of(hp)])
            o = o * r_bufs[slot][hp]
            out2 = jnp.where(first_head, o[:GRID_W], o[GRID_W:])
            o_ref[pl.ds(qoff, GRID_W), lanes_of(hp)] = out2.astype(BF16)

    def block(t, slot, do_scores=True, do_softmax=True, do_values=True):
        if do_scores:
            scores(t + 1, 1 - slot)
        if do_softmax:
            softmax(slot)
        if do_values:
            values(t - 1, 1 - slot)

    assert g % 2 == 0 and g >= 4
    block(-1, 1, do_softmax=False, do_values=False)
    block(0, 0, do_values=False)

    unroll = NA_BLOCKS_PER_ITER
    assert unroll % 2 == 0 and (g - 2) % unroll == 0

    def blocks(j, carry):
        for kk in range(unroll):
            block(unroll * j + 1 + kk, (1 + kk) % 2)
        return carry

    lax.fori_loop(0, (g - 2) // unroll, blocks, 0)
    block(g - 1, 1, do_scores=False)
    block(g, 0, do_scores=False, do_softmax=False)


def na_attention(q, k, v, bias):
    b, n, c = q.shape
    rows = n // GRID_W
    g = NA_ROWS_PER_STEP
    main = g * GRID_W
    halo = NA_KH * GRID_W
    assert rows >= g + 2 * NA_KH
    spec_main = pl.BlockSpec((None, main, c), lambda bi, i: (bi, i, 0))
    spec_kv = pl.BlockSpec((pl.Element(1), pl.Element(main + 2 * halo), pl.Element(c)),
                           lambda bi, i: (bi, _na_first_key_row(i, rows) * GRID_W, 0))
    return pl.pallas_call(
        functools.partial(_na_kernel, rows=rows),
        out_shape=jax.ShapeDtypeStruct((b, n, c), BF16),
        grid=(b, rows // g),
        in_specs=[spec_main, spec_kv, spec_kv, _const_spec(bias.shape)],
        out_specs=spec_main,
        scratch_shapes=[pltpu.VMEM(bias.shape[1:], F32)] * 2 + [pltpu.VMEM(bias.shape[1:], BF16)] * 2
        + [pltpu.VMEM(bias.shape[1:3] + (LANES,), F32)] * 2,
        compiler_params=pltpu.CompilerParams(dimension_semantics=("arbitrary", "arbitrary"),
                                             vmem_limit_bytes=VMEM_LIMIT),
        name="na_attention",
    )(q, k, v, bias)


def _mla_kernel(qt_ref, k_ref, vt_ref, o_ref, *bufs, n):
    tq, tk, g = MLA_TQ, MLA_TK, MLA_GROUP
    ng = n // (tk * g)
    total = (n // tq) * ng
    nbuf = MLA_NBUF
    assert total >= 4
    s_bufs, p_bufs = bufs[:nbuf], bufs[nbuf:]

    def split(u):
        return lax.div(u, ng), lax.rem(u, ng)

    def scores_chunk(u, slot, c):
        qi, kg = split(u)
        qt = qt_ref[:, pl.ds(pl.multiple_of(qi * tq, tq), tq)]
        off = pl.multiple_of((kg * g + c) * tk, tk)
        s = _dot(k_ref[pl.ds(off, tk), :], qt)
        s_bufs[slot][c * tk:(c + 1) * tk] = s
        return jnp.max(s, axis=0, keepdims=True)

    def first_group_reset(u, m):
        _, kg = split(u)
        return jnp.where(kg == 0, -jnp.inf, m)

    def softmax_chunk(slot, c, m, mx):
        m_new = jnp.maximum(m, mx)
        p_bufs[slot][c * tk:(c + 1) * tk] = jnp.exp2(s_bufs[slot][c * tk:(c + 1) * tk] - m_new).astype(BF16)
        return m_new, jnp.exp2(m - m_new)

    def values_chunk(u, slot, c, r, acc):
        _, kg = split(u)
        off = pl.multiple_of((kg * g + c) * tk, tk)
        return r * acc + _dot(vt_ref[:, pl.ds(off, tk)], p_bufs[slot][c * tk:(c + 1) * tk])

    def write_out(u, acc):
        qi, _ = split(u)
        out = acc[:MLA_V] * (1.0 / acc[MLA_V:MLA_V + 1])
        o_ref[:, pl.ds(pl.multiple_of(qi * tq, tq), tq)] = out.astype(BF16)

    def block(t, slot, carry, do_scores=True, do_softmax=True, do_values=True):
        m, mxs, rs_prev, acc = carry
        if do_softmax:
            m = first_group_reset(t, m)
        mxs_next, rs = list(mxs), list(rs_prev)
        if do_scores:
            for c in range(g):
                mxs_next[c] = scores_chunk(t + 1, (slot + 1) % nbuf, c)
        if do_softmax:
            for c in range(g):
                m, rs[c] = softmax_chunk(slot, c, m, mxs[c])
        if do_values:
            for c in range(g):
                acc = values_chunk(t - 1, (slot - 1) % nbuf, c, rs_prev[c], acc)
            write_out(t - 1, acc)
        return m, tuple(mxs_next), tuple(rs), acc

    row = jnp.zeros((1, tq), F32)
    carry = (row, (row,) * g, (row,) * g, jnp.zeros((MLA_VA, tq), F32))
    carry = block(-1, -1 % nbuf, carry, do_softmax=False, do_values=False)
    carry = block(0, 0, carry, do_values=False)

    unroll = MLA_BLOCKS_PER_ITER
    assert unroll % nbuf == 0
    n_iter, n_rest = divmod(total - 2, unroll)

    def blocks(i, carry):
        t0 = unroll * i + 1
        for kk in range(unroll):
            carry = block(t0 + kk, (1 + kk) % nbuf, carry)
        return carry

    carry = lax.fori_loop(0, n_iter, blocks, carry)
    for t in range(n_iter * unroll + 1, n_iter * unroll + 1 + n_rest):
        carry = block(t, t % nbuf, carry)
    carry = block(total - 1, (total - 1) % nbuf, carry, do_scores=False)
    block(total, total % nbuf, carry, do_scores=False, do_softmax=False)


def _mla_fast_kernel(qt_ref, k_ref, vt_ref, m_ref, o_ref, p0, p1, *, n):
    tq, tk, g = MLA_TQ, MLA_TK, MLA_GROUP
    ng = n // (tk * g)
    total = (n // tq) * ng
    p_bufs = (p0, p1)

    def split(u):
        return lax.div(u, ng), lax.rem(u, ng)

    def probs(u, slot, l):
        qi, kg = split(u)
        qoff = pl.multiple_of(qi * tq, tq)
        qt = qt_ref[:, pl.ds(qoff, tq)]
        m = m_ref[:, pl.ds(qoff, tq)]
        feat = lax.broadcasted_iota(jnp.int32, (QK_PAD, tq), 0)
        qt = jnp.where(feat == MLA_QK, -m, qt.astype(F32)).astype(BF16)
        l = jnp.where(kg == 0, 0.0, l)
        for c in range(g):
            off = pl.multiple_of((kg * g + c) * tk, tk)
            p = jnp.exp2(_dot(k_ref[pl.ds(off, tk), :], qt))
            p_bufs[slot][c * tk:(c + 1) * tk] = p.astype(BF16)
            l = l + jnp.sum(p, axis=0, keepdims=True)
        return l

    def values(u, slot, acc, l):
        qi, kg = split(u)
        acc = jnp.where(kg == 0, 0.0, acc)
        for c in range(g):
            off = pl.multiple_of((kg * g + c) * tk, tk)
            acc = acc + _dot(vt_ref[0:MLA_V, pl.ds(off, tk)], p_bufs[slot][c * tk:(c + 1) * tk])
        o_ref[:, pl.ds(pl.multiple_of(qi * tq, tq), tq)] = (acc * (1.0 / l)).astype(BF16)
        return acc

    def block(t, slot, carry):
        acc, l = carry
        l_next = probs(t + 1, 1 - slot, l)
        return values(t, slot, acc, l), l_next

    unroll = next(u for u in MLA_FAST_BLOCKS_PER_ITER if (total - 2) // u >= 2 or u == 2)
    assert total >= 2
    carry = (jnp.zeros((MLA_V, tq), F32), probs(0, 0, jnp.zeros((1, tq), F32)))
    carry = block(0, 0, carry)
    n_iter, n_rest = divmod(total - 2, unroll)

    def blocks(i, carry):
        t0 = unroll * i + 1
        for kk in range(unroll):
            carry = block(t0 + kk, (1 + kk) % 2, carry)
        return carry

    carry = lax.fori_loop(0, n_iter, blocks, carry)
    for t in range(n_iter * unroll + 1, total - 1):
        carry = block(t, t % 2, carry)
    values(total - 1, (total - 1) % 2, *carry)


def mla_attention(qt, k, vt, qn2, lb, kn2):
    b, h, _, n = qt.shape
    rows = MLA_GROUP * MLA_TK
    head = lambda r, c: pl.BlockSpec((None, None, r, c), lambda bi, hi: (bi, hi, 0, 0))
    params = pltpu.CompilerParams(dimension_semantics=("arbitrary",) * 2, vmem_limit_bytes=VMEM_LIMIT)
    out_shape = jax.ShapeDtypeStruct((b, h, MLA_V, n), BF16)

    k2max = jnp.max(kn2, axis=(1, 3, 4))
    m = (jnp.sqrt(qn2 * k2max[:, :, None, None]) * MLA_BOUND_INFLATE).astype(BF16).astype(F32)
    gap = m - lb

    def fast(qt, k, vt, m):
        return pl.pallas_call(
            functools.partial(_mla_fast_kernel, n=n),
            out_shape=out_shape,
            grid=(b, h),
            in_specs=[head(QK_PAD, n), head(n, QK_PAD), head(MLA_VA, n), head(1, n)],
            out_specs=head(MLA_V, n),
            scratch_shapes=[pltpu.VMEM((rows, MLA_TQ), BF16)] * 2,
            compiler_params=params,
            name="mla_attention_fast",
        )(qt, k, vt, m)

    def exact(qt, k, vt, m):
        return pl.pallas_call(
            functools.partial(_mla_kernel, n=n),
            out_shape=out_shape,
            grid=(b, h),
            in_specs=[head(QK_PAD, n), head(n, QK_PAD), head(MLA_VA, n)],
            out_specs=head(MLA_V, n),
            scratch_shapes=[pltpu.VMEM((rows, MLA_TQ), F32)] * MLA_NBUF + [pltpu.VMEM((rows, MLA_TQ), BF16)] * MLA_NBUF,
            compiler_params=params,
            name="mla_attention",
        )(qt, k, vt)

    return lax.cond(jnp.max(gap) < MLA_MAX_GAP, fast, exact, qt, k, vt, m)


def _mixer_out_kernel(x_ref, na_ref, at_ref, ga_ref, gb_ref, wna_ref, wmla_ref, wout_ref, g_ref, b_ref,
                      o_ref, mix_ref):
    na = na_ref[...]
    at = at_ref[...].T
    d = o_ref.shape[1]
    for c in range(d // MIX_CHUNK):
        cs = slice(c * MIX_CHUNK, (c + 1) * MIX_CHUNK)
        ya = _dot(na, wna_ref[:, cs])
        yb = _dot(at, wmla_ref[:, cs])
        mix_ref[:, cs] = (ga_ref[:, cs].astype(F32) * ya + gb_ref[:, cs].astype(F32) * yb).astype(BF16)
    y = _dot(mix_ref[...], wout_ref[...])
    o_ref[...] = _layer_norm(ALPHA * x_ref[...] + y, g_ref[...], b_ref[...])


def mixer_out(x, na, at, ga, gb, w_na_o, w_mla_o, w_out, g, bb):
    b, n, d = x.shape
    tm = FFN_TILE
    tok = lambda c: pl.BlockSpec((None, tm, c), lambda bi, i: (bi, i, 0))
    return pl.pallas_call(
        _mixer_out_kernel,
        out_shape=jax.ShapeDtypeStruct((b, n, d), F32),
        grid=(b, n // tm),
        in_specs=[tok(d), tok(NA_WIDTH),
                  pl.BlockSpec((None, at.shape[1], tm), lambda bi, i: (bi, 0, i)),
                  tok(d), tok(d),
                  _const_spec(w_na_o.shape), _const_spec(w_mla_o.shape), _const_spec(w_out.shape),
                  _const_spec(g.shape), _const_spec(bb.shape)],
        out_specs=tok(d),
        scratch_shapes=[pltpu.VMEM((tm, d), BF16)],
        compiler_params=pltpu.CompilerParams(dimension_semantics=("arbitrary", "arbitrary"),
                                             vmem_limit_bytes=VMEM_LIMIT),
        name="mixer_out",
    )(x, na, at, ga, gb, w_na_o, w_mla_o, w_out, g, bb)


def _prep_mixer_weights(w_in, b_gate, q_norm_g, kv_norm_g, w_uq, w_ukv):
    d = w_in.shape[0]
    half = MLA_ROPE // 2
    swap = lambda w: jnp.concatenate([-w[..., half:], w[..., :half]], axis=-1)
    pad_rope = lambda w: jnp.pad(w, ((0, 0), (MLA_NOPE, QK_PAD - MLA_QK)))
    c_kr = 3 * NA_WIDTH + MLA_Q_RANK + MLA_KV_RANK
    w_kr = w_in[:, c_kr:c_kr + MLA_ROPE]
    w_packed = jnp.concatenate([w_in[:, :c_kr], pad_rope(w_kr), pad_rope(swap(w_kr)),
                                w_in[:, c_kr + MLA_ROPE:]], axis=1)
    ukv = w_ukv.reshape(MLA_KV_RANK, MLA_HEADS, MLA_NOPE + MLA_V)
    wk = jnp.pad(ukv[..., :MLA_NOPE], ((0, 0), (0, 0), (0, QK_PAD - MLA_NOPE)))
    wk = wk.reshape(MLA_KV_RANK, MLA_HEADS * QK_PAD)
    wvt = ukv[..., MLA_NOPE:].reshape(MLA_KV_RANK, MLA_HEADS * MLA_V).T
    uq = w_uq.reshape(MLA_Q_RANK, MLA_HEADS, MLA_QK)
    wqt = jnp.pad(uq, ((0, 0), (0, 0), (0, QK_PAD - MLA_QK))).reshape(MLA_Q_RANK, MLA_HEADS * QK_PAD).T
    wqs = swap(uq[..., MLA_NOPE:]).reshape(MLA_Q_RANK, MLA_HEADS * MLA_ROPE).T
    return {
        "w_in": w_packed.astype(BF16), "b_gate": b_gate.reshape(1, 2 * d),
        "q_norm_g": q_norm_g.reshape(1, -1), "kv_norm_g": kv_norm_g.reshape(1, -1),
        "wk": wk.astype(BF16), "wvt": wvt.astype(BF16), "wqt": wqt.astype(BF16), "wqs": wqs.astype(BF16),
    }


def _rope_tables(n):
    inv = 1.0 / (ROPE_BASE ** (jnp.arange(0, MLA_ROPE, 2, dtype=F32) / MLA_ROPE))
    ang = jnp.arange(n, dtype=F32)[:, None] * inv[None, :]
    cos2 = jnp.tile(jnp.cos(ang), (1, 2))
    sin2 = jnp.tile(jnp.sin(ang), (1, 2))
    pad = ((0, 0), (MLA_NOPE, QK_PAD - MLA_QK))
    return {"ck": jnp.pad(cos2, pad), "sk": jnp.pad(sin2, pad), "cq": cos2.T, "sq": sin2.T}


def _na_bias_table(rpb):
    qc = np.arange(GRID_W)[:, None]
    kc = np.arange(GRID_W)[None, :]
    dc = np.clip(kc - qc + NA_KW - 1, 0, 2 * NA_KW - 2)
    onehot = (dc[None] == np.arange(2 * NA_KW - 1)[:, None, None]).astype(np.float32)
    win = np.clip(qc - NA_KW // 2, 0, GRID_W - NA_KW)
    in_win = (kc >= win) & (kc < win + NA_KW)
    t = jnp.einsum("hrc,cqk->hqrk", rpb * LOG2E, jnp.asarray(onehot), precision=lax.Precision.HIGHEST)
    t = jnp.where(in_win[:, None, :], t, NEG_BIG)
    bias = jnp.stack([t[:, :, NA_KH - 1 - dl:2 * NA_KH - 1 - dl] for dl in range(NA_KH)])
    return bias.reshape(NA_KH, NA_HEADS // 2, 2 * GRID_W, NA_KH * GRID_W)


def _encoder_layer(x, p, tabs):
    b, n, d = x.shape
    x1 = ffn_ln(x.reshape(b * n, d), p["ffn1_w_in"], p["ffn1_w_out"], p["ln1_g"], p["ln1_b"])
    x1 = x1.reshape(b, n, d)
    naq, nak, nav, qt, k, vt, ga, gb, qn2, lb, kn2 = mixer_in(x1, p["mixer"], tabs)
    na = na_attention(naq, nak, nav, p["na_bias"])
    at = mla_attention(qt, k, vt, qn2, lb, kn2).reshape(b, MLA_HEADS * MLA_V, n)
    x2 = mixer_out(x1, na, at, ga, gb, p["w_na_o"], p["w_mla_o"], p["w_out"], p["ln2_g"], p["ln2_b"])
    y = ffn_ln(x2.reshape(b * n, d), p["ffn2_w_in"], p["ffn2_w_out"], p["ln3_g"], p["ln3_b"])
    return y.reshape(b, n, d)


def kernel(x_prompt, x_sample, ffn1_w_in, ffn1_w_out, ln1_g, ln1_b, w_in, b_gate, na_rpb, q_norm_g, kv_norm_g, w_uq, w_ukv, w_na_o, w_mla_o, w_out, ln2_g, ln2_b, ffn2_w_in, ffn2_w_out, ln3_g, ln3_b):
    assert ffn1_w_in.shape[0] == DEPTH

    def layer_params(l):
        row = lambda a: a[l].reshape(1, -1)
        return {
            "ffn1_w_in": ffn1_w_in[l].astype(BF16), "ffn1_w_out": ffn1_w_out[l].astype(BF16),
            "ln1_g": row(ln1_g), "ln1_b": row(ln1_b),
            "mixer": _prep_mixer_weights(w_in[l], b_gate[l], q_norm_g[l], kv_norm_g[l], w_uq[l], w_ukv[l]),
            "na_bias": _na_bias_table(na_rpb[l]),
            "w_na_o": w_na_o[l].astype(BF16), "w_mla_o": w_mla_o[l].astype(BF16), "w_out": w_out[l].astype(BF16),
            "ln2_g": row(ln2_g), "ln2_b": row(ln2_b),
            "ffn2_w_in": ffn2_w_in[l].astype(BF16), "ffn2_w_out": ffn2_w_out[l].astype(BF16),
            "ln3_g": row(ln3_g), "ln3_b": row(ln3_b),
        }

    layers = [layer_params(l) for l in range(DEPTH)]
    outs = []
    for x in (x_prompt, x_sample):
        tabs = _rope_tables(x.shape[1])
        for p in layers:
            x = _encoder_layer(x, p, tabs)
        outs.append(x)
    return tuple(outs)
```

```python
import functools
import math

import numpy as np
import jax
import jax.numpy as jnp
from jax import lax
from jax.experimental import pallas as pl
from jax.experimental.pallas import tpu as pltpu

F32 = jnp.float32
BF16 = jnp.bfloat16

DEPTH = 1
GRID_W = 64
NA_HEADS = 8
NA_HEAD_DIM = 64
NA_WIDTH = NA_HEADS * NA_HEAD_DIM
NA_KH = 8
NA_KW = 16
MLA_HEADS = 8
MLA_NOPE = 64
MLA_ROPE = 32
MLA_QK = MLA_NOPE + MLA_ROPE
MLA_V = 64
MLA_VA = MLA_V + 16
MLA_Q_RANK = 384
MLA_KV_RANK = 256
ROPE_BASE = 10000.0
LN_EPS = 1e-5
RMS_EPS = 1e-6
ALPHA = (2.0 * DEPTH) ** 0.25
LOG2E = math.log2(math.e)

LANES = 128
QK_PAD = 128
VMEM_LIMIT = 56 * 1024 * 1024

TOKEN_TILE = 1024
FFN_TILE = 1024
FFN_CHUNK = 256
MIX_CHUNK = 256
NA_ROWS_PER_STEP = 32
NA_BLOCKS_PER_ITER = 6
MLA_TQ = 512
MLA_TK = 512
MLA_GROUP = 2
MLA_NBUF = 2
MLA_BLOCKS_PER_ITER = 10
MLA_FAST_BLOCKS_PER_ITER = (30, 10, 2)
MLA_PROBE_KEYS = 16
MLA_BOUND_INFLATE = 1.02
MLA_MAX_GAP = 64.0
NEG_BIG = -1e30


def _const_spec(shape):
    nd = len(shape)
    return pl.BlockSpec(shape, lambda *_: (0,) * nd, pipeline_mode=pl.Buffered(1))


def _layer_norm(y, g, b):
    mu = jnp.mean(y, axis=-1, keepdims=True)
    d = y - mu
    var = jnp.mean(d * d, axis=-1, keepdims=True)
    return d * lax.rsqrt(var + LN_EPS) * g + b


def _rms_norm(y, g):
    return y * lax.rsqrt(jnp.mean(y * y, axis=-1, keepdims=True) + RMS_EPS) * g


def _dot(a, b):
    return jnp.dot(a, b, preferred_element_type=F32)


def _swiglu_ln(x, w_in_ref, w_out_ref, g_ref, b_ref, h_ref, d_ff):
    xb = x.astype(BF16)
    for c in range(d_ff // FFN_CHUNK):
        lo, hi = c * FFN_CHUNK, (c + 1) * FFN_CHUNK
        a = _dot(xb, w_in_ref[:, lo:hi])
        u = _dot(xb, w_in_ref[:, d_ff + lo:d_ff + hi])
        h_ref[:, lo:hi] = (a * jax.nn.sigmoid(a) * u).astype(BF16)
    y = _dot(h_ref[...], w_out_ref[...])
    return _layer_norm(ALPHA * x + 0.5 * y, g_ref[...], b_ref[...])


def _ffn_ln_kernel(x_ref, w_in_ref, w_out_ref, g_ref, b_ref, o_ref, h_ref, *, d_ff):
    o_ref[...] = _swiglu_ln(x_ref[...], w_in_ref, w_out_ref, g_ref, b_ref, h_ref, d_ff)


def ffn_ln(x, w_in, w_out, g, b):
    t, d = x.shape
    d_ff = w_out.shape[0]
    tm = FFN_TILE
    return pl.pallas_call(
        functools.partial(_ffn_ln_kernel, d_ff=d_ff),
        out_shape=jax.ShapeDtypeStruct((t, d), F32),
        grid=(t // tm,),
        in_specs=[pl.BlockSpec((tm, d), lambda i: (i, 0)),
                  _const_spec(w_in.shape), _const_spec(w_out.shape),
                  _const_spec(g.shape), _const_spec(b.shape)],
        out_specs=pl.BlockSpec((tm, d), lambda i: (i, 0)),
        scratch_shapes=[pltpu.VMEM((tm, d_ff), BF16)],
        compiler_params=pltpu.CompilerParams(dimension_semantics=("arbitrary",),
                                             vmem_limit_bytes=VMEM_LIMIT),
        name="ffn_ln",
    )(x, w_in, w_out, g, b)


_C_NAQ, _C_NAK, _C_NAV = 0, NA_WIDTH, 2 * NA_WIDTH
_C_CQ = 3 * NA_WIDTH
_C_CKV = _C_CQ + MLA_Q_RANK
_C_KR = _C_CKV + MLA_KV_RANK
_C_GATE = _C_KR + 2 * QK_PAD


def _mixer_in_kernel(x_ref, w_ref, bg_ref, qg_ref, kvg_ref, wk_ref, wvt_ref, wqt_ref, wqs_ref,
                     ck_ref, sk_ref, cq_ref, sq_ref,
                     naq_ref, nak_ref, nav_ref, qt_ref, k_ref, vt_ref, ga_ref, gb_ref,
                     qn2_ref, lb_ref, kn2_ref, *, d_model):
    xb = x_ref[...].astype(BF16)
    naq_ref[...] = (_dot(xb, w_ref[:, _C_NAQ:_C_NAK]) * (NA_HEAD_DIM ** -0.5 * LOG2E)).astype(BF16)
    nak_ref[...] = _dot(xb, w_ref[:, _C_NAK:_C_NAV]).astype(BF16)
    nav_ref[...] = _dot(xb, w_ref[:, _C_NAV:_C_CQ]).astype(BF16)

    cqn = _rms_norm(_dot(xb, w_ref[:, _C_CQ:_C_CKV]), qg_ref[...])
    ckvn = _rms_norm(_dot(xb, w_ref[:, _C_CKV:_C_KR]), kvg_ref[...])

    kr = _dot(xb, w_ref[:, _C_KR:_C_GATE])
    kr_blk = kr[:, :QK_PAD] * ck_ref[...] + kr[:, QK_PAD:] * sk_ref[...]
    kall = _dot(ckvn.astype(BF16), wk_ref[...])
    k_probe = []
    one_hot = (lax.broadcasted_iota(jnp.int32, kr_blk.shape, 1) == MLA_QK).astype(F32)
    for h in range(MLA_HEADS):
        kf = kall[:, h * QK_PAD:(h + 1) * QK_PAD] + kr_blk
        kb = (kf + one_hot).astype(BF16)
        k_ref[h] = kb
        k2 = jnp.max(jnp.sum(kf * kf, axis=1, keepdims=True), axis=0, keepdims=True)
        kn2_ref[h] = jnp.broadcast_to(k2, (1, LANES))
        k_probe.append(kb[0:MLA_PROBE_KEYS])

    ckvn_t = ckvn.T.astype(BF16)
    vt = _dot(wvt_ref[...], ckvn_t)
    ones = jnp.ones((MLA_VA - MLA_V, vt.shape[1]), BF16)
    for h in range(MLA_HEADS):
        vt_ref[h, 0:MLA_V] = vt[h * MLA_V:(h + 1) * MLA_V].astype(BF16)
        vt_ref[h, MLA_V:MLA_VA] = ones

    cqn_t = cqn.T.astype(BF16)
    q_scale = (MLA_QK ** -0.5) * LOG2E
    qt = _dot(wqt_ref[...], cqn_t) * q_scale
    qs = _dot(wqs_ref[...], cqn_t) * q_scale
    cq, sq = cq_ref[...], sq_ref[...]
    zeros = jnp.zeros((QK_PAD - MLA_QK, qt.shape[1]), BF16)
    for h in range(MLA_HEADS):
        base = h * QK_PAD
        nope = qt[base:base + MLA_NOPE]
        rope = qt[base + MLA_NOPE:base + MLA_QK] * cq + qs[h * MLA_ROPE:(h + 1) * MLA_ROPE] * sq
        qh = jnp.concatenate([nope.astype(BF16), rope.astype(BF16), zeros], axis=0)
        qt_ref[h] = qh
        qn2_ref[h] = (jnp.sum(nope * nope, axis=0, keepdims=True)
                      + jnp.sum(rope * rope, axis=0, keepdims=True))
        lb_ref[h] = jnp.max(_dot(k_probe[h], qh), axis=0, keepdims=True)

    for half, out_ref in enumerate((ga_ref, gb_ref)):
        lo = _C_GATE + half * d_model
        logits = _dot(xb, w_ref[:, lo:lo + d_model]) + bg_ref[:, half * d_model:(half + 1) * d_model]
        out_ref[...] = jax.nn.sigmoid(logits).astype(BF16)


def mixer_in(x, wts, tabs):
    b, n, d = x.shape
    tm = TOKEN_TILE
    h = MLA_HEADS
    tok = lambda c: pl.BlockSpec((None, tm, c), lambda bi, i: (bi, i, 0))
    out_shape = (
        jax.ShapeDtypeStruct((b, n, NA_WIDTH), BF16), jax.ShapeDtypeStruct((b, n, NA_WIDTH), BF16),
        jax.ShapeDtypeStruct((b, n, NA_WIDTH), BF16),
        jax.ShapeDtypeStruct((b, h, QK_PAD, n), BF16),
        jax.ShapeDtypeStruct((b, h, n, QK_PAD), BF16),
        jax.ShapeDtypeStruct((b, h, MLA_VA, n), BF16),
        jax.ShapeDtypeStruct((b, n, d), BF16), jax.ShapeDtypeStruct((b, n, d), BF16),
        jax.ShapeDtypeStruct((b, h, 1, n), F32), jax.ShapeDtypeStruct((b, h, 1, n), F32),
        jax.ShapeDtypeStruct((b, n // tm, h, 1, LANES), F32),
    )
    stat = pl.BlockSpec((None, h, 1, tm), lambda bi, i: (bi, 0, 0, i))
    out_specs = (
        tok(NA_WIDTH), tok(NA_WIDTH), tok(NA_WIDTH),
        pl.BlockSpec((None, h, QK_PAD, tm), lambda bi, i: (bi, 0, 0, i)),
        pl.BlockSpec((None, h, tm, QK_PAD), lambda bi, i: (bi, 0, i, 0)),
        pl.BlockSpec((None, h, MLA_VA, tm), lambda bi, i: (bi, 0, 0, i)),
        tok(d), tok(d),
        stat, stat,
        pl.BlockSpec((None, None, h, 1, LANES), lambda bi, i: (bi, i, 0, 0, 0)),
    )
    consts = [wts["w_in"], wts["b_gate"], wts["q_norm_g"], wts["kv_norm_g"],
              wts["wk"], wts["wvt"], wts["wqt"], wts["wqs"]]
    in_specs = ([tok(d)] + [_const_spec(c.shape) for c in consts] + [
        pl.BlockSpec((tm, QK_PAD), lambda bi, i: (i, 0)),
        pl.BlockSpec((tm, QK_PAD), lambda bi, i: (i, 0)),
        pl.BlockSpec((MLA_ROPE, tm), lambda bi, i: (0, i)),
        pl.BlockSpec((MLA_ROPE, tm), lambda bi, i: (0, i)),
    ])
    return pl.pallas_call(
        functools.partial(_mixer_in_kernel, d_model=d),
        out_shape=out_shape,
        grid=(b, n // tm),
        in_specs=in_specs,
        out_specs=out_specs,
        compiler_params=pltpu.CompilerParams(dimension_semantics=("arbitrary", "arbitrary"),
                                             vmem_limit_bytes=VMEM_LIMIT),
        name="mixer_in",
    )(x, *consts, tabs["ck"], tabs["sk"], tabs["cq"], tabs["sq"])


def _na_first_key_row(i, rows):
    g = NA_ROWS_PER_STEP
    return jnp.clip(i * g - NA_KH, 0, rows - (g + 2 * NA_KH))


def _na_kernel(q_ref, k_ref, v_ref, bias_ref, o_ref, s0, s1, p0, p1, r0, r1, *, rows):
    g = NA_ROWS_PER_STEP
    window = NA_KH * GRID_W
    npair = NA_HEADS // 2
    i = pl.program_id(1)
    base_row = _na_first_key_row(i, rows)
    kbuf, vbuf = k_ref.at[0], v_ref.at[0]
    lane = lax.broadcasted_iota(jnp.int32, (GRID_W, LANES), 1)
    first_head = lane < NA_HEAD_DIM
    s_bufs, p_bufs, r_bufs = (s0, s1), (p0, p1), (r0, r1)
    lanes_of = lambda hp: slice(hp * LANES, (hp + 1) * LANES)

    def geometry(rho):
        r = i * g + rho
        rs = jnp.clip(r - NA_KH // 2, 0, rows - NA_KH)
        off = pl.multiple_of((rs - base_row) * GRID_W, GRID_W)
        return r - rs, off, pl.multiple_of(rho * GRID_W, GRID_W)

    def scores(rho, slot):
        delta, off, qoff = geometry(rho)
        for hp in range(npair):
            q2 = q_ref[pl.ds(qoff, GRID_W), lanes_of(hp)]
            zero = jnp.zeros_like(q2)
            qs = jnp.concatenate([jnp.where(first_head, q2, zero),
                                  jnp.where(first_head, zero, q2)], axis=0)
            k2 = kbuf[pl.ds(off, window), lanes_of(hp)]
            s = lax.dot_general(qs, k2, (((1,), (1,)), ((), ())), preferred_element_type=F32)
            s_bufs[slot][hp] = s + bias_ref[delta, hp]

    def softmax(slot):
        for hp in range(npair):
            for rows_ in (slice(0, GRID_W), slice(GRID_W, 2 * GRID_W)):
                s = s_bufs[slot][hp, rows_]
                p = jnp.exp2(s - jnp.max(s, axis=1, keepdims=True))
                p_bufs[slot][hp, rows_] = p.astype(BF16)
                r_bufs[slot][hp, rows_] = jnp.broadcast_to(1.0 / jnp.sum(p, axis=1, keepdims=True),
                                                           (GRID_W, LANES))

    def values(rho, slot):
        _, off, qoff = geometry(rho)
        for hp in range(npair):
            o = _dot(p_bufs[slot][hp], vbuf[pl.ds(off, window), lanes_of(hp)])
            o = o * r_bufs[slot][hp]
            out2 = jnp.where(first_head, o[:GRID_W], o[GRID_W:])
            o_ref[pl.ds(qoff, GRID_W), lanes_of(hp)] = out2.astype(BF16)

    def block(t, slot, do_scores=True, do_softmax=True, do_values=True):
        if do_scores:
            scores(t + 1, 1 - slot)
        if do_softmax:
            softmax(slot)
        if do_values:
            values(t - 1, 1 - slot)

    assert g % 2 == 0 and g >= 4
    block(-1, 1, do_softmax=False, do_values=False)
    block(0, 0, do_values=False)

    unroll = NA_BLOCKS_PER_ITER
    assert unroll % 2 == 0 and (g - 2) % unroll == 0

    def blocks(j, carry):
        for kk in range(unroll):
            block(unroll * j + 1 + kk, (1 + kk) % 2)
        return carry

    lax.fori_loop(0, (g - 2) // unroll, blocks, 0)
    block(g - 1, 1, do_scores=False)
    block(g, 0, do_scores=False, do_softmax=False)


def na_attention(q, k, v, bias):
    b, n, c = q.shape
    rows = n // GRID_W
    g = NA_ROWS_PER_STEP
    main = g * GRID_W
    halo = NA_KH * GRID_W
    assert rows >= g + 2 * NA_KH
    spec_main = pl.BlockSpec((None, main, c), lambda bi, i: (bi, i, 0))
    spec_kv = pl.BlockSpec((pl.Element(1), pl.Element(main + 2 * halo), pl.Element(c)),
                           lambda bi, i: (bi, _na_first_key_row(i, rows) * GRID_W, 0))
    return pl.pallas_call(
        functools.partial(_na_kernel, rows=rows),
        out_shape=jax.ShapeDtypeStruct((b, n, c), BF16),
        grid=(b, rows // g),
        in_specs=[spec_main, spec_kv, spec_kv, _const_spec(bias.shape)],
        out_specs=spec_main,
        scratch_shapes=[pltpu.VMEM(bias.shape[1:], F32)] * 2 + [pltpu.VMEM(bias.shape[1:], BF16)] * 2
        + [pltpu.VMEM(bias.shape[1:3] + (LANES,), F32)] * 2,
        compiler_params=pltpu.CompilerParams(dimension_semantics=("arbitrary", "arbitrary"),
                                             vmem_limit_bytes=VMEM_LIMIT),
        name="na_attention",
    )(q, k, v, bias)


def _mla_kernel(qt_ref, k_ref, vt_ref, o_ref, *bufs, n):
    tq, tk, g = MLA_TQ, MLA_TK, MLA_GROUP
    ng = n // (tk * g)
    total = (n // tq) * ng
    nbuf = MLA_NBUF
    assert total >= 4
    s_bufs, p_bufs = bufs[:nbuf], bufs[nbuf:]

    def split(u):
        return lax.div(u, ng), lax.rem(u, ng)

    def scores_chunk(u, slot, c):
        qi, kg = split(u)
        qt = qt_ref[:, pl.ds(pl.multiple_of(qi * tq, tq), tq)]
        off = pl.multiple_of((kg * g + c) * tk, tk)
        s = _dot(k_ref[pl.ds(off, tk), :], qt)
        s_bufs[slot][c * tk:(c + 1) * tk] = s
        return jnp.max(s, axis=0, keepdims=True)

    def first_group_reset(u, m):
        _, kg = split(u)
        return jnp.where(kg == 0, -jnp.inf, m)

    def softmax_chunk(slot, c, m, mx):
        m_new = jnp.maximum(m, mx)
        p_bufs[slot][c * tk:(c + 1) * tk] = jnp.exp2(s_bufs[slot][c * tk:(c + 1) * tk] - m_new).astype(BF16)
        return m_new, jnp.exp2(m - m_new)

    def values_chunk(u, slot, c, r, acc):
        _, kg = split(u)
        off = pl.multiple_of((kg * g + c) * tk, tk)
        return r * acc + _dot(vt_ref[:, pl.ds(off, tk)], p_bufs[slot][c * tk:(c + 1) * tk])

    def write_out(u, acc):
        qi, _ = split(u)
        out = acc[:MLA_V] * (1.0 / acc[MLA_V:MLA_V + 1])
        o_ref[:, pl.ds(pl.multiple_of(qi * tq, tq), tq)] = out.astype(BF16)

    def block(t, slot, carry, do_scores=True, do_softmax=True, do_values=True):
        m, mxs, rs_prev, acc = carry
        if do_softmax:
            m = first_group_reset(t, m)
        mxs_next, rs = list(mxs), list(rs_prev)
        if do_scores:
            for c in range(g):
                mxs_next[c] = scores_chunk(t + 1, (slot + 1) % nbuf, c)
        if do_softmax:
            for c in range(g):
                m, rs[c] = softmax_chunk(slot, c, m, mxs[c])
        if do_values:
            for c in range(g):
                acc = values_chunk(t - 1, (slot - 1) % nbuf, c, rs_prev[c], acc)
            write_out(t - 1, acc)
        return m, tuple(mxs_next), tuple(rs), acc

    row = jnp.zeros((1, tq), F32)
    carry = (row, (row,) * g, (row,) * g, jnp.zeros((MLA_VA, tq), F32))
    carry = block(-1, -1 % nbuf, carry, do_softmax=False, do_values=False)
    carry = block(0, 0, carry, do_values=False)

    unroll = MLA_BLOCKS_PER_ITER
    assert unroll % nbuf == 0
    n_iter, n_rest = divmod(total - 2, unroll)

    def blocks(i, carry):
        t0 = unroll * i + 1
        for kk in range(unroll):
            carry = block(t0 + kk, (1 + kk) % nbuf, carry)
        return carry

    carry = lax.fori_loop(0, n_iter, blocks, carry)
    for t in range(n_iter * unroll + 1, n_iter * unroll + 1 + n_rest):
        carry = block(t, t % nbuf, carry)
    carry = block(total - 1, (total - 1) % nbuf, carry, do_scores=False)
    block(total, total % nbuf, carry, do_scores=False, do_softmax=False)


def _mla_fast_kernel(qt_ref, k_ref, vt_ref, m_ref, o_ref, p0, p1, *, n):
    tq, tk, g = MLA_TQ, MLA_TK, MLA_GROUP
    ng = n // (tk * g)
    total = (n // tq) * ng
    p_bufs = (p0, p1)

    def split(u):
        return lax.div(u, ng), lax.rem(u, ng)

    def probs(u, slot, l):
        qi, kg = split(u)
        qoff = pl.multiple_of(qi * tq, tq)
        qt = qt_ref[:, pl.ds(qoff, tq)]
        m = m_ref[:, pl.ds(qoff, tq)]
        feat = lax.broadcasted_iota(jnp.int32, (QK_PAD, tq), 0)
        qt = jnp.where(feat == MLA_QK, -m, qt.astype(F32)).astype(BF16)
        l = jnp.where(kg == 0, 0.0, l)
        for c in range(g):
            off = pl.multiple_of((kg * g + c) * tk, tk)
            p = jnp.exp2(_dot(k_ref[pl.ds(off, tk), :], qt))
            p_bufs[slot][c * tk:(c + 1) * tk] = p.astype(BF16)
            l = l + jnp.sum(p, axis=0, keepdims=True)
        return l

    def values(u, slot, acc, l):
        qi, kg = split(u)
        acc = jnp.where(kg == 0, 0.0, acc)
        for c in range(g):
            off = pl.multiple_of((kg * g + c) * tk, tk)
            acc = acc + _dot(vt_ref[0:MLA_V, pl.ds(off, tk)], p_bufs[slot][c * tk:(c + 1) * tk])
        o_ref[:, pl.ds(pl.multiple_of(qi * tq, tq), tq)] = (acc * (1.0 / l)).astype(BF16)
        return acc

    def block(t, slot, carry):
        acc, l = carry
        l_next = probs(t + 1, 1 - slot, l)
        return values(t, slot, acc, l), l_next

    unroll = next(u for u in MLA_FAST_BLOCKS_PER_ITER if (total - 2) // u >= 2 or u == 2)
    assert total >= 2
    carry = (jnp.zeros((MLA_V, tq), F32), probs(0, 0, jnp.zeros((1, tq), F32)))
    carry = block(0, 0, carry)
    n_iter, n_rest = divmod(total - 2, unroll)

    def blocks(i, carry):
        t0 = unroll * i + 1
        for kk in range(unroll):
            carry = block(t0 + kk, (1 + kk) % 2, carry)
        return carry

    carry = lax.fori_loop(0, n_iter, blocks, carry)
    for t in range(n_iter * unroll + 1, total - 1):
        carry = block(t, t % 2, carry)
    values(total - 1, (total - 1) % 2, *carry)


def mla_attention(qt, k, vt, qn2, lb, kn2):
    b, h, _, n = qt.shape
    rows = MLA_GROUP * MLA_TK
    head = lambda r, c: pl.BlockSpec((None, None, r, c), lambda bi, hi: (bi, hi, 0, 0))
    params = pltpu.CompilerParams(dimension_semantics=("arbitrary",) * 2, vmem_limit_bytes=VMEM_LIMIT)
    out_shape = jax.ShapeDtypeStruct((b, h, MLA_V, n), BF16)

    k2max = jnp.max(kn2, axis=(1, 3, 4))
    m = (jnp.sqrt(qn2 * k2max[:, :, None, None]) * MLA_BOUND_INFLATE).astype(BF16).astype(F32)
    gap = m - lb

    def fast(qt, k, vt, m):
        return pl.pallas_call(
            functools.partial(_mla_fast_kernel, n=n),
            out_shape=out_shape,
            grid=(b, h),
            in_specs=[head(QK_PAD, n), head(n, QK_PAD), head(MLA_VA, n), head(1, n)],
            out_specs=head(MLA_V, n),
            scratch_shapes=[pltpu.VMEM((rows, MLA_TQ), BF16)] * 2,
            compiler_params=params,
            name="mla_attention_fast",
        )(qt, k, vt, m)

    def exact(qt, k, vt, m):
        return pl.pallas_call(
            functools.partial(_mla_kernel, n=n),
            out_shape=out_shape,
            grid=(b, h),
            in_specs=[head(QK_PAD, n), head(n, QK_PAD), head(MLA_VA, n)],
            out_specs=head(MLA_V, n),
            scratch_shapes=[pltpu.VMEM((rows, MLA_TQ), F32)] * MLA_NBUF + [pltpu.VMEM((rows, MLA_TQ), BF16)] * MLA_NBUF,
            compiler_params=params,
            name="mla_attention",
        )(qt, k, vt)

    return lax.cond(jnp.max(gap) < MLA_MAX_GAP, fast, exact, qt, k, vt, m)


def _mixer_out_kernel(x_ref, na_ref, at_ref, ga_ref, gb_ref, wna_ref, wmla_ref, wout_ref, g_ref, b_ref,
                      o_ref, mix_ref):
    na = na_ref[...]
    at = at_ref[...].T
    d = o_ref.shape[1]
    for c in range(d // MIX_CHUNK):
        cs = slice(c * MIX_CHUNK, (c + 1) * MIX_CHUNK)
        ya = _dot(na, wna_ref[:, cs])
        yb = _dot(at, wmla_ref[:, cs])
        mix_ref[:, cs] = (ga_ref[:, cs].astype(F32) * ya + gb_ref[:, cs].astype(F32) * yb).astype(BF16)
    y = _dot(mix_ref[...], wout_ref[...])
    o_ref[...] = _layer_norm(ALPHA * x_ref[...] + y, g_ref[...], b_ref[...])


def mixer_out(x, na, at, ga, gb, w_na_o, w_mla_o, w_out, g, bb):
    b, n, d = x.shape
    tm = FFN_TILE
    tok = lambda c: pl.BlockSpec((None, tm, c), lambda bi, i: (bi, i, 0))
    return pl.pallas_call(
        _mixer_out_kernel,
        out_shape=jax.ShapeDtypeStruct((b, n, d), F32),
        grid=(b, n // tm),
        in_specs=[tok(d), tok(NA_WIDTH),
                  pl.BlockSpec((None, at.shape[1], tm), lambda bi, i: (bi, 0, i)),
                  tok(d), tok(d),
                  _const_spec(w_na_o.shape), _const_spec(w_mla_o.shape), _const_spec(w_out.shape),
                  _const_spec(g.shape), _const_spec(bb.shape)],
        out_specs=tok(d),
        scratch_shapes=[pltpu.VMEM((tm, d), BF16)],
        compiler_params=pltpu.CompilerParams(dimension_semantics=("arbitrary", "arbitrary"),
                                             vmem_limit_bytes=VMEM_LIMIT),
        name="mixer_out",
    )(x, na, at, ga, gb, w_na_o, w_mla_o, w_out, g, bb)


def _prep_mixer_weights(w_in, b_gate, q_norm_g, kv_norm_g, w_uq, w_ukv):
    d = w_in.shape[0]
    half = MLA_ROPE // 2
    swap = lambda w: jnp.concatenate([-w[..., half:], w[..., :half]], axis=-1)
    pad_rope = lambda w: jnp.pad(w, ((0, 0), (MLA_NOPE, QK_PAD - MLA_QK)))
    c_kr = 3 * NA_WIDTH + MLA_Q_RANK + MLA_KV_RANK
    w_kr = w_in[:, c_kr:c_kr + MLA_ROPE]
    w_packed = jnp.concatenate([w_in[:, :c_kr], pad_rope(w_kr), pad_rope(swap(w_kr)),
                                w_in[:, c_kr + MLA_ROPE:]], axis=1)
    ukv = w_ukv.reshape(MLA_KV_RANK, MLA_HEADS, MLA_NOPE + MLA_V)
    wk = jnp.pad(ukv[..., :MLA_NOPE], ((0, 0), (0, 0), (0, QK_PAD - MLA_NOPE)))
    wk = wk.reshape(MLA_KV_RANK, MLA_HEADS * QK_PAD)
    wvt = ukv[..., MLA_NOPE:].reshape(MLA_KV_RANK, MLA_HEADS * MLA_V).T
    uq = w_uq.reshape(MLA_Q_RANK, MLA_HEADS, MLA_QK)
    wqt = jnp.pad(uq, ((0, 0), (0, 0), (0, QK_PAD - MLA_QK))).reshape(MLA_Q_RANK, MLA_HEADS * QK_PAD).T
    wqs = swap(uq[..., MLA_NOPE:]).reshape(MLA_Q_RANK, MLA_HEADS * MLA_ROPE).T
    return {
        "w_in": w_packed.astype(BF16), "b_gate": b_gate.reshape(1, 2 * d),
        "q_norm_g": q_norm_g.reshape(1, -1), "kv_norm_g": kv_norm_g.reshape(1, -1),
        "wk": wk.astype(BF16), "wvt": wvt.astype(BF16), "wqt": wqt.astype(BF16), "wqs": wqs.astype(BF16),
    }


def _rope_tables(n):
    inv = 1.0 / (ROPE_BASE ** (jnp.arange(0, MLA_ROPE, 2, dtype=F32) / MLA_ROPE))
    ang = jnp.arange(n, dtype=F32)[:, None] * inv[None, :]
    cos2 = jnp.tile(jnp.cos(ang), (1, 2))
    sin2 = jnp.tile(jnp.sin(ang), (1, 2))
    pad = ((0, 0), (MLA_NOPE, QK_PAD - MLA_QK))
    return {"ck": jnp.pad(cos2, pad), "sk": jnp.pad(sin2, pad), "cq": cos2.T, "sq": sin2.T}


def _na_bias_table(rpb):
    qc = np.arange(GRID_W)[:, None]
    kc = np.arange(GRID_W)[None, :]
    dc = np.clip(kc - qc + NA_KW - 1, 0, 2 * NA_KW - 2)
    onehot = (dc[None] == np.arange(2 * NA_KW - 1)[:, None, None]).astype(np.float32)
    win = np.clip(qc - NA_KW // 2, 0, GRID_W - NA_KW)
    in_win = (kc >= win) & (kc < win + NA_KW)
    t = jnp.einsum("hrc,cqk->hqrk", rpb * LOG2E, jnp.asarray(onehot), precision=lax.Precision.HIGHEST)
    t = jnp.where(in_win[:, None, :], t, NEG_BIG)
    bias = jnp.stack([t[:, :, NA_KH - 1 - dl:2 * NA_KH - 1 - dl] for dl in range(NA_KH)])
    return bias.reshape(NA_KH, NA_HEADS // 2, 2 * GRID_W, NA_KH * GRID_W)


def _encoder_layer(x, p, tabs):
    b, n, d = x.shape
    x1 = ffn_ln(x.reshape(b * n, d), p["ffn1_w_in"], p["ffn1_w_out"], p["ln1_g"], p["ln1_b"])
    x1 = x1.reshape(b, n, d)
    naq, nak, nav, qt, k, vt, ga, gb, qn2, lb, kn2 = mixer_in(x1, p["mixer"], tabs)
    na = na_attention(naq, nak, nav, p["na_bias"])
    at = mla_attention(qt, k, vt, qn2, lb, kn2).reshape(b, MLA_HEADS * MLA_V, n)
    x2 = mixer_out(x1, na, at, ga, gb, p["w_na_o"], p["w_mla_o"], p["w_out"], p["ln2_g"], p["ln2_b"])
    y = ffn_ln(x2.reshape(b * n, d), p["ffn2_w_in"], p["ffn2_w_out"], p["ln3_g"], p["ln3_b"])
    return y.reshape(b, n, d)


def kernel(x_prompt, x_sample, ffn1_w_in, ffn1_w_out, ln1_g, ln1_b, w_in, b_gate, na_rpb, q_norm_g, kv_norm_g, w_uq, w_ukv, w_na_o, w_mla_o, w_out, ln2_g, ln2_b, ffn2_w_in, ffn2_w_out, ln3_g, ln3_b):
    assert ffn1_w_in.shape[0] == DEPTH

    def layer_params(l):
        row = lambda a: a[l].reshape(1, -1)
        return {
            "ffn1_w_in": ffn1_w_in[l].astype(BF16), "ffn1_w_out": ffn1_w_out[l].astype(BF16),
            "ln1_g": row(ln1_g), "ln1_b": row(ln1_b),
            "mixer": _prep_mixer_weights(w_in[l], b_gate[l], q_norm_g[l], kv_norm_g[l], w_uq[l], w_ukv[l]),
            "na_bias": _na_bias_table(na_rpb[l]),
            "w_na_o": w_na_o[l].astype(BF16), "w_mla_o": w_mla_o[l].astype(BF16), "w_out": w_out[l].astype(BF16),
            "ln2_g": row(ln2_g), "ln2_b": row(ln2_b),
            "ffn2_w_in": ffn2_w_in[l].astype(BF16), "ffn2_w_out": ffn2_w_out[l].astype(BF16),
            "ln3_g": row(ln3_g), "ln3_b": row(ln3_b),
        }

    layers = [layer_params(l) for l in range(DEPTH)]
    outs = []
    for x in (x_prompt, x_sample):
        tabs = _rope_tables(x.shape[1])
        for p in layers:
            x = _encoder_layer(x, p, tabs)
        outs.append(x)
    return tuple(outs)
```
